```python
import math
import jax
import jax.numpy as jnp
from jax import lax
import numpy as np

D_MODEL = 4096
BATCH = 2
SEQ = 8192
DEPTH = 2

CHUNK = 64
HEAD_DIM = 128
FOX_HEADS = 12
FOX_WIDTH = FOX_HEADS * HEAD_DIM
Q_BLOCK = 128
SC_GROUPS = 8
SC_WIDTH = SC_GROUPS * HEAD_DIM
SC_KERNEL = 3
GDN_HEADS = 12
GDN_WIDTH = GDN_HEADS * HEAD_DIM
GDN_CONV = 4
MIX_WIDTH = FOX_WIDTH + SC_WIDTH + GDN_WIDTH
PROJ_DIM = 3 * FOX_WIDTH + FOX_HEADS + 3 * SC_WIDTH + 4 * GDN_WIDTH + 2 * GDN_HEADS
FFN_HIDDEN = -(-8 * D_MODEL // 768) * 256
RMS_EPS = 1e-6

kernel_name = "hybrid_fox_shortconv_gdn_sandwich"


def _proj_split_points():
    sizes = (FOX_WIDTH, FOX_WIDTH, FOX_WIDTH, FOX_HEADS,
             SC_WIDTH, SC_WIDTH, SC_WIDTH,
             GDN_WIDTH, GDN_WIDTH, GDN_WIDTH, GDN_HEADS, GDN_HEADS, GDN_WIDTH)
    pts, acc = [], 0
    for s in sizes[:-1]:
        acc += s
        pts.append(acc)
    return pts


def rms_norm(x, gain):
    xf = x.astype(jnp.float32)
    y = xf * lax.rsqrt(jnp.mean(xf * xf, axis=-1, keepdims=True) + RMS_EPS)
    return (y * gain.astype(jnp.float32)).astype(x.dtype)


def l2_normalize(t):
    return t * lax.rsqrt(jnp.sum(t * t, axis=-1, keepdims=True) + RMS_EPS)


def causal_depthwise_conv(x, w):
    K = w.shape[0]
    S = x.shape[1]
    xp = jnp.pad(x, ((0, 0), (K - 1, 0), (0, 0)))
    y = w[0] * xp[:, 0:S]
    for j in range(1, K):
        y = y + w[j] * xp[:, j:j + S]
    return y


def forgetting_attention(q, k, v, f_logit, b_f, out_gain):
    B_, S, H, D = q.shape
    log_f = jax.nn.log_sigmoid(f_logit.astype(jnp.float32) + b_f.astype(jnp.float32))
    F = jnp.cumsum(log_f, axis=1).transpose(0, 2, 1)
    kh = k.transpose(0, 2, 1, 3)
    vh = v.transpose(0, 2, 1, 3)
    n_blocks = S // Q_BLOCK
    qb = q.transpose(0, 2, 1, 3).reshape(B_, H, n_blocks, Q_BLOCK, D).transpose(2, 0, 1, 3, 4)
    Fq = F.reshape(B_, H, n_blocks, Q_BLOCK).transpose(2, 0, 1, 3)
    scale = D ** -0.5
    k_pos = jnp.arange(S)

    def block(args):
        i, q_blk, F_blk = args
        s = jnp.einsum('bhqd,bhkd->bhqk', q_blk, kh).astype(jnp.float32) * scale
        s = s + F_blk[..., :, None] - F[:, :, None, :]
        q_pos = i * Q_BLOCK + jnp.arange(Q_BLOCK)
        s = jnp.where(k_pos[None, :] <= q_pos[:, None], s, -jnp.inf)
        p = jax.nn.softmax(s, axis=-1)
        return jnp.einsum('bhqk,bhkd->bhqd', p.astype(vh.dtype), vh)

    out = lax.map(block, (jnp.arange(n_blocks), qb, Fq))
    out = out.transpose(1, 0, 3, 2, 4).reshape(B_, S, H, D)
    out = rms_norm(out, out_gain)
    return out.reshape(B_, S, H * D)


def short_conv_mixer(b_gate, c_gate, h, conv_w):
    return b_gate * causal_depthwise_conv(c_gate * h, conv_w)


def gated_deltanet(q, k, v, a, b, z, conv_w, a_log, dt_bias, out_gain):
    f32 = jnp.float32
    qkv = jax.nn.silu(causal_depthwise_conv(jnp.concatenate([q, k, v], axis=-1), conv_w))
    q, k, v = jnp.split(qkv, 3, axis=-1)
    B_, S, _ = q.shape
    H, D = GDN_HEADS, HEAD_DIM
    N = S // CHUNK

    def heads(t):
        return t.reshape(B_, N, CHUNK, H, D).transpose(0, 3, 1, 2, 4).astype(f32)

    def per_head(t):
        return t.reshape(B_, N, CHUNK, H).transpose(0, 3, 1, 2)

    q = l2_normalize(heads(q)) * (D ** -0.5)
    k = l2_normalize(heads(k))
    v = heads(v)
    beta = per_head(jax.nn.sigmoid(b.astype(f32)))
    g = -jnp.exp(a_log.astype(f32)) * jax.nn.softplus(a.astype(f32) + dt_bias.astype(f32))
    g = jnp.cumsum(per_head(g), axis=-1)

    tri_incl = jnp.tril(jnp.ones((CHUNK, CHUNK), dtype=bool))
    tri_strict = jnp.tril(jnp.ones((CHUNK, CHUNK), dtype=bool), -1)
    decay = jnp.exp(jnp.where(tri_incl, g[..., :, None] - g[..., None, :], -jnp.inf))

    kk = jnp.einsum('bhncd,bhnsd->bhncs', k, k)
    L = jnp.where(tri_strict, beta[..., :, None] * kk * decay, 0.0)
    A = L + jnp.eye(CHUNK, dtype=f32)
    rhs = jnp.concatenate([v * beta[..., None], k * (beta * jnp.exp(g))[..., None]], axis=-1)
    sol = lax.linalg.triangular_solve(A, rhs, left_side=True, lower=True, unit_diagonal=True)
    u, w = jnp.split(sol, 2, axis=-1)

    attn_intra = jnp.where(tri_incl, jnp.einsum('bhncd,bhnsd->bhncs', q, k) * decay, 0.0)
    q_dec = q * jnp.exp(g)[..., None]
    k_dec = k * jnp.exp(g[..., -1:] - g)[..., None]
    g_last = jnp.exp(g[..., -1])

    def step(state, inp):
        u_c, w_c, q_c, k_c, a_c, gl = inp
        v_new = u_c - jnp.einsum('bhcd,bhde->bhce', w_c, state)
        o = jnp.einsum('bhcd,bhde->bhce', q_c, state) + jnp.einsum('bhcs,bhse->bhce', a_c, v_new)
        state = state * gl[..., None, None] + jnp.einsum('bhcd,bhce->bhde', k_c, v_new)
        return state, o

    xs = (jnp.moveaxis(u, 2, 0), jnp.moveaxis(w, 2, 0), jnp.moveaxis(q_dec, 2, 0),
          jnp.moveaxis(k_dec, 2, 0), jnp.moveaxis(attn_intra, 2, 0), jnp.moveaxis(g_last, 2, 0))
    state0 = jnp.zeros((B_, H, D, D), f32)
    _, o = lax.scan(step, state0, xs)
    o = o.transpose(1, 0, 3, 2, 4).reshape(B_, S, H, D)
    o = rms_norm(o, out_gain) * jax.nn.silu(z.reshape(B_, S, H, D).astype(f32))
    return o.reshape(B_, S, H * D).astype(z.dtype)


def hybrid_layer(x, mix_pre_norm, w_in, fox_forget_bias, fox_out_norm, sc_conv_w,
                 gdn_conv_w, gdn_a_log, gdn_dt_bias, gdn_out_norm, w_out, mix_post_norm,
                 ffn_pre_norm, w_gate, w_up, w_down, ffn_post_norm):
    B_, S, _ = x.shape
    u = rms_norm(x, mix_pre_norm)
    proj = u @ w_in
    (fq, fk, fv, ff, sb, sc, sh, gq, gk, gv, ga, gb, gz) = jnp.split(proj, _proj_split_points(), axis=-1)

    fox_out = forgetting_attention(fq.reshape(B_, S, FOX_HEADS, HEAD_DIM),
                                   fk.reshape(B_, S, FOX_HEADS, HEAD_DIM),
                                   fv.reshape(B_, S, FOX_HEADS, HEAD_DIM),
                                   ff, fox_forget_bias, fox_out_norm)
    sc_out = short_conv_mixer(sb, sc, sh, sc_conv_w)
    gdn_out = gated_deltanet(gq, gk, gv, ga, gb, gz, gdn_conv_w, gdn_a_log, gdn_dt_bias, gdn_out_norm)

    mix = jnp.concatenate([fox_out.astype(x.dtype), sc_out.astype(x.dtype), gdn_out.astype(x.dtype)], axis=-1)
    h = x + rms_norm(mix @ w_out, mix_post_norm)

    v = rms_norm(h, ffn_pre_norm)
    y = (jax.nn.silu(v @ w_gate) * (v @ w_up)) @ w_down
    return h + rms_norm(y, ffn_post_norm)


def setup_inputs(seed: int = 0) -> dict:
    key = jax.random.key(seed)
    ks = jax.random.split(key, 18)
    f32 = jnp.float32

    def normal(k, shape, scale):
        return jax.random.normal(k, shape, f32) * scale

    def gain(k, shape):
        return 1.0 + 0.05 * jax.random.normal(k, shape, f32)

    x = normal(ks[0], (BATCH, SEQ, D_MODEL), 1.0)
    mix_pre_norm = gain(ks[1], (DEPTH, D_MODEL))
    w_in = normal(ks[2], (DEPTH, D_MODEL, PROJ_DIM), D_MODEL ** -0.5)
    fox_forget_bias = 3.0 + 0.5 * jax.random.normal(ks[3], (DEPTH, FOX_HEADS), f32)
    fox_out_norm = gain(ks[4], (DEPTH, HEAD_DIM))
    sc_conv_w = normal(ks[5], (DEPTH, SC_KERNEL, SC_WIDTH), SC_KERNEL ** -0.5)
    gdn_conv_w = normal(ks[6], (DEPTH, GDN_CONV, 3 * GDN_WIDTH), GDN_CONV ** -0.5)
    gdn_a_log = jnp.log(jax.random.uniform(ks[7], (DEPTH, GDN_HEADS), f32, 1.0, 16.0))
    dt = jnp.exp(jax.random.uniform(ks[8], (DEPTH, GDN_HEADS), f32, math.log(1e-3), math.log(1e-1)))
    gdn_dt_bias = dt + jnp.log(-jnp.expm1(-dt))
    gdn_out_norm = gain(ks[9], (DEPTH, HEAD_DIM))
    w_out = normal(ks[10], (DEPTH, MIX_WIDTH, D_MODEL), MIX_WIDTH ** -0.5)
    mix_post_norm = gain(ks[11], (DEPTH, D_MODEL))
    ffn_pre_norm = gain(ks[12], (DEPTH, D_MODEL))
    w_gate = normal(ks[13], (DEPTH, D_MODEL, FFN_HIDDEN), D_MODEL ** -0.5)
    w_up = normal(ks[14], (DEPTH, D_MODEL, FFN_HIDDEN), D_MODEL ** -0.5)
    w_down = normal(ks[15], (DEPTH, FFN_HIDDEN, D_MODEL), FFN_HIDDEN ** -0.5)
    ffn_post_norm = gain(ks[16], (DEPTH, D_MODEL))
    return {"x": x, "mix_pre_norm": mix_pre_norm, "w_in": w_in, "fox_forget_bias": fox_forget_bias,
            "fox_out_norm": fox_out_norm, "sc_conv_w": sc_conv_w, "gdn_conv_w": gdn_conv_w,
            "gdn_a_log": gdn_a_log, "gdn_dt_bias": gdn_dt_bias, "gdn_out_norm": gdn_out_norm,
            "w_out": w_out, "mix_post_norm": mix_post_norm, "ffn_pre_norm": ffn_pre_norm,
            "w_gate": w_gate, "w_up": w_up, "w_down": w_down, "ffn_post_norm": ffn_post_norm}


def reference(x, mix_pre_norm, w_in, fox_forget_bias, fox_out_norm, sc_conv_w, gdn_conv_w,
              gdn_a_log, gdn_dt_bias, gdn_out_norm, w_out, mix_post_norm, ffn_pre_norm,
              w_gate, w_up, w_down, ffn_post_norm):
    h = x
    for l in range(DEPTH):
        h = hybrid_layer(h, mix_pre_norm[l], w_in[l], fox_forget_bias[l], fox_out_norm[l],
                         sc_conv_w[l], gdn_conv_w[l], gdn_a_log[l], gdn_dt_bias[l],
                         gdn_out_norm[l], w_out[l], mix_post_norm[l], ffn_pre_norm[l],
                         w_gate[l], w_up[l], w_down[l], ffn_post_norm[l])
    return h
```

```python
import functools
import math

import jax
import jax.numpy as jnp
from jax import lax
from jax.experimental import pallas as pl
from jax.experimental.pallas import tpu as pltpu

F32 = jnp.float32
BF16 = jnp.bfloat16

HEAD_DIM = 128
FOX_HEADS = 12
FOX_WIDTH = FOX_HEADS * HEAD_DIM
SC_WIDTH = 8 * HEAD_DIM
SC_KERNEL = 3
GDN_HEADS = 12
GDN_WIDTH = GDN_HEADS * HEAD_DIM
GDN_CONV = 4
CHUNK = 64
RMS_EPS = 1e-6
GATE_LANES = 128
FFN_PAD = 1024

VMEM_LIMIT_BYTES = 56 * 1024 * 1024

_FQ, _FK, _FV = 0, 12, 24
_SB, _SC, _SH = 36, 44, 52
_GQ, _GK, _GV, _GZ = 60, 72, 84, 96
MAIN_WIDTH = 108 * HEAD_DIM

_NT = (((1,), (1,)), ((), ()))
_TN = (((0,), (0,)), ((), ()))
_HI = lax.Precision.HIGHEST


def _params(sem):
    return pltpu.CompilerParams(dimension_semantics=sem, vmem_limit_bytes=VMEM_LIMIT_BYTES)


def _norm_cast_kernel(x_ref, g_ref, o_ref):
    x = x_ref[...]
    ms = jnp.mean(x * x, axis=-1, keepdims=True)
    o_ref[...] = (x * lax.rsqrt(ms + RMS_EPS) * g_ref[...]).astype(o_ref.dtype)


def _norm_cast(x, gain, tm=256):
    T, D = x.shape
    tm = min(tm, T)
    return pl.pallas_call(
        _norm_cast_kernel,
        out_shape=jax.ShapeDtypeStruct((T, D), BF16),
        grid=(T // tm,),
        in_specs=[pl.BlockSpec((tm, D), lambda i: (i, 0)),
                  pl.BlockSpec((1, D), lambda i: (0, 0))],
        out_specs=pl.BlockSpec((tm, D), lambda i: (i, 0)),
        compiler_params=_params(("parallel",)),
        name="norm_cast",
    )(x, gain.reshape(1, D))


def _norm_residual_kernel(y_ref, x_ref, g_ref, gn_ref, h_ref, *maybe_u_ref):
    y = y_ref[...]
    ms = jnp.mean(y * y, axis=-1, keepdims=True)
    h = x_ref[...] + y * lax.rsqrt(ms + RMS_EPS) * g_ref[...]
    h_ref[...] = h
    if maybe_u_ref:
        ms2 = jnp.mean(h * h, axis=-1, keepdims=True)
        maybe_u_ref[0][...] = (h * lax.rsqrt(ms2 + RMS_EPS) * gn_ref[...]).astype(BF16)


def _norm_residual(y, x, gain, next_gain, tm=256):
    T, D = x.shape
    tm = min(tm, T)
    emit_next = next_gain is not None
    gn = (next_gain if emit_next else gain).reshape(1, D)
    row = pl.BlockSpec((tm, D), lambda i: (i, 0))
    vec = pl.BlockSpec((1, D), lambda i: (0, 0))
    out_shape = [jax.ShapeDtypeStruct((T, D), F32)]
    out_specs = [row]
    if emit_next:
        out_shape.append(jax.ShapeDtypeStruct((T, D), BF16))
        out_specs.append(row)
    res = pl.pallas_call(
        _norm_residual_kernel,
        out_shape=out_shape,
        grid=(T // tm,),
        in_specs=[row, row, vec, vec],
        out_specs=out_specs,
        compiler_params=_params(("parallel",)),
        name="norm_residual",
    )(y, x, gain.reshape(1, D), gn)
    return (res[0], res[1]) if emit_next else (res[0], None)


def _matmul_kernel(x_ref, w_ref, o_ref, acc_ref, *, nk):
    part = jnp.dot(x_ref[...], w_ref[...], preferred_element_type=F32)
    if nk == 1:
        o_ref[...] = part.astype(o_ref.dtype)
        return
    k = pl.program_id(2)

    @pl.when(k == 0)
    def _():
        acc_ref[...] = part

    @pl.when(k > 0)
    def _():
        acc_ref[...] += part

    @pl.when(k == nk - 1)
    def _():
        o_ref[...] = acc_ref[...].astype(o_ref.dtype)


def _matmul(x, w, out_dtype, tm, tn, tk=None, name="matmul"):
    M, K = x.shape
    _, N = w.shape
    tm, tn = min(tm, M), min(tn, N)
    tk = K if tk is None else tk
    nk = K // tk
    acc_shape = (tm, tn) if nk > 1 else (8, 128)
    return pl.pallas_call(
        functools.partial(_matmul_kernel, nk=nk),
        out_shape=jax.ShapeDtypeStruct((M, N), out_dtype),
        grid=(M // tm, N // tn, nk),
        in_specs=[pl.BlockSpec((tm, tk), lambda i, j, k: (i, k)),
                  pl.BlockSpec((tk, tn), lambda i, j, k: (k, j))],
        out_specs=pl.BlockSpec((tm, tn), lambda i, j, k: (i, j)),
        scratch_shapes=[pltpu.VMEM(acc_shape, F32)],
        compiler_params=_params(("parallel", "parallel", "arbitrary")),
        name=name,
    )(x, w)


def _out_proj_kernel(a1_ref, a2_ref, a3_ref, w1_ref, w2_ref, w3_ref, o_ref):
    acc = jnp.dot(a1_ref[...], w1_ref[...], preferred_element_type=F32)
    acc += jnp.dot(a2_ref[...], w2_ref[...], preferred_element_type=F32)
    acc += jnp.dot(a3_ref[...], w3_ref[...], preferred_element_type=F32)
    o_ref[...] = acc


def _out_proj(a1, a2, a3, w1, w2, w3, tm=1024, tn=1024):
    T = a1.shape[0]
    N = w1.shape[1]
    tm, tn = min(tm, T), min(tn, N)

    def lhs(a):
        return pl.BlockSpec((tm, a.shape[1]), lambda i, j: (i, 0))

    def rhs(w):
        return pl.BlockSpec((w.shape[0], tn), lambda i, j: (0, j))

    return pl.pallas_call(
        _out_proj_kernel,
        out_shape=jax.ShapeDtypeStruct((T, N), F32),
        grid=(T // tm, N // tn),
        in_specs=[lhs(a1), lhs(a2), lhs(a3), rhs(w1), rhs(w2), rhs(w3)],
        out_specs=pl.BlockSpec((tm, tn), lambda i, j: (i, j)),
        compiler_params=_params(("parallel", "parallel")),
        name="out_proj",
    )(a1, a2, a3, w1, w2, w3)


def _ffn_up_kernel(x_ref, wg_ref, wu_ref, o_ref):
    x = x_ref[...]
    g = jnp.dot(x, wg_ref[...], preferred_element_type=F32)
    u = jnp.dot(x, wu_ref[...], preferred_element_type=F32)
    o_ref[...] = (g * jax.nn.sigmoid(g) * u).astype(o_ref.dtype)


def _ffn_up(x, wg, wu, tm=1024, tn=512):
    T, K = x.shape
    N = wg.shape[1]
    tm, tn = min(tm, T), min(tn, N)
    return pl.pallas_call(
        _ffn_up_kernel,
        out_shape=jax.ShapeDtypeStruct((T, N), BF16),
        grid=(T // tm, N // tn),
        in_specs=[pl.BlockSpec((tm, K), lambda i, j: (i, 0)),
                  pl.BlockSpec((K, tn), lambda i, j: (0, j)),
                  pl.BlockSpec((K, tn), lambda i, j: (0, j))],
        out_specs=pl.BlockSpec((tm, tn), lambda i, j: (i, j)),
        compiler_params=_params(("parallel", "parallel")),
        name="ffn_up",
    )(x, wg, wu)


def _gates_kernel(w_ref, u_ref, bias_ref, alog_ref, gt_ref, carry_ref, *, steps_per_seq, ts):
    t = pl.program_id(0)

    @pl.when(t % steps_per_seq == 0)
    def _():
        carry_ref[...] = jnp.zeros_like(carry_ref)

    z = lax.dot_general(w_ref[...], u_ref[...], _NT, preferred_element_type=F32) + bias_ref[...]
    row = lax.broadcasted_iota(jnp.int32, z.shape, 0)
    tail = jnp.log1p(jnp.exp(-jnp.abs(z)))
    log_sig = jnp.minimum(z, 0.0) - tail
    softplus = jnp.maximum(z, 0.0) + tail
    sig = 1.0 / (1.0 + jnp.exp(-z))
    decay = -jnp.exp(alog_ref[...]) * softplus
    val = jnp.where(row < FOX_HEADS, log_sig, jnp.where(row < FOX_HEADS + GDN_HEADS, decay, sig))

    src = lax.broadcasted_iota(jnp.int32, (ts, ts), 0)
    dst = lax.broadcasted_iota(jnp.int32, (ts, ts), 1)
    upper = src <= dst
    same_chunk = (src // CHUNK) == (dst // CHUNK)
    cum_all = jnp.dot(val, upper.astype(F32), precision=_HI, preferred_element_type=F32)
    cum_chunk = jnp.dot(val, (upper & same_chunk).astype(F32), precision=_HI,
                        preferred_element_type=F32)
    cum_all = cum_all + carry_ref[...]
    carry_ref[...] = cum_all[:, ts - 1:ts]
    gt_ref[...] = jnp.where(row < FOX_HEADS, cum_all,
                            jnp.where(row < FOX_HEADS + GDN_HEADS, cum_chunk, val))


def _gates(u, w_small_t, bias_col, alog_col, seq_len, ts=512):
    T, D = u.shape
    ts = min(ts, seq_len)
    return pl.pallas_call(
        functools.partial(_gates_kernel, steps_per_seq=seq_len // ts, ts=ts),
        out_shape=jax.ShapeDtypeStruct((GATE_LANES, T), F32),
        grid=(T // ts,),
        in_specs=[pl.BlockSpec((GATE_LANES, D), lambda t: (0, 0)),
                  pl.BlockSpec((ts, D), lambda t: (t, 0)),
                  pl.BlockSpec((GATE_LANES, 1), lambda t: (0, 0)),
                  pl.BlockSpec((GATE_LANES, 1), lambda t: (0, 0))],
        out_specs=pl.BlockSpec((GATE_LANES, ts), lambda t: (0, t)),
        scratch_shapes=[pltpu.VMEM((GATE_LANES, 1), F32)],
        compiler_params=_params(("arbitrary",)),
        name="gates",
    )(w_small_t, u, bias_col, alog_col)


def _fox_kernel(q_ref, k_ref, v_ref, f_ref, g_ref, o_ref, m_ref, l_ref, acc_ref, *, tq, scale):
    qi = pl.program_id(1)
    q = q_ref[...]
    m_ref[...] = jnp.full_like(m_ref, -jnp.inf)
    l_ref[...] = jnp.zeros_like(l_ref)
    acc_ref[...] = jnp.zeros_like(acc_ref)

    def step(kj, masked):
        k0 = pl.multiple_of(kj * tq, tq)
        k = k_ref[pl.ds(k0, tq), :]
        v = v_ref[pl.ds(k0, tq), :]
        fk = f_ref[0, pl.ds(kj, 1), :]
        s = lax.dot_general(q, k, _NT, preferred_element_type=F32) * scale - fk
        if masked:
            r = lax.broadcasted_iota(jnp.int32, s.shape, 0)
            c = lax.broadcasted_iota(jnp.int32, s.shape, 1)
            s = jnp.where(c <= r, s, -jnp.inf)
        m_prev = m_ref[...]
        m_new = jnp.maximum(m_prev, jnp.max(s, axis=-1, keepdims=True))
        alpha = jnp.exp(m_prev - m_new)
        p = jnp.exp(s - m_new)
        l_ref[...] = alpha * l_ref[...] + jnp.sum(p, axis=-1, keepdims=True)
        acc_ref[...] = alpha * acc_ref[...] + jnp.dot(p.astype(BF16), v, preferred_element_type=F32)
        m_ref[...] = m_new

    def body(kj, carry):
        step(kj, False)
        return carry

    lax.fori_loop(0, qi, body, 0)
    step(qi, True)

    out = acc_ref[...] / l_ref[...]
    ms = jnp.mean(out * out, axis=-1, keepdims=True)
    o_ref[...] = (out * lax.rsqrt(ms + RMS_EPS) * g_ref[...]).astype(o_ref.dtype)


def _fox_attention(proj, gates_t, out_gain, batch, seq_len, tq=512):
    T = proj.shape[0]
    tq = min(tq, seq_len)
    nq = seq_len // tq
    H = FOX_HEADS
    f_blocks = gates_t.reshape(GATE_LANES, T // tq, tq)
    return pl.pallas_call(
        functools.partial(_fox_kernel, tq=tq, scale=HEAD_DIM ** -0.5),
        out_shape=jax.ShapeDtypeStruct((T, FOX_WIDTH), BF16),
        grid=(batch * H, nq),
        in_specs=[pl.BlockSpec((tq, HEAD_DIM), lambda bh, qi: ((bh // H) * nq + qi, _FQ + bh % H)),
                  pl.BlockSpec((seq_len, HEAD_DIM), lambda bh, qi: (bh // H, _FK + bh % H)),
                  pl.BlockSpec((seq_len, HEAD_DIM), lambda bh, qi: (bh // H, _FV + bh % H)),
                  pl.BlockSpec((1, nq, tq), lambda bh, qi: (bh % H, bh // H, 0)),
                  pl.BlockSpec((1, HEAD_DIM), lambda bh, qi: (0, 0))],
        out_specs=pl.BlockSpec((tq, HEAD_DIM), lambda bh, qi: ((bh // H) * nq + qi, bh % H)),
        scratch_shapes=[pltpu.VMEM((tq, 1), F32), pltpu.VMEM((tq, 1), F32),
                        pltpu.VMEM((tq, HEAD_DIM), F32)],
        compiler_params=_params(("parallel", "arbitrary")),
        name="fox_attention",
    )(proj, proj, proj, f_blocks, out_gain.reshape(1, HEAD_DIM))


_HALO = 8


def _short_conv_kernel(b_ref, c_ref, h_ref, w_ref, o_ref, buf_ref, *, steps_per_seq, ts):
    t = pl.program_id(1)

    @pl.when(t % steps_per_seq == 0)
    def _():
        buf_ref[0:_HALO, :] = jnp.zeros((_HALO, buf_ref.shape[1]), F32)

    buf_ref[_HALO:_HALO + ts, :] = c_ref[...].astype(F32) * h_ref[...].astype(F32)
    w = w_ref[...]
    y = w[0:1, :] * buf_ref[pl.ds(_HALO - 2, ts), :]
    y += w[1:2, :] * buf_ref[pl.ds(_HALO - 1, ts), :]
    y += w[2:3, :] * buf_ref[pl.ds(_HALO, ts), :]
    o_ref[...] = (b_ref[...].astype(F32) * y).astype(o_ref.dtype)
    buf_ref[0:_HALO, :] = buf_ref[ts:ts + _HALO, :]


def _short_conv(proj, conv_w, seq_len, ts=512, tc=512):
    T = proj.shape[0]
    ts = min(ts, seq_len)
    per = tc // HEAD_DIM

    def col(base):
        return pl.BlockSpec((ts, tc), lambda c, t: (t, base // per + c))

    return pl.pallas_call(
        functools.partial(_short_conv_kernel, steps_per_seq=seq_len // ts, ts=ts),
        out_shape=jax.ShapeDtypeStruct((T, SC_WIDTH), BF16),
        grid=(SC_WIDTH // tc, T // ts),
        in_specs=[col(_SB), col(_SC), col(_SH),
                  pl.BlockSpec((SC_KERNEL, tc), lambda c, t: (0, c))],
        out_specs=pl.BlockSpec((ts, tc), lambda c, t: (t, c)),
        scratch_shapes=[pltpu.VMEM((_HALO + ts, tc), F32)],
        compiler_params=_params(("parallel", "arbitrary")),
        name="short_conv",
    )(proj, proj, proj, conv_w)


def _bmm(a, b, dims, precision=_HI):
    return lax.dot_general(a, b, dims, precision=precision, preferred_element_type=F32)


_B_NN = (((2,), (1,)), ((0,), (0,)))
_B_NT = (((2,), (2,)), ((0,), (0,)))


def _gdn_kernel(q_ref, k_ref, v_ref, z_ref, wq_ref, wk_ref, wv_ref, gc_ref, gr_ref, gain_ref,
                o_ref, qbuf, kbuf, vbuf, state_ref, obuf, *, rows):
    h = pl.program_id(1)
    t = pl.program_id(2)
    n = rows // CHUNK
    D = HEAD_DIM

    @pl.when(t == 0)
    def _():
        zeros = jnp.zeros((_HALO, D), F32)
        qbuf[0:_HALO, :] = zeros
        kbuf[0:_HALO, :] = zeros
        vbuf[0:_HALO, :] = zeros
        state_ref[...] = jnp.zeros_like(state_ref)

    def conv_silu(x_ref, w_ref, buf):
        buf[_HALO:_HALO + rows, :] = x_ref[...].astype(F32)
        w = w_ref[...]
        y = w[0:1, :] * buf[pl.ds(_HALO - 3, rows), :]
        y += w[1:2, :] * buf[pl.ds(_HALO - 2, rows), :]
        y += w[2:3, :] * buf[pl.ds(_HALO - 1, rows), :]
        y += w[3:4, :] * buf[pl.ds(_HALO, rows), :]
        buf[0:_HALO, :] = buf[rows:rows + _HALO, :]
        return y * jax.nn.sigmoid(y)

    q = conv_silu(q_ref, wq_ref, qbuf)
    k = conv_silu(k_ref, wk_ref, kbuf)
    v = conv_silu(v_ref, wv_ref, vbuf)
    q = q * (lax.rsqrt(jnp.sum(q * q, axis=-1, keepdims=True) + RMS_EPS) * (D ** -0.5))
    k = k * lax.rsqrt(jnp.sum(k * k, axis=-1, keepdims=True) + RMS_EPS)

    gates = gc_ref[...]
    lane = lax.broadcasted_iota(jnp.int32, gates.shape, 1)
    g_cum = jnp.sum(jnp.where(lane == FOX_HEADS + h, gates, 0.0), axis=-1, keepdims=True)
    beta = jnp.sum(jnp.where(lane == FOX_HEADS + GDN_HEADS + h, gates, 0.0), axis=-1, keepdims=True)

    q3 = q.reshape(n, CHUNK, D)
    k3 = k.reshape(n, CHUNK, D)
    v3 = v.reshape(n, CHUNK, D)
    g3 = g_cum.reshape(n, CHUNK, 1)
    b3 = beta.reshape(n, CHUNK, 1)
    g_row = gr_ref[:, pl.ds(FOX_HEADS + h, 1), :]
    g_last = g3[:, CHUNK - 1:CHUNK, :]

    ri = lax.broadcasted_iota(jnp.int32, (CHUNK, CHUNK), 0)
    ci = lax.broadcasted_iota(jnp.int32, (CHUNK, CHUNK), 1)
    tri_incl = (ci <= ri)[None]
    tri_strict = (ci < ri)[None]
    same16 = ((ri // 16) == (ci // 16))[None]
    same32 = ((ri // 32) == (ci // 32))[None]
    eye = (ri == ci).astype(F32)[None]

    decay = jnp.exp(jnp.where(tri_incl, g3 - g_row, -jnp.inf))
    kk = _bmm(k3, k3, _B_NT)
    L = jnp.where(tri_strict, b3 * kk * decay, 0.0)

    P = jnp.where(same16, L, 0.0)
    X = eye - P
    P2 = _bmm(P, P, _B_NN)
    X = _bmm(X, eye + P2, _B_NN)
    P4 = _bmm(P2, P2, _B_NN)
    X = _bmm(X, eye + P4, _B_NN)
    P8 = _bmm(P4, P4, _B_NN)
    X = _bmm(X, eye + P8, _B_NN)
    O32 = jnp.where(same32 & jnp.logical_not(same16), L, 0.0)
    X = X - _bmm(_bmm(X, O32, _B_NN), X, _B_NN)
    O64 = jnp.where(same32, 0.0, L)
    X = X - _bmm(_bmm(X, O64, _B_NN), X, _B_NN)

    e3 = jnp.exp(g3)
    rhs = jnp.concatenate([v3 * b3, k3 * (b3 * e3)], axis=-1)
    sol = _bmm(X, rhs, _B_NN)
    u3 = sol[:, :, :D]
    w3 = sol[:, :, D:]
    attn = jnp.where(tri_incl, _bmm(q3, k3, _B_NT) * decay, 0.0)
    q_dec = q3 * e3
    k_dec = k3 * jnp.exp(g_last - g3)
    gl = jnp.exp(g_last)

    S = state_ref[...]
    for c in range(n):
        v_new = u3[c] - jnp.dot(w3[c], S, precision=_HI, preferred_element_type=F32)
        o_c = jnp.dot(q_dec[c], S, precision=_HI, preferred_element_type=F32)
        o_c += jnp.dot(attn[c], v_new, precision=_HI, preferred_element_type=F32)
        S = S * gl[c] + lax.dot_general(k_dec[c], v_new, _TN, precision=_HI,
                                        preferred_element_type=F32)
        obuf[c * CHUNK:(c + 1) * CHUNK, :] = o_c
    state_ref[...] = S

    o = obuf[...]
    ms = jnp.mean(o * o, axis=-1, keepdims=True)
    z = z_ref[...].astype(F32)
    o = o * lax.rsqrt(ms + RMS_EPS) * gain_ref[...] * (z * jax.nn.sigmoid(z))
    o_ref[...] = o.astype(o_ref.dtype)


def _gdn(proj, conv_w, gates_c, gates_r, out_gain, batch, seq_len, rows=512):
    T = proj.shape[0]
    rows = min(rows, seq_len)
    nt = seq_len // rows
    n = rows // CHUNK
    H = GDN_HEADS

    def col(base):
        return pl.BlockSpec((rows, HEAD_DIM), lambda b, h, t: (b * nt + t, base + h))

    def wcol(base):
        return pl.BlockSpec((GDN_CONV, HEAD_DIM), lambda b, h, t: (0, base + h))

    return pl.pallas_call(
        functools.partial(_gdn_kernel, rows=rows),
        out_shape=jax.ShapeDtypeStruct((T, GDN_WIDTH), BF16),
        grid=(batch, H, nt),
        in_specs=[col(_GQ), col(_GK), col(_GV), col(_GZ),
                  wcol(0), wcol(H), wcol(2 * H),
                  pl.BlockSpec((rows, GATE_LANES), lambda b, h, t: (b * nt + t, 0)),
                  pl.BlockSpec((n, GATE_LANES, CHUNK), lambda b, h, t: (b * nt + t, 0, 0)),
                  pl.BlockSpec((1, HEAD_DIM), lambda b, h, t: (0, 0))],
        out_specs=pl.BlockSpec((rows, HEAD_DIM), lambda b, h, t: (b * nt + t, h)),
        scratch_shapes=[pltpu.VMEM((_HALO + rows, HEAD_DIM), F32),
                        pltpu.VMEM((_HALO + rows, HEAD_DIM), F32),
                        pltpu.VMEM((_HALO + rows, HEAD_DIM), F32),
                        pltpu.VMEM((HEAD_DIM, HEAD_DIM), F32),
                        pltpu.VMEM((rows, HEAD_DIM), F32)],
        compiler_params=_params(("parallel", "parallel", "arbitrary")),
        name="gdn",
    )(proj, proj, proj, proj, conv_w, conv_w, conv_w, gates_c, gates_r,
      out_gain.reshape(1, HEAD_DIM))


def _split_w_in(w_in):
    sizes = (FOX_WIDTH, FOX_WIDTH, FOX_WIDTH, FOX_HEADS, SC_WIDTH, SC_WIDTH, SC_WIDTH,
             GDN_WIDTH, GDN_WIDTH, GDN_WIDTH, GDN_HEADS, GDN_HEADS, GDN_WIDTH)
    offs = [0]
    for s in sizes:
        offs.append(offs[-1] + s)
    part = [w_in[:, offs[i]:offs[i + 1]] for i in range(len(sizes))]
    fq, fk, fv, ff, sb, sc, sh, gq, gk, gv, ga, gb, gz = part
    w_main = jnp.concatenate([fq, fk, fv, sb, sc, sh, gq, gk, gv, gz], axis=1).astype(BF16)
    small = jnp.concatenate([ff, ga, gb], axis=1)
    small = jnp.pad(small, ((0, 0), (0, GATE_LANES - small.shape[1])))
    return w_main, small.T.astype(BF16)


def _gate_columns(fox_forget_bias, gdn_a_log, gdn_dt_bias):
    pad = GATE_LANES - FOX_HEADS - GDN_HEADS
    bias = jnp.concatenate([fox_forget_bias, gdn_dt_bias, jnp.zeros((pad,), F32)])
    alog = jnp.concatenate([jnp.zeros((FOX_HEADS,), F32), gdn_a_log, jnp.zeros((pad,), F32)])
    return bias.reshape(GATE_LANES, 1), alog.reshape(GATE_LANES, 1)


def _layer(x, u, batch, seq_len, w_in, fox_forget_bias, fox_out_norm, sc_conv_w, gdn_conv_w,
           gdn_a_log, gdn_dt_bias, gdn_out_norm, w_out, mix_post_norm, ffn_pre_norm,
           w_gate, w_up, w_down, ffn_post_norm, next_pre_norm):
    T, D = x.shape
    w_main, w_small_t = _split_w_in(w_in)
    bias_col, alog_col = _gate_columns(fox_forget_bias, gdn_a_log, gdn_dt_bias)

    proj = _matmul(u, w_main, BF16, tm=1024, tn=768, name="in_proj")
    gates_t = _gates(u, w_small_t, bias_col, alog_col, seq_len)
    gates_c = gates_t.T
    gates_r = gates_t.reshape(GATE_LANES, T // CHUNK, CHUNK).transpose(1, 0, 2)

    fox_out = _fox_attention(proj, gates_t, fox_out_norm, batch, seq_len)
    sc_out = _short_conv(proj, sc_conv_w, seq_len)
    gdn_out = _gdn(proj, gdn_conv_w, gates_c, gates_r, gdn_out_norm, batch, seq_len)

    w_out_b = w_out.astype(BF16)
    y = _out_proj(fox_out, sc_out, gdn_out,
                  w_out_b[:FOX_WIDTH], w_out_b[FOX_WIDTH:FOX_WIDTH + SC_WIDTH],
                  w_out_b[FOX_WIDTH + SC_WIDTH:])
    h, v = _norm_residual(y, x, mix_post_norm, ffn_pre_norm)

    hidden = w_gate.shape[1]
    pad = (-hidden) % FFN_PAD
    wg = jnp.pad(w_gate.astype(BF16), ((0, 0), (0, pad)))
    wu = jnp.pad(w_up.astype(BF16), ((0, 0), (0, pad)))
    wd = jnp.pad(w_down.astype(BF16), ((0, pad), (0, 0)))
    act = _ffn_up(v, wg, wu)
    kd = hidden + pad
    y2 = _matmul(act, wd, F32, tm=1024, tn=1024, tk=kd // 4, name="ffn_down")
    return _norm_residual(y2, h, ffn_post_norm, next_pre_norm)


def kernel(x, mix_pre_norm, w_in, fox_forget_bias, fox_out_norm, sc_conv_w, gdn_conv_w, gdn_a_log,
           gdn_dt_bias, gdn_out_norm, w_out, mix_post_norm, ffn_pre_norm, w_gate, w_up, w_down,
           ffn_post_norm):
    B, S, D = x.shape
    depth = w_in.shape[0]
    h = x.reshape(B * S, D)
    u = _norm_cast(h, mix_pre_norm[0])
    for l in range(depth):
        nxt = mix_pre_norm[l + 1] if l + 1 < depth else None
        h, u = _layer(h, u, B, S, w_in[l], fox_forget_bias[l], fox_out_norm[l], sc_conv_w[l],
                      gdn_conv_w[l], gdn_a_log[l], gdn_dt_bias[l], gdn_out_norm[l], w_out[l],
                      mix_post_norm[l], ffn_pre_norm[l], w_gate[l], w_up[l], w_down[l],
                      ffn_post_norm[l], nxt)
    return h.reshape(B, S, D)
```

```python
import functools
import math

import jax
import jax.numpy as jnp
from jax import lax
from jax.experimental import pallas as pl
from jax.experimental.pallas import tpu as pltpu

F32 = jnp.float32
BF16 = jnp.bfloat16

HEAD_DIM = 128
FOX_HEADS = 12
FOX_WIDTH = FOX_HEADS * HEAD_DIM
SC_WIDTH = 8 * HEAD_DIM
SC_KERNEL = 3
GDN_HEADS = 12
GDN_WIDTH = GDN_HEADS * HEAD_DIM
GDN_CONV = 4
CHUNK = 64
RMS_EPS = 1e-6
GATE_LANES = 128
FFN_PAD = 1024

VMEM_LIMIT_BYTES = 56 * 1024 * 1024

_FQ, _FK, _FV = 0, 12, 24
_SB, _SC, _SH = 36, 44, 52
_GQ, _GK, _GV, _GZ = 60, 72, 84, 96
MAIN_WIDTH = 108 * HEAD_DIM

_NT = (((1,), (1,)), ((), ()))
_TN = (((0,), (0,)), ((), ()))
_HI = lax.Precision.HIGHEST


def _params(sem):
    return pltpu.CompilerParams(dimension_semantics=sem, vmem_limit_bytes=VMEM_LIMIT_BYTES)


def _norm_cast_kernel(x_ref, g_ref, o_ref):
    x = x_ref[...]
    ms = jnp.mean(x * x, axis=-1, keepdims=True)
    o_ref[...] = (x * lax.rsqrt(ms + RMS_EPS) * g_ref[...]).astype(o_ref.dtype)


def _norm_cast(x, gain, tm=256):
    T, D = x.shape
    tm = min(tm, T)
    return pl.pallas_call(
        _norm_cast_kernel,
        out_shape=jax.ShapeDtypeStruct((T, D), BF16),
        grid=(T // tm,),
        in_specs=[pl.BlockSpec((tm, D), lambda i: (i, 0)),
                  pl.BlockSpec((1, D), lambda i: (0, 0))],
        out_specs=pl.BlockSpec((tm, D), lambda i: (i, 0)),
        compiler_params=_params(("parallel",)),
        name="norm_cast",
    )(x, gain.reshape(1, D))


def _norm_residual_kernel(y_ref, x_ref, g_ref, gn_ref, h_ref, *maybe_u_ref):
    y = y_ref[...]
    ms = jnp.mean(y * y, axis=-1, keepdims=True)
    h = x_ref[...] + y * lax.rsqrt(ms + RMS_EPS) * g_ref[...]
    h_ref[...] = h
    if maybe_u_ref:
        ms2 = jnp.mean(h * h, axis=-1, keepdims=True)
        maybe_u_ref[0][...] = (h * lax.rsqrt(ms2 + RMS_EPS) * gn_ref[...]).astype(BF16)


def _norm_residual(y, x, gain, next_gain, tm=256):
    T, D = x.shape
    tm = min(tm, T)
    emit_next = next_gain is not None
    gn = (next_gain if emit_next else gain).reshape(1, D)
    row = pl.BlockSpec((tm, D), lambda i: (i, 0))
    vec = pl.BlockSpec((1, D), lambda i: (0, 0))
    out_shape = [jax.ShapeDtypeStruct((T, D), F32)]
    out_specs = [row]
    if emit_next:
        out_shape.append(jax.ShapeDtypeStruct((T, D), BF16))
        out_specs.append(row)
    res = pl.pallas_call(
        _norm_residual_kernel,
        out_shape=out_shape,
        grid=(T // tm,),
        in_specs=[row, row, vec, vec],
        out_specs=out_specs,
        compiler_params=_params(("parallel",)),
        name="norm_residual",
    )(y, x, gain.reshape(1, D), gn)
    return (res[0], res[1]) if emit_next else (res[0], None)


def _cast_weight_kernel(w_ref, o_ref, *, rows, cols, tr, tc, masked):
    w = w_ref[0]
    if masked:
        ri = pl.program_id(0) * tr + lax.broadcasted_iota(jnp.int32, w.shape, 0)
        ci = pl.program_id(1) * tc + lax.broadcasted_iota(jnp.int32, w.shape, 1)
        w = jnp.where((ri < rows) & (ci < cols), w, 0.0)
    o_ref[...] = w.astype(o_ref.dtype)


def _cast_weight(w_stack, layer, rows_pad=None, cols_pad=None, tr=512, tc=1024):
    _, rows, cols = w_stack.shape
    rows_pad = rows if rows_pad is None else rows_pad
    cols_pad = cols if cols_pad is None else cols_pad
    masked = (rows_pad != rows) or (cols_pad != cols)
    return pl.pallas_call(
        functools.partial(_cast_weight_kernel, rows=rows, cols=cols, tr=tr, tc=tc, masked=masked),
        out_shape=jax.ShapeDtypeStruct((rows_pad, cols_pad), BF16),
        grid=(rows_pad // tr, cols_pad // tc),
        in_specs=[pl.BlockSpec((1, tr, tc), lambda i, j: (layer, i, j))],
        out_specs=pl.BlockSpec((tr, tc), lambda i, j: (i, j)),
        compiler_params=_params(("parallel", "parallel")),
        name="cast_weight",
    )(w_stack)


def _matmul_kernel(x_ref, w_ref, o_ref, acc_ref, *, nk):
    part = jnp.dot(x_ref[...], w_ref[...], preferred_element_type=F32)
    if nk == 1:
        o_ref[...] = part.astype(o_ref.dtype)
        return
    k = pl.program_id(2)

    @pl.when(k == 0)
    def _():
        acc_ref[...] = part

    @pl.when(k > 0)
    def _():
        acc_ref[...] += part

    @pl.when(k == nk - 1)
    def _():
        o_ref[...] = acc_ref[...].astype(o_ref.dtype)


def _matmul(x, w, out_dtype, tm, tn, tk=None, name="matmul"):
    M, K = x.shape
    _, N = w.shape
    tm, tn = min(tm, M), min(tn, N)
    tk = K if tk is None else tk
    nk = K // tk
    acc_shape = (tm, tn) if nk > 1 else (8, 128)
    return pl.pallas_call(
        functools.partial(_matmul_kernel, nk=nk),
        out_shape=jax.ShapeDtypeStruct((M, N), out_dtype),
        grid=(M // tm, N // tn, nk),
        in_specs=[pl.BlockSpec((tm, tk), lambda i, j, k: (i, k)),
                  pl.BlockSpec((tk, tn), lambda i, j, k: (k, j))],
        out_specs=pl.BlockSpec((tm, tn), lambda i, j, k: (i, j)),
        scratch_shapes=[pltpu.VMEM(acc_shape, F32)],
        compiler_params=_params(("parallel", "parallel", "arbitrary")),
        name=name,
    )(x, w)


def _out_proj_kernel(a1_ref, a2_ref, a3_ref, w1_ref, w2_ref, w3_ref, o_ref):
    acc = jnp.dot(a1_ref[...], w1_ref[...], preferred_element_type=F32)
    acc += jnp.dot(a2_ref[...], w2_ref[...], preferred_element_type=F32)
    acc += jnp.dot(a3_ref[...], w3_ref[...], preferred_element_type=F32)
    o_ref[...] = acc


def _out_proj(a1, a2, a3, w1, w2, w3, tm=1024, tn=1024):
    T = a1.shape[0]
    N = w1.shape[1]
    tm, tn = min(tm, T), min(tn, N)

    def lhs(a):
        return pl.BlockSpec((tm, a.shape[1]), lambda i, j: (i, 0))

    def rhs(w):
        return pl.BlockSpec((w.shape[0], tn), lambda i, j: (0, j))

    return pl.pallas_call(
        _out_proj_kernel,
        out_shape=jax.ShapeDtypeStruct((T, N), F32),
        grid=(T // tm, N // tn),
        in_specs=[lhs(a1), lhs(a2), lhs(a3), rhs(w1), rhs(w2), rhs(w3)],
        out_specs=pl.BlockSpec((tm, tn), lambda i, j: (i, j)),
        compiler_params=_params(("parallel", "parallel")),
        name="out_proj",
    )(a1, a2, a3, w1, w2, w3)


def _ffn_up_kernel(x_ref, wg_ref, wu_ref, o_ref):
    x = x_ref[...]
    g = jnp.dot(x, wg_ref[...], preferred_element_type=F32)
    u = jnp.dot(x, wu_ref[...], preferred_element_type=F32)
    o_ref[...] = (g * jax.nn.sigmoid(g) * u).astype(o_ref.dtype)


def _ffn_up(x, wg, wu, tm=1024, tn=512):
    T, K = x.shape
    N = wg.shape[1]
    tm, tn = min(tm, T), min(tn, N)
    return pl.pallas_call(
        _ffn_up_kernel,
        out_shape=jax.ShapeDtypeStruct((T, N), BF16),
        grid=(T // tm, N // tn),
        in_specs=[pl.BlockSpec((tm, K), lambda i, j: (i, 0)),
                  pl.BlockSpec((K, tn), lambda i, j: (0, j)),
                  pl.BlockSpec((K, tn), lambda i, j: (0, j))],
        out_specs=pl.BlockSpec((tm, tn), lambda i, j: (i, j)),
        compiler_params=_params(("parallel", "parallel")),
        name="ffn_up",
    )(x, wg, wu)


def _gates_kernel(w_ref, u_ref, bias_ref, alog_ref, gt_ref, carry_ref, *, steps_per_seq, ts):
    t = pl.program_id(0)

    @pl.when(t % steps_per_seq == 0)
    def _():
        carry_ref[...] = jnp.zeros_like(carry_ref)

    z = lax.dot_general(w_ref[...], u_ref[...], _NT, preferred_element_type=F32) + bias_ref[...]
    row = lax.broadcasted_iota(jnp.int32, z.shape, 0)
    tail = jnp.log1p(jnp.exp(-jnp.abs(z)))
    log_sig = jnp.minimum(z, 0.0) - tail
    softplus = jnp.maximum(z, 0.0) + tail
    sig = 1.0 / (1.0 + jnp.exp(-z))
    decay = -jnp.exp(alog_ref[...]) * softplus
    val = jnp.where(row < FOX_HEADS, log_sig, jnp.where(row < FOX_HEADS + GDN_HEADS, decay, sig))

    src = lax.broadcasted_iota(jnp.int32, (ts, ts), 0)
    dst = lax.broadcasted_iota(jnp.int32, (ts, ts), 1)
    upper = src <= dst
    same_chunk = (src // CHUNK) == (dst // CHUNK)
    cum_all = jnp.dot(val, upper.astype(F32), precision=_HI, preferred_element_type=F32)
    cum_chunk = jnp.dot(val, (upper & same_chunk).astype(F32), precision=_HI,
                        preferred_element_type=F32)
    cum_all = cum_all + carry_ref[...]
    carry_ref[...] = cum_all[:, ts - 1:ts]
    gt_ref[...] = jnp.where(row < FOX_HEADS, cum_all,
                            jnp.where(row < FOX_HEADS + GDN_HEADS, cum_chunk, val))


def _gates(u, w_small_t, bias_col, alog_col, seq_len, ts=512):
    T, D = u.shape
    ts = min(ts, seq_len)
    return pl.pallas_call(
        functools.partial(_gates_kernel, steps_per_seq=seq_len // ts, ts=ts),
        out_shape=jax.ShapeDtypeStruct((GATE_LANES, T), F32),
        grid=(T // ts,),
        in_specs=[pl.BlockSpec((GATE_LANES, D), lambda t: (0, 0)),
                  pl.BlockSpec((ts, D), lambda t: (t, 0)),
                  pl.BlockSpec((GATE_LANES, 1), lambda t: (0, 0)),
                  pl.BlockSpec((GATE_LANES, 1), lambda t: (0, 0))],
        out_specs=pl.BlockSpec((GATE_LANES, ts), lambda t: (0, t)),
        scratch_shapes=[pltpu.VMEM((GATE_LANES, 1), F32)],
        compiler_params=_params(("arbitrary",)),
        name="gates",
    )(w_small_t, u, bias_col, alog_col)


FOX_PAIR = 2
_FOX_BUILD_ROWS = 512


def _fox_kernel(q_ref, k_ref, v_ref, gc_ref, g_ref, o_ref, kaug, vaug, m_ref, acc_ref, *,
                tq, tk, seq_len, pairs):
    D = HEAD_DIM
    pair = pl.program_id(0) % pairs
    qi = pl.program_id(1)
    inv_scale = D ** 0.5
    c_exp = (D ** -0.5) * math.log2(math.e)

    @pl.when(qi == 0)
    def _build():
        ri = lax.broadcasted_iota(jnp.int32, (3 * D, D), 0)
        ci = lax.broadcasted_iota(jnp.int32, (3 * D, D), 1)
        ones = jnp.ones((_FOX_BUILD_ROWS, D), BF16)

        def chunk(i, carry):
            r0 = pl.multiple_of(i * _FOX_BUILD_ROWS, _FOX_BUILD_ROWS)
            rows = pl.ds(r0, _FOX_BUILD_ROWS)
            g = gc_ref[rows, :] * (-inv_scale)
            hi = g.astype(BF16)
            r1 = g - hi.astype(F32)
            mid = r1.astype(BF16)
            lo = (r1 - mid.astype(F32)).astype(BF16)
            pieces = jnp.concatenate([hi, mid, lo], axis=1)
            for gg in range(FOX_PAIR):
                h = pair * FOX_PAIR + gg
                sel = (((ri == h) & (ci == 0)) | ((ri == D + h) & (ci == 1))
                       | ((ri == 2 * D + h) & (ci == 2)))
                aug = jnp.dot(pieces, sel.astype(BF16), preferred_element_type=F32)
                kaug[gg, rows, 0:D] = k_ref[rows, gg * D:(gg + 1) * D]
                kaug[gg, rows, D:2 * D] = aug.astype(BF16)
                vaug[gg, rows, 0:D] = v_ref[rows, gg * D:(gg + 1) * D]
                vaug[gg, rows, D:2 * D] = ones
            return carry

        lax.fori_loop(0, seq_len // _FOX_BUILD_ROWS, chunk, 0)

    lane = lax.broadcasted_iota(jnp.int32, (tq, D), 1)
    ones3 = jnp.where(lane < 3, 1.0, 0.0).astype(BF16)
    q_aug = [jnp.concatenate([q_ref[:, gg * D:(gg + 1) * D], ones3], axis=1)
             for gg in range(FOX_PAIR)]
    m_ref[...] = jnp.full_like(m_ref, -jnp.inf)
    acc_ref[...] = jnp.zeros_like(acc_ref)

    def step(kj, masked):
        k0 = pl.multiple_of(kj * tk, tk)
        for gg in range(FOX_PAIR):
            s = lax.dot_general(q_aug[gg], kaug[gg, pl.ds(k0, tk), :], _NT,
                                preferred_element_type=F32)
            if masked:
                r = qi * tq + lax.broadcasted_iota(jnp.int32, s.shape, 0)
                c = kj * tk + lax.broadcasted_iota(jnp.int32, s.shape, 1)
                s = jnp.where(c <= r, s, -jnp.inf)
            m_prev = m_ref[gg]
            m_new = jnp.maximum(m_prev, jnp.max(s, axis=-1, keepdims=True))
            alpha = jnp.exp2((m_prev - m_new) * c_exp)
            p = jnp.concatenate(
                [jnp.exp2((s[:, j * D:(j + 1) * D] - m_new) * c_exp) for j in range(tk // D)],
                axis=1).astype(BF16)
            pv = jnp.dot(p, vaug[gg, pl.ds(k0, tk), :], preferred_element_type=F32)
            acc_ref[gg] = jnp.concatenate([alpha, alpha], axis=1) * acc_ref[gg] + pv
            m_ref[gg] = m_new

    def body(kj, carry):
        step(kj, False)
        return carry

    n_full = (qi * tq) // tk
    lax.fori_loop(0, n_full, body, 0)
    step(n_full, True)

    for gg in range(FOX_PAIR):
        acc = acc_ref[gg]
        out = acc[:, 0:D] / acc[:, D:2 * D]
        ms = jnp.mean(out * out, axis=-1, keepdims=True)
        o_ref[:, gg * D:(gg + 1) * D] = (out * lax.rsqrt(ms + RMS_EPS) * g_ref[...]).astype(o_ref.dtype)


def _fox_attention(proj, gates_c, out_gain, batch, seq_len, tq=512, tk=1024):
    T = proj.shape[0]
    tq = min(tq, seq_len)
    tk = min(tk, seq_len)
    assert tk % tq == 0 and seq_len % tk == 0
    nq = seq_len // tq
    pairs = FOX_HEADS // FOX_PAIR
    W = FOX_PAIR * HEAD_DIM

    def rows_spec(base):
        return pl.BlockSpec((seq_len, W), lambda bp, qi: (bp // pairs, base // FOX_PAIR + bp % pairs))

    return pl.pallas_call(
        functools.partial(_fox_kernel, tq=tq, tk=tk, seq_len=seq_len, pairs=pairs),
        out_shape=jax.ShapeDtypeStruct((T, FOX_WIDTH), BF16),
        grid=(batch * pairs, nq),
        in_specs=[pl.BlockSpec((tq, W), lambda bp, qi: ((bp // pairs) * nq + qi,
                                                         _FQ // FOX_PAIR + bp % pairs)),
                  rows_spec(_FK), rows_spec(_FV),
                  pl.BlockSpec((seq_len, GATE_LANES), lambda bp, qi: (bp // pairs, 0)),
                  pl.BlockSpec((1, HEAD_DIM), lambda bp, qi: (0, 0))],
        out_specs=pl.BlockSpec((tq, W), lambda bp, qi: ((bp // pairs) * nq + qi, bp % pairs)),
        scratch_shapes=[pltpu.VMEM((FOX_PAIR, seq_len, 2 * HEAD_DIM), BF16),
                        pltpu.VMEM((FOX_PAIR, seq_len, 2 * HEAD_DIM), BF16),
                        pltpu.VMEM((FOX_PAIR, tq, HEAD_DIM), F32),
                        pltpu.VMEM((FOX_PAIR, tq, 2 * HEAD_DIM), F32)],
        compiler_params=_params(("parallel", "arbitrary")),
        name="fox_attention",
    )(proj, proj, proj, gates_c, out_gain.reshape(1, HEAD_DIM))


_HALO = 8


def _short_conv_kernel(b_ref, c_ref, h_ref, w_ref, o_ref, buf_ref, *, steps_per_seq, ts):
    t = pl.program_id(1)

    @pl.when(t % steps_per_seq == 0)
    def _():
        buf_ref[0:_HALO, :] = jnp.zeros((_HALO, buf_ref.shape[1]), F32)

    buf_ref[_HALO:_HALO + ts, :] = c_ref[...].astype(F32) * h_ref[...].astype(F32)
    w = w_ref[...]
    y = w[0:1, :] * buf_ref[pl.ds(_HALO - 2, ts), :]
    y += w[1:2, :] * buf_ref[pl.ds(_HALO - 1, ts), :]
    y += w[2:3, :] * buf_ref[pl.ds(_HALO, ts), :]
    o_ref[...] = (b_ref[...].astype(F32) * y).astype(o_ref.dtype)
    buf_ref[0:_HALO, :] = buf_ref[ts:ts + _HALO, :]


def _short_conv(proj, conv_w, seq_len, ts=512, tc=512):
    T = proj.shape[0]
    ts = min(ts, seq_len)
    per = tc // HEAD_DIM

    def col(base):
        return pl.BlockSpec((ts, tc), lambda c, t: (t, base // per + c))

    return pl.pallas_call(
        functools.partial(_short_conv_kernel, steps_per_seq=seq_len // ts, ts=ts),
        out_shape=jax.ShapeDtypeStruct((T, SC_WIDTH), BF16),
        grid=(SC_WIDTH // tc, T // ts),
        in_specs=[col(_SB), col(_SC), col(_SH),
                  pl.BlockSpec((SC_KERNEL, tc), lambda c, t: (0, c))],
        out_specs=pl.BlockSpec((ts, tc), lambda c, t: (t, c)),
        scratch_shapes=[pltpu.VMEM((_HALO + ts, tc), F32)],
        compiler_params=_params(("parallel", "arbitrary")),
        name="short_conv",
    )(proj, proj, proj, conv_w)


def _bmm(a, b, dims):
    return lax.dot_general(a.astype(BF16), b.astype(BF16), dims, preferred_element_type=F32)


_B_NN = (((2,), (1,)), ((0,), (0,)))
_B_NT = (((2,), (2,)), ((0,), (0,)))


GDN_PAIR = 2


def _gdn_kernel(q_ref, k_ref, v_ref, z_ref, wq_ref, wk_ref, wv_ref, gc_ref, gr_ref, gain_ref,
                o_ref, qbuf, kbuf, vbuf, state_ref, obuf, *, rows):
    hp = pl.program_id(1)
    t = pl.program_id(2)
    n = rows // CHUNK
    D = HEAD_DIM

    @pl.when(t == 0)
    def _():
        zeros = jnp.zeros((_HALO, GDN_PAIR * D), F32)
        qbuf[0:_HALO, :] = zeros
        kbuf[0:_HALO, :] = zeros
        vbuf[0:_HALO, :] = zeros
        state_ref[...] = jnp.zeros_like(state_ref)

    def conv_silu(x_ref, w_ref, buf):
        buf[_HALO:_HALO + rows, :] = x_ref[...].astype(F32)
        w = w_ref[...]
        y = w[0:1, :] * buf[pl.ds(_HALO - 3, rows), :]
        y += w[1:2, :] * buf[pl.ds(_HALO - 2, rows), :]
        y += w[2:3, :] * buf[pl.ds(_HALO - 1, rows), :]
        y += w[3:4, :] * buf[pl.ds(_HALO, rows), :]
        buf[0:_HALO, :] = buf[rows:rows + _HALO, :]
        return y * jax.nn.sigmoid(y)

    q_all = conv_silu(q_ref, wq_ref, qbuf)
    k_all = conv_silu(k_ref, wk_ref, kbuf)
    v_all = conv_silu(v_ref, wv_ref, vbuf)

    gates = gc_ref[...]
    lane = lax.broadcasted_iota(jnp.int32, gates.shape, 1)
    ri = lax.broadcasted_iota(jnp.int32, (CHUNK, CHUNK), 0)
    ci = lax.broadcasted_iota(jnp.int32, (CHUNK, CHUNK), 1)
    tri_incl = (ci <= ri)[None]
    tri_strict = (ci < ri)[None]
    same16 = ((ri // 16) == (ci // 16))[None]
    same32 = ((ri // 32) == (ci // 32))[None]
    eye = (ri == ci).astype(F32)[None]

    def chunk_terms(gg):
        h = hp * GDN_PAIR + gg
        cols = slice(gg * D, (gg + 1) * D)
        q, k, v = q_all[:, cols], k_all[:, cols], v_all[:, cols]
        q = q * (lax.rsqrt(jnp.sum(q * q, axis=-1, keepdims=True) + RMS_EPS) * (D ** -0.5))
        k = k * lax.rsqrt(jnp.sum(k * k, axis=-1, keepdims=True) + RMS_EPS)
        g_cum = jnp.sum(jnp.where(lane == FOX_HEADS + h, gates, 0.0), axis=-1, keepdims=True)
        beta = jnp.sum(jnp.where(lane == FOX_HEADS + GDN_HEADS + h, gates, 0.0), axis=-1,
                       keepdims=True)
        q3 = q.reshape(n, CHUNK, D)
        k3 = k.reshape(n, CHUNK, D)
        v3 = v.reshape(n, CHUNK, D)
        g3 = g_cum.reshape(n, CHUNK, 1)
        b3 = beta.reshape(n, CHUNK, 1)
        g_row = gr_ref[:, pl.ds(FOX_HEADS + h, 1), :]
        g_last = g3[:, CHUNK - 1:CHUNK, :]

        decay = jnp.exp(jnp.where(tri_incl, g3 - g_row, -jnp.inf))
        kk = _bmm(k3, k3, _B_NT)
        L = jnp.where(tri_strict, b3 * kk * decay, 0.0)

        P = jnp.where(same16, L, 0.0)
        X = eye - P
        P2 = _bmm(P, P, _B_NN)
        X = _bmm(X, eye + P2, _B_NN)
        P4 = _bmm(P2, P2, _B_NN)
        X = _bmm(X, eye + P4, _B_NN)
        P8 = _bmm(P4, P4, _B_NN)
        X = _bmm(X, eye + P8, _B_NN)
        O32 = jnp.where(same32 & jnp.logical_not(same16), L, 0.0)
        X = X - _bmm(_bmm(X, O32, _B_NN), X, _B_NN)
        O64 = jnp.where(same32, 0.0, L)
        X = X - _bmm(_bmm(X, O64, _B_NN), X, _B_NN)

        e3 = jnp.exp(g3)
        rhs = jnp.concatenate([v3 * b3, k3 * (b3 * e3)], axis=-1)
        sol = _bmm(X, rhs, _B_NN)
        attn = jnp.where(tri_incl, _bmm(q3, k3, _B_NT) * decay, 0.0)
        return dict(u=sol[:, :, :D], w=sol[:, :, D:].astype(BF16), q=(q3 * e3).astype(BF16),
                    a=attn.astype(BF16), k=(k3 * jnp.exp(g_last - g3)).astype(BF16),
                    gl=jnp.exp(g_last))

    terms = [chunk_terms(gg) for gg in range(GDN_PAIR)]
    S = [state_ref[gg] for gg in range(GDN_PAIR)]
    for c in range(n):
        for gg in range(GDN_PAIR):
            tm = terms[gg]
            S_b = S[gg].astype(BF16)
            v_new = tm["u"][c] - jnp.dot(tm["w"][c], S_b, preferred_element_type=F32)
            v_b = v_new.astype(BF16)
            o_c = jnp.dot(tm["q"][c], S_b, preferred_element_type=F32)
            o_c += jnp.dot(tm["a"][c], v_b, preferred_element_type=F32)
            S[gg] = S[gg] * tm["gl"][c] + lax.dot_general(tm["k"][c], v_b, _TN,
                                                         preferred_element_type=F32)
            obuf[c * CHUNK:(c + 1) * CHUNK, gg * D:(gg + 1) * D] = o_c
    for gg in range(GDN_PAIR):
        state_ref[gg] = S[gg]

    for gg in range(GDN_PAIR):
        cols = slice(gg * D, (gg + 1) * D)
        o = obuf[:, cols]
        ms = jnp.mean(o * o, axis=-1, keepdims=True)
        z = z_ref[:, cols].astype(F32)
        o = o * lax.rsqrt(ms + RMS_EPS) * gain_ref[...] * (z * jax.nn.sigmoid(z))
        o_ref[:, cols] = o.astype(o_ref.dtype)


def _gdn(proj, conv_w, gates_c, gates_r, out_gain, batch, seq_len, rows=512):
    T = proj.shape[0]
    rows = min(rows, seq_len)
    nt = seq_len // rows
    n = rows // CHUNK
    pairs = GDN_HEADS // GDN_PAIR
    W = GDN_PAIR * HEAD_DIM

    def col(base):
        return pl.BlockSpec((rows, W), lambda b, hp, t: (b * nt + t, base // GDN_PAIR + hp))

    def wcol(base):
        return pl.BlockSpec((GDN_CONV, W), lambda b, hp, t: (0, base // GDN_PAIR + hp))

    return pl.pallas_call(
        functools.partial(_gdn_kernel, rows=rows),
        out_shape=jax.ShapeDtypeStruct((T, GDN_WIDTH), BF16),
        grid=(batch, pairs, nt),
        in_specs=[col(_GQ), col(_GK), col(_GV), col(_GZ),
                  wcol(0), wcol(GDN_HEADS), wcol(2 * GDN_HEADS),
                  pl.BlockSpec((rows, GATE_LANES), lambda b, hp, t: (b * nt + t, 0)),
                  pl.BlockSpec((n, GATE_LANES, CHUNK), lambda b, hp, t: (b * nt + t, 0, 0)),
                  pl.BlockSpec((1, HEAD_DIM), lambda b, hp, t: (0, 0))],
        out_specs=pl.BlockSpec((rows, W), lambda b, hp, t: (b * nt + t, hp)),
        scratch_shapes=[pltpu.VMEM((_HALO + rows, W), F32),
                        pltpu.VMEM((_HALO + rows, W), F32),
                        pltpu.VMEM((_HALO + rows, W), F32),
                        pltpu.VMEM((GDN_PAIR, HEAD_DIM, HEAD_DIM), F32),
                        pltpu.VMEM((rows, W), F32)],
        compiler_params=_params(("parallel", "parallel", "arbitrary")),
        name="gdn",
    )(proj, proj, proj, proj, conv_w, conv_w, conv_w, gates_c, gates_r,
      out_gain.reshape(1, HEAD_DIM))


def _split_w_in(w_in):
    sizes = (FOX_WIDTH, FOX_WIDTH, FOX_WIDTH, FOX_HEADS, SC_WIDTH, SC_WIDTH, SC_WIDTH,
             GDN_WIDTH, GDN_WIDTH, GDN_WIDTH, GDN_HEADS, GDN_HEADS, GDN_WIDTH)
    offs = [0]
    for s in sizes:
        offs.append(offs[-1] + s)
    part = [w_in[:, offs[i]:offs[i + 1]] for i in range(len(sizes))]
    fq, fk, fv, ff, sb, sc, sh, gq, gk, gv, ga, gb, gz = part
    w_main = jnp.concatenate([fq, fk, fv, sb, sc, sh, gq, gk, gv, gz], axis=1).astype(BF16)
    small = jnp.concatenate([ff, ga, gb], axis=1)
    small = jnp.pad(small, ((0, 0), (0, GATE_LANES - small.shape[1])))
    return w_main, small.T.astype(BF16)


def _gate_columns(fox_forget_bias, gdn_a_log, gdn_dt_bias):
    pad = GATE_LANES - FOX_HEADS - GDN_HEADS
    bias = jnp.concatenate([fox_forget_bias, gdn_dt_bias, jnp.zeros((pad,), F32)])
    alog = jnp.concatenate([jnp.zeros((FOX_HEADS,), F32), gdn_a_log, jnp.zeros((pad,), F32)])
    return bias.reshape(GATE_LANES, 1), alog.reshape(GATE_LANES, 1)


def _layer(x, u, batch, seq_len, layer, w_in, fox_forget_bias, fox_out_norm, sc_conv_w, gdn_conv_w,
           gdn_a_log, gdn_dt_bias, gdn_out_norm, w_out, mix_post_norm, ffn_pre_norm,
           w_gate, w_up, w_down, ffn_post_norm, next_pre_norm):
    T, D = x.shape
    w_main, w_small_t = _split_w_in(w_in)
    bias_col, alog_col = _gate_columns(fox_forget_bias, gdn_a_log, gdn_dt_bias)

    proj = _matmul(u, w_main, BF16, tm=1024, tn=768, name="in_proj")
    gates_t = _gates(u, w_small_t, bias_col, alog_col, seq_len)
    gates_c = gates_t.T
    gates_r = gates_t.reshape(GATE_LANES, T // CHUNK, CHUNK).transpose(1, 0, 2)

    fox_out = _fox_attention(proj, gates_c, fox_out_norm, batch, seq_len)
    sc_out = _short_conv(proj, sc_conv_w, seq_len)
    gdn_out = _gdn(proj, gdn_conv_w, gates_c, gates_r, gdn_out_norm, batch, seq_len)

    w_out_b = _cast_weight(w_out, layer)
    y = _out_proj(fox_out, sc_out, gdn_out,
                  w_out_b[:FOX_WIDTH], w_out_b[FOX_WIDTH:FOX_WIDTH + SC_WIDTH],
                  w_out_b[FOX_WIDTH + SC_WIDTH:])
    h, v = _norm_residual(y, x, mix_post_norm, ffn_pre_norm)

    hidden = w_gate.shape[2]
    kd = hidden + (-hidden) % FFN_PAD
    wg = _cast_weight(w_gate, layer, cols_pad=kd)
    wu = _cast_weight(w_up, layer, cols_pad=kd)
    wd = _cast_weight(w_down, layer, rows_pad=kd)
    act = _ffn_up(v, wg, wu)
    y2 = _matmul(act, wd, F32, tm=1024, tn=1024, tk=kd // 4, name="ffn_down")
    return _norm_residual(y2, h, ffn_post_norm, next_pre_norm)


def kernel(x, mix_pre_norm, w_in, fox_forget_bias, fox_out_norm, sc_conv_w, gdn_conv_w, gdn_a_log,
           gdn_dt_bias, gdn_out_norm, w_out, mix_post_norm, ffn_pre_norm, w_gate, w_up, w_down,
           ffn_post_norm):
    B, S, D = x.shape
    depth = w_in.shape[0]
    h = x.reshape(B * S, D)
    u = _norm_cast(h, mix_pre_norm[0])
    for l in range(depth):
        nxt = mix_pre_norm[l + 1] if l + 1 < depth else None
        h, u = _layer(h, u, B, S, l, w_in[l], fox_forget_bias[l], fox_out_norm[l], sc_conv_w[l],
                      gdn_conv_w[l], gdn_a_log[l], gdn_dt_bias[l], gdn_out_norm[l], w_out,
                      mix_post_norm[l], ffn_pre_norm[l], w_gate, w_up, w_down,
                      ffn_post_norm[l], nxt)
    return h.reshape(B, S, D)
```

```python
import functools
import math

import jax
import jax.numpy as jnp
from jax import lax
from jax.experimental import pallas as pl
from jax.experimental.pallas import tpu as pltpu

F32 = jnp.float32
BF16 = jnp.bfloat16

HEAD_DIM = 128
FOX_HEADS = 12
FOX_WIDTH = FOX_HEADS * HEAD_DIM
SC_WIDTH = 8 * HEAD_DIM
SC_KERNEL = 3
GDN_HEADS = 12
GDN_WIDTH = GDN_HEADS * HEAD_DIM
GDN_CONV = 4
CHUNK = 64
RMS_EPS = 1e-6
GATE_LANES = 128
FFN_PAD = 1024

VMEM_LIMIT_BYTES = 56 * 1024 * 1024

_FQ, _FK, _FV = 0, 12, 24
_SB, _SC, _SH = 0, 8, 16
_GQ, _GK, _GV = 24, 36, 48
_GZ = 0
_A_COL, _A_WIDTH = 0, 3 * FOX_WIDTH
_B_COL, _B_WIDTH = 3 * FOX_WIDTH + FOX_HEADS, 3 * SC_WIDTH + 3 * GDN_WIDTH
_C_COL, _C_WIDTH = _B_COL + _B_WIDTH + 2 * GDN_HEADS, GDN_WIDTH

_NT = (((1,), (1,)), ((), ()))
_TN = (((0,), (0,)), ((), ()))
_HI = lax.Precision.HIGHEST


def _params(sem):
    return pltpu.CompilerParams(dimension_semantics=sem, vmem_limit_bytes=VMEM_LIMIT_BYTES)


def _norm_cast_kernel(x_ref, g_ref, o_ref):
    x = x_ref[...]
    ms = jnp.mean(x * x, axis=-1, keepdims=True)
    o_ref[...] = (x * lax.rsqrt(ms + RMS_EPS) * g_ref[...]).astype(o_ref.dtype)


def _norm_cast(x, gain, tm=256):
    T, D = x.shape
    tm = min(tm, T)
    return pl.pallas_call(
        _norm_cast_kernel,
        out_shape=jax.ShapeDtypeStruct((T, D), BF16),
        grid=(T // tm,),
        in_specs=[pl.BlockSpec((tm, D), lambda i: (i, 0)),
                  pl.BlockSpec((1, D), lambda i: (0, 0))],
        out_specs=pl.BlockSpec((tm, D), lambda i: (i, 0)),
        compiler_params=_params(("parallel",)),
        name="norm_cast",
    )(x, gain.reshape(1, D))


def _norm_residual_kernel(y_ref, x_ref, g_ref, gn_ref, h_ref, *maybe_u_ref):
    y = y_ref[...]
    ms = jnp.mean(y * y, axis=-1, keepdims=True)
    h = x_ref[...] + y * lax.rsqrt(ms + RMS_EPS) * g_ref[...]
    h_ref[...] = h
    if maybe_u_ref:
        ms2 = jnp.mean(h * h, axis=-1, keepdims=True)
        maybe_u_ref[0][...] = (h * lax.rsqrt(ms2 + RMS_EPS) * gn_ref[...]).astype(BF16)


def _norm_residual(y, x, gain, next_gain, tm=256):
    T, D = x.shape
    tm = min(tm, T)
    emit_next = next_gain is not None
    gn = (next_gain if emit_next else gain).reshape(1, D)
    row = pl.BlockSpec((tm, D), lambda i: (i, 0))
    vec = pl.BlockSpec((1, D), lambda i: (0, 0))
    out_shape = [jax.ShapeDtypeStruct((T, D), F32)]
    out_specs = [row]
    if emit_next:
        out_shape.append(jax.ShapeDtypeStruct((T, D), BF16))
        out_specs.append(row)
    res = pl.pallas_call(
        _norm_residual_kernel,
        out_shape=out_shape,
        grid=(T // tm,),
        in_specs=[row, row, vec, vec],
        out_specs=out_specs,
        compiler_params=_params(("parallel",)),
        name="norm_residual",
    )(y, x, gain.reshape(1, D), gn)
    return (res[0], res[1]) if emit_next else (res[0], None)


def _cast_weight_kernel(w_ref, o_ref, *, rows, cols, tr, tc, masked):
    w = w_ref[0]
    if masked:
        ri = pl.program_id(0) * tr + lax.broadcasted_iota(jnp.int32, w.shape, 0)
        ci = pl.program_id(1) * tc + lax.broadcasted_iota(jnp.int32, w.shape, 1)
        w = jnp.where((ri < rows) & (ci < cols), w, 0.0)
    o_ref[...] = w.astype(o_ref.dtype)


def _cast_weight(w_stack, layer, rows_pad=None, cols_pad=None, tr=512, tc=1024):
    _, rows, cols = w_stack.shape
    rows_pad = rows if rows_pad is None else rows_pad
    cols_pad = cols if cols_pad is None else cols_pad
    masked = (rows_pad != rows) or (cols_pad != cols)
    return pl.pallas_call(
        functools.partial(_cast_weight_kernel, rows=rows, cols=cols, tr=tr, tc=tc, masked=masked),
        out_shape=jax.ShapeDtypeStruct((rows_pad, cols_pad), BF16),
        grid=(rows_pad // tr, cols_pad // tc),
        in_specs=[pl.BlockSpec((1, tr, tc), lambda i, j: (layer, i, j))],
        out_specs=pl.BlockSpec((tr, tc), lambda i, j: (i, j)),
        compiler_params=_params(("parallel", "parallel")),
        name="cast_weight",
    )(w_stack)


def _cast_window_kernel(a_ref, b_ref, o_ref, *, shift):
    if shift == 0:
        o_ref[...] = a_ref[0].astype(o_ref.dtype)
        return
    x = jnp.concatenate([a_ref[0], b_ref[0]], axis=1)
    width = x.shape[1]
    y = pltpu.roll(x, width - shift, axis=1)
    o_ref[...] = y[:, :o_ref.shape[1]].astype(o_ref.dtype)


def _cast_window(w_stack, layer, src_col, width, tr=512, tc=512):
    rows = w_stack.shape[1]
    base = (src_col // HEAD_DIM) * HEAD_DIM
    shift = src_col - base
    assert base % tc == 0 and width % tc == 0 and rows % tr == 0
    return pl.pallas_call(
        functools.partial(_cast_window_kernel, shift=shift),
        out_shape=jax.ShapeDtypeStruct((rows, width), BF16),
        grid=(rows // tr, width // tc),
        in_specs=[pl.BlockSpec((1, tr, tc), lambda i, j: (layer, i, base // tc + j)),
                  pl.BlockSpec((1, tr, HEAD_DIM),
                               lambda i, j: (layer, i, (base + (j + 1) * tc) // HEAD_DIM))],
        out_specs=pl.BlockSpec((tr, tc), lambda i, j: (i, j)),
        compiler_params=_params(("parallel", "parallel")),
        name="cast_window",
    )(w_stack, w_stack)


def _matmul_scaled_kernel(x_ref, w_ref, s_ref, o_ref):
    acc = jnp.dot(x_ref[...], w_ref[...], preferred_element_type=F32)
    o_ref[...] = (acc * s_ref[...]).astype(o_ref.dtype)


def _matmul_scaled(x, w, col_scale, out_dtype, tm, tn, name):
    M, K = x.shape
    _, N = w.shape
    tm, tn = min(tm, M), min(tn, N)
    return pl.pallas_call(
        _matmul_scaled_kernel,
        out_shape=jax.ShapeDtypeStruct((M, N), out_dtype),
        grid=(M // tm, N // tn),
        in_specs=[pl.BlockSpec((tm, K), lambda i, j: (i, 0)),
                  pl.BlockSpec((K, tn), lambda i, j: (0, j)),
                  pl.BlockSpec((1, tn), lambda i, j: (0, j))],
        out_specs=pl.BlockSpec((tm, tn), lambda i, j: (i, j)),
        compiler_params=_params(("parallel", "parallel")),
        name=name,
    )(x, w, col_scale)


def _matmul_kernel(x_ref, w_ref, o_ref, acc_ref, *, nk):
    part = jnp.dot(x_ref[...], w_ref[...], preferred_element_type=F32)
    if nk == 1:
        o_ref[...] = part.astype(o_ref.dtype)
        return
    k = pl.program_id(2)

    @pl.when(k == 0)
    def _():
        acc_ref[...] = part

    @pl.when(k > 0)
    def _():
        acc_ref[...] += part

    @pl.when(k == nk - 1)
    def _():
        o_ref[...] = acc_ref[...].astype(o_ref.dtype)


def _matmul(x, w, out_dtype, tm, tn, tk=None, name="matmul"):
    M, K = x.shape
    _, N = w.shape
    tm, tn = min(tm, M), min(tn, N)
    tk = K if tk is None else tk
    nk = K // tk
    acc_shape = (tm, tn) if nk > 1 else (8, 128)
    return pl.pallas_call(
        functools.partial(_matmul_kernel, nk=nk),
        out_shape=jax.ShapeDtypeStruct((M, N), out_dtype),
        grid=(M // tm, N // tn, nk),
        in_specs=[pl.BlockSpec((tm, tk), lambda i, j, k: (i, k)),
                  pl.BlockSpec((tk, tn), lambda i, j, k: (k, j))],
        out_specs=pl.BlockSpec((tm, tn), lambda i, j, k: (i, j)),
        scratch_shapes=[pltpu.VMEM(acc_shape, F32)],
        compiler_params=_params(("parallel", "parallel", "arbitrary")),
        name=name,
    )(x, w)


def _out_proj_kernel(a1_ref, a2_ref, a3_ref, w1_ref, w2_ref, w3_ref, o_ref):
    acc = jnp.dot(a1_ref[...], w1_ref[...], preferred_element_type=F32)
    acc += jnp.dot(a2_ref[...], w2_ref[...], preferred_element_type=F32)
    acc += jnp.dot(a3_ref[...], w3_ref[...], preferred_element_type=F32)
    o_ref[...] = acc


def _out_proj(a1, a2, a3, w1, w2, w3, tm=1024, tn=1024):
    T = a1.shape[0]
    N = w1.shape[1]
    tm, tn = min(tm, T), min(tn, N)

    def lhs(a):
        return pl.BlockSpec((tm, a.shape[1]), lambda i, j: (i, 0))

    def rhs(w):
        return pl.BlockSpec((w.shape[0], tn), lambda i, j: (0, j))

    return pl.pallas_call(
        _out_proj_kernel,
        out_shape=jax.ShapeDtypeStruct((T, N), F32),
        grid=(T // tm, N // tn),
        in_specs=[lhs(a1), lhs(a2), lhs(a3), rhs(w1), rhs(w2), rhs(w3)],
        out_specs=pl.BlockSpec((tm, tn), lambda i, j: (i, j)),
        compiler_params=_params(("parallel", "parallel")),
        name="out_proj",
    )(a1, a2, a3, w1, w2, w3)


def _ffn_up_kernel(x_ref, wg_ref, wu_ref, o_ref):
    x = x_ref[...]
    g = jnp.dot(x, wg_ref[...], preferred_element_type=F32)
    u = jnp.dot(x, wu_ref[...], preferred_element_type=F32)
    o_ref[...] = (g * jax.nn.sigmoid(g) * u).astype(o_ref.dtype)


def _ffn_up(x, wg, wu, tm=1024, tn=512):
    T, K = x.shape
    N = wg.shape[1]
    tm, tn = min(tm, T), min(tn, N)
    return pl.pallas_call(
        _ffn_up_kernel,
        out_shape=jax.ShapeDtypeStruct((T, N), BF16),
        grid=(T // tm, N // tn),
        in_specs=[pl.BlockSpec((tm, K), lambda i, j: (i, 0)),
                  pl.BlockSpec((K, tn), lambda i, j: (0, j)),
                  pl.BlockSpec((K, tn), lambda i, j: (0, j))],
        out_specs=pl.BlockSpec((tm, tn), lambda i, j: (i, j)),
        compiler_params=_params(("parallel", "parallel")),
        name="ffn_up",
    )(x, wg, wu)


def _gates_kernel(w_ref, u_ref, bias_ref, alog_ref, gt_ref, carry_ref, *, steps_per_seq, ts):
    t = pl.program_id(0)

    @pl.when(t % steps_per_seq == 0)
    def _():
        carry_ref[...] = jnp.zeros_like(carry_ref)

    z = lax.dot_general(w_ref[...], u_ref[...], _NT, preferred_element_type=F32) + bias_ref[...]
    row = lax.broadcasted_iota(jnp.int32, z.shape, 0)
    tail = jnp.log1p(jnp.exp(-jnp.abs(z)))
    log_sig = jnp.minimum(z, 0.0) - tail
    softplus = jnp.maximum(z, 0.0) + tail
    sig = 1.0 / (1.0 + jnp.exp(-z))
    decay = -jnp.exp(alog_ref[...]) * softplus
    val = jnp.where(row < FOX_HEADS, log_sig, jnp.where(row < FOX_HEADS + GDN_HEADS, decay, sig))

    src = lax.broadcasted_iota(jnp.int32, (ts, ts), 0)
    dst = lax.broadcasted_iota(jnp.int32, (ts, ts), 1)
    upper = src <= dst
    same_chunk = (src // CHUNK) == (dst // CHUNK)
    cum_all = jnp.dot(val, upper.astype(F32), precision=_HI, preferred_element_type=F32)
    cum_chunk = jnp.dot(val, (upper & same_chunk).astype(F32), precision=_HI,
                        preferred_element_type=F32)
    cum_all = cum_all + carry_ref[...]
    carry_ref[...] = cum_all[:, ts - 1:ts]
    gt_ref[...] = jnp.where(row < FOX_HEADS, cum_all,
                            jnp.where(row < FOX_HEADS + GDN_HEADS, cum_chunk, val))


def _gates(u, w_small_t, bias_col, alog_col, seq_len, ts=512):
    T, D = u.shape
    ts = min(ts, seq_len)
    return pl.pallas_call(
        functools.partial(_gates_kernel, steps_per_seq=seq_len // ts, ts=ts),
        out_shape=jax.ShapeDtypeStruct((GATE_LANES, T), F32),
        grid=(T // ts,),
        in_specs=[pl.BlockSpec((GATE_LANES, D), lambda t: (0, 0)),
                  pl.BlockSpec((ts, D), lambda t: (t, 0)),
                  pl.BlockSpec((GATE_LANES, 1), lambda t: (0, 0)),
                  pl.BlockSpec((GATE_LANES, 1), lambda t: (0, 0))],
        out_specs=pl.BlockSpec((GATE_LANES, ts), lambda t: (0, t)),
        scratch_shapes=[pltpu.VMEM((GATE_LANES, 1), F32)],
        compiler_params=_params(("arbitrary",)),
        name="gates",
    )(w_small_t, u, bias_col, alog_col)


FOX_Q_SCALE = (HEAD_DIM ** -0.5) * math.log2(math.e)
FOX_PAIR = 2
_FOX_BUILD_ROWS = 512


def _fox_kernel(q_ref, k_ref, v_ref, gc_ref, g_ref, o_ref, kaug, vaug, m_ref, acc_ref, sa_ref,
                sb_ref, *,
                tq, tk, seq_len, pairs):
    D = HEAD_DIM
    pair = pl.program_id(0) % pairs
    qi = pl.program_id(1)

    @pl.when(qi == 0)
    def _build():
        ri = lax.broadcasted_iota(jnp.int32, (3 * D, D), 0)
        ci = lax.broadcasted_iota(jnp.int32, (3 * D, D), 1)
        ones = jnp.ones((_FOX_BUILD_ROWS, D), BF16)

        def chunk(i, carry):
            r0 = pl.multiple_of(i * _FOX_BUILD_ROWS, _FOX_BUILD_ROWS)
            rows = pl.ds(r0, _FOX_BUILD_ROWS)
            g = gc_ref[rows, :] * (-math.log2(math.e))
            hi = g.astype(BF16)
            r1 = g - hi.astype(F32)
            mid = r1.astype(BF16)
            lo = (r1 - mid.astype(F32)).astype(BF16)
            pieces = jnp.concatenate([hi, mid, lo], axis=1)
            for gg in range(FOX_PAIR):
                h = pair * FOX_PAIR + gg
                sel = (((ri == h) & (ci == 0)) | ((ri == D + h) & (ci == 1))
                       | ((ri == 2 * D + h) & (ci == 2)))
                aug = jnp.dot(pieces, sel.astype(BF16), preferred_element_type=F32)
                kaug[gg, rows, 0:D] = k_ref[rows, gg * D:(gg + 1) * D]
                kaug[gg, rows, D:2 * D] = aug.astype(BF16)
                vaug[gg, rows, 0:D] = v_ref[rows, gg * D:(gg + 1) * D]
                vaug[gg, rows, D:2 * D] = ones
            return carry

        lax.fori_loop(0, seq_len // _FOX_BUILD_ROWS, chunk, 0)

    lane = lax.broadcasted_iota(jnp.int32, (tq, D), 1)
    ones3 = jnp.where(lane < 3, 1.0, 0.0).astype(BF16)
    q_aug = [jnp.concatenate([q_ref[:, gg * D:(gg + 1) * D], ones3], axis=1)
             for gg in range(FOX_PAIR)]
    m_ref[...] = jnp.full_like(m_ref, -jnp.inf)
    acc_ref[...] = jnp.zeros_like(acc_ref)

    def scores(kj, slot_ref):
        k0 = pl.multiple_of(kj * tk, tk)
        for gg in range(FOX_PAIR):
            slot_ref[gg] = lax.dot_general(q_aug[gg], kaug[gg, pl.ds(k0, tk), :], _NT,
                                           preferred_element_type=F32)

    def accumulate(kj, slot_ref, masked=False):
        k0 = pl.multiple_of(kj * tk, tk)
        for gg in range(FOX_PAIR):
            s = slot_ref[gg]
            if masked:
                r = qi * tq + lax.broadcasted_iota(jnp.int32, s.shape, 0)
                c = kj * tk + lax.broadcasted_iota(jnp.int32, s.shape, 1)
                s = jnp.where(c <= r, s, -jnp.inf)
            m_prev = m_ref[gg]
            m_new = jnp.maximum(m_prev, jnp.max(s, axis=-1, keepdims=True))
            alpha = jnp.exp2(m_prev - m_new)
            p = jnp.concatenate(
                [jnp.exp2(s[:, j * D:(j + 1) * D] - m_new) for j in range(tk // D)],
                axis=1).astype(BF16)
            pv = jnp.dot(p, vaug[gg, pl.ds(k0, tk), :], preferred_element_type=F32)
            acc_ref[gg] = jnp.concatenate([alpha, alpha], axis=1) * acc_ref[gg] + pv
            m_ref[gg] = m_new

    n_full = (qi * tq) // tk
    scores(0, sa_ref)

    def body(i, carry):
        scores(2 * i + 1, sb_ref)
        accumulate(2 * i, sa_ref)
        scores(2 * i + 2, sa_ref)
        accumulate(2 * i + 1, sb_ref)
        return carry

    lax.fori_loop(0, n_full // 2, body, 0)

    @pl.when(n_full % 2 == 0)
    def _():
        accumulate(n_full, sa_ref, masked=True)

    @pl.when(n_full % 2 == 1)
    def _():
        scores(n_full, sb_ref)
        accumulate(n_full - 1, sa_ref)
        accumulate(n_full, sb_ref, masked=True)

    for gg in range(FOX_PAIR):
        acc = acc_ref[gg]
        out = acc[:, 0:D] / acc[:, D:2 * D]
        ms = jnp.mean(out * out, axis=-1, keepdims=True)
        o_ref[:, gg * D:(gg + 1) * D] = (out * lax.rsqrt(ms + RMS_EPS) * g_ref[...]).astype(o_ref.dtype)


def _fox_attention(proj, gates_c, out_gain, batch, seq_len, tq=512, tk=1024):
    T = proj.shape[0]
    tq = min(tq, seq_len)
    tk = min(tk, seq_len)
    assert tk % tq == 0 and seq_len % tk == 0
    nq = seq_len // tq
    pairs = FOX_HEADS // FOX_PAIR
    W = FOX_PAIR * HEAD_DIM

    once = pl.Buffered(1)

    def rows_spec(base):
        return pl.BlockSpec((seq_len, W), lambda bp, qi: (bp // pairs, base // FOX_PAIR + bp % pairs),
                            pipeline_mode=once)

    return pl.pallas_call(
        functools.partial(_fox_kernel, tq=tq, tk=tk, seq_len=seq_len, pairs=pairs),
        out_shape=jax.ShapeDtypeStruct((T, FOX_WIDTH), BF16),
        grid=(batch * pairs, nq),
        in_specs=[pl.BlockSpec((tq, W), lambda bp, qi: ((bp // pairs) * nq + qi,
                                                         _FQ // FOX_PAIR + bp % pairs)),
                  rows_spec(_FK), rows_spec(_FV),
                  pl.BlockSpec((seq_len, GATE_LANES), lambda bp, qi: (bp // pairs, 0),
                               pipeline_mode=once),
                  pl.BlockSpec((1, HEAD_DIM), lambda bp, qi: (0, 0))],
        out_specs=pl.BlockSpec((tq, W), lambda bp, qi: ((bp // pairs) * nq + qi, bp % pairs)),
        scratch_shapes=[pltpu.VMEM((FOX_PAIR, seq_len, 2 * HEAD_DIM), BF16),
                        pltpu.VMEM((FOX_PAIR, seq_len, 2 * HEAD_DIM), BF16),
                        pltpu.VMEM((FOX_PAIR, tq, HEAD_DIM), F32),
                        pltpu.VMEM((FOX_PAIR, tq, 2 * HEAD_DIM), F32),
                        pltpu.VMEM((FOX_PAIR, tq, tk), F32),
                        pltpu.VMEM((FOX_PAIR, tq, tk), F32)],
        compiler_params=_params(("parallel", "arbitrary")),
        name="fox_attention",
    )(proj, proj, proj, gates_c, out_gain.reshape(1, HEAD_DIM))


_HALO = 8


def _short_conv_kernel(b_ref, c_ref, h_ref, w_ref, o_ref, buf_ref, *, steps_per_seq, ts):
    t = pl.program_id(1)

    @pl.when(t % steps_per_seq == 0)
    def _():
        buf_ref[0:_HALO, :] = jnp.zeros((_HALO, buf_ref.shape[1]), F32)

    buf_ref[_HALO:_HALO + ts, :] = c_ref[...].astype(F32) * h_ref[...].astype(F32)
    w = w_ref[...]
    y = w[0:1, :] * buf_ref[pl.ds(_HALO - 2, ts), :]
    y += w[1:2, :] * buf_ref[pl.ds(_HALO - 1, ts), :]
    y += w[2:3, :] * buf_ref[pl.ds(_HALO, ts), :]
    o_ref[...] = (b_ref[...].astype(F32) * y).astype(o_ref.dtype)
    buf_ref[0:_HALO, :] = buf_ref[ts:ts + _HALO, :]


def _short_conv(proj, conv_w, seq_len, ts=512, tc=512):
    T = proj.shape[0]
    ts = min(ts, seq_len)
    per = tc // HEAD_DIM

    def col(base):
        return pl.BlockSpec((ts, tc), lambda c, t: (t, base // per + c))

    return pl.pallas_call(
        functools.partial(_short_conv_kernel, steps_per_seq=seq_len // ts, ts=ts),
        out_shape=jax.ShapeDtypeStruct((T, SC_WIDTH), BF16),
        grid=(SC_WIDTH // tc, T // ts),
        in_specs=[col(_SB), col(_SC), col(_SH),
                  pl.BlockSpec((SC_KERNEL, tc), lambda c, t: (0, c))],
        out_specs=pl.BlockSpec((ts, tc), lambda c, t: (t, c)),
        scratch_shapes=[pltpu.VMEM((_HALO + ts, tc), F32)],
        compiler_params=_params(("parallel", "arbitrary")),
        name="short_conv",
    )(proj, proj, proj, conv_w)


def _bmm(a, b, dims):
    return lax.dot_general(a.astype(BF16), b.astype(BF16), dims, preferred_element_type=F32)


_B_NN = (((2,), (1,)), ((0,), (0,)))
_B_NT = (((2,), (2,)), ((0,), (0,)))


GDN_PAIR = 4


def _gdn_kernel(q_ref, k_ref, v_ref, z_ref, wq_ref, wk_ref, wv_ref, gc_ref, gr_ref, gain_ref,
                o_ref, qbuf, kbuf, vbuf, state_ref, obuf, *, rows):
    hp = pl.program_id(1)
    t = pl.program_id(2)
    n = rows // CHUNK
    D = HEAD_DIM

    @pl.when(t == 0)
    def _():
        zeros = jnp.zeros((_HALO, GDN_PAIR * D), F32)
        qbuf[0:_HALO, :] = zeros
        kbuf[0:_HALO, :] = zeros
        vbuf[0:_HALO, :] = zeros
        state_ref[...] = jnp.zeros_like(state_ref)

    def conv_silu(x_ref, w_ref, buf):
        buf[_HALO:_HALO + rows, :] = x_ref[...].astype(F32)
        w = w_ref[...]
        y = w[0:1, :] * buf[pl.ds(_HALO - 3, rows), :]
        y += w[1:2, :] * buf[pl.ds(_HALO - 2, rows), :]
        y += w[2:3, :] * buf[pl.ds(_HALO - 1, rows), :]
        y += w[3:4, :] * buf[pl.ds(_HALO, rows), :]
        buf[0:_HALO, :] = buf[rows:rows + _HALO, :]
        return y * jax.nn.sigmoid(y)

    q_all = conv_silu(q_ref, wq_ref, qbuf)
    k_all = conv_silu(k_ref, wk_ref, kbuf)
    v_all = conv_silu(v_ref, wv_ref, vbuf)

    gates = gc_ref[...]
    lane = lax.broadcasted_iota(jnp.int32, gates.shape, 1)
    ri = lax.broadcasted_iota(jnp.int32, (CHUNK, CHUNK), 0)
    ci = lax.broadcasted_iota(jnp.int32, (CHUNK, CHUNK), 1)
    tri_incl = (ci <= ri)[None]
    tri_strict = (ci < ri)[None]
    same16 = ((ri // 16) == (ci // 16))[None]
    same32 = ((ri // 32) == (ci // 32))[None]
    eye = (ri == ci).astype(F32)[None]

    def chunk_terms(gg):
        h = hp * GDN_PAIR + gg
        cols = slice(gg * D, (gg + 1) * D)
        q, k, v = q_all[:, cols], k_all[:, cols], v_all[:, cols]
        q = q * (lax.rsqrt(jnp.sum(q * q, axis=-1, keepdims=True) + RMS_EPS) * (D ** -0.5))
        k = k * lax.rsqrt(jnp.sum(k * k, axis=-1, keepdims=True) + RMS_EPS)
        g_cum = jnp.sum(jnp.where(lane == FOX_HEADS + h, gates, 0.0), axis=-1, keepdims=True)
        beta = jnp.sum(jnp.where(lane == FOX_HEADS + GDN_HEADS + h, gates, 0.0), axis=-1,
                       keepdims=True)
        q3 = q.reshape(n, CHUNK, D)
        k3 = k.reshape(n, CHUNK, D)
        v3 = v.reshape(n, CHUNK, D)
        g3 = g_cum.reshape(n, CHUNK, 1)
        b3 = beta.reshape(n, CHUNK, 1)
        g_row = gr_ref[:, pl.ds(FOX_HEADS + h, 1), :]
        g_last = g3[:, CHUNK - 1:CHUNK, :]

        decay = jnp.exp(jnp.where(tri_incl, g3 - g_row, -jnp.inf))
        kk = _bmm(k3, k3, _B_NT)
        L = jnp.where(tri_strict, b3 * kk * decay, 0.0)

        P = jnp.where(same16, L, 0.0)
        X = eye - P
        P2 = _bmm(P, P, _B_NN)
        X = _bmm(X, eye + P2, _B_NN)
        P4 = _bmm(P2, P2, _B_NN)
        X = _bmm(X, eye + P4, _B_NN)
        P8 = _bmm(P4, P4, _B_NN)
        X = _bmm(X, eye + P8, _B_NN)
        O32 = jnp.where(same32 & jnp.logical_not(same16), L, 0.0)
        X = X - _bmm(_bmm(X, O32, _B_NN), X, _B_NN)
        O64 = jnp.where(same32, 0.0, L)
        X = X - _bmm(_bmm(X, O64, _B_NN), X, _B_NN)

        e3 = jnp.exp(g3)
        rhs = jnp.concatenate([v3 * b3, k3 * (b3 * e3)], axis=-1)
        sol = _bmm(X, rhs, _B_NN)
        attn = jnp.where(tri_incl, _bmm(q3, k3, _B_NT) * decay, 0.0)
        return dict(u=sol[:, :, :D], w=sol[:, :, D:].astype(BF16), q=(q3 * e3).astype(BF16),
                    a=attn.astype(BF16), k=(k3 * jnp.exp(g_last - g3)).astype(BF16),
                    gl=jnp.exp(g_last))

    def pair_terms(p):
        t0, t1 = chunk_terms(2 * p), chunk_terms(2 * p + 1)
        return dict(
            u=jnp.concatenate([t0["u"], t1["u"]], axis=-1),
            wq=jnp.concatenate([jnp.concatenate([t0["w"], t1["w"]], axis=-1),
                                jnp.concatenate([t0["q"], t1["q"]], axis=-1)], axis=1),
            a=jnp.concatenate([t0["a"], t1["a"]], axis=-1),
            k=jnp.concatenate([t0["k"], t1["k"]], axis=-1),
            gl=jnp.concatenate([jnp.broadcast_to(t0["gl"], (n, 1, D)),
                                jnp.broadcast_to(t1["gl"], (n, 1, D))], axis=-1))

    def block_diag(x):
        first = lax.broadcasted_iota(jnp.int32, x.shape, 1) < D
        zero = jnp.zeros_like(x)
        return jnp.concatenate([jnp.where(first, x, zero), jnp.where(first, zero, x)], axis=0)

    n_pairs = GDN_PAIR // 2
    terms = [pair_terms(p) for p in range(n_pairs)]
    S = [state_ref[:, p * 2 * D:(p + 1) * 2 * D] for p in range(n_pairs)]
    for c in range(n):
        for p in range(n_pairs):
            tm = terms[p]
            r1 = jnp.dot(tm["wq"][c], block_diag(S[p].astype(BF16)), preferred_element_type=F32)
            v_b = (tm["u"][c] - r1[0:CHUNK]).astype(BF16)
            o_c = r1[CHUNK:2 * CHUNK] + jnp.dot(tm["a"][c], block_diag(v_b),
                                                preferred_element_type=F32)
            kv = lax.dot_general(tm["k"][c], v_b, _TN, preferred_element_type=F32)
            S[p] = S[p] * tm["gl"][c] + jnp.concatenate([kv[0:D, 0:D], kv[D:2 * D, D:2 * D]],
                                                         axis=1)
            obuf[c * CHUNK:(c + 1) * CHUNK, p * 2 * D:(p + 1) * 2 * D] = o_c
    for p in range(n_pairs):
        state_ref[:, p * 2 * D:(p + 1) * 2 * D] = S[p]

    for gg in range(GDN_PAIR):
        cols = slice(gg * D, (gg + 1) * D)
        o = obuf[:, cols]
        ms = jnp.mean(o * o, axis=-1, keepdims=True)
        z = z_ref[:, cols].astype(F32)
        o = o * lax.rsqrt(ms + RMS_EPS) * gain_ref[...] * (z * jax.nn.sigmoid(z))
        o_ref[:, cols] = o.astype(o_ref.dtype)


def _gdn(proj, proj_z, conv_w, gates_c, gates_r, out_gain, batch, seq_len, rows=512):
    T = proj.shape[0]
    rows = min(rows, seq_len)
    nt = seq_len // rows
    n = rows // CHUNK
    pairs = GDN_HEADS // GDN_PAIR
    W = GDN_PAIR * HEAD_DIM

    def col(base):
        return pl.BlockSpec((rows, W), lambda b, hp, t: (b * nt + t, base // GDN_PAIR + hp))

    def wcol(base):
        return pl.BlockSpec((GDN_CONV, W), lambda b, hp, t: (0, base // GDN_PAIR + hp))

    return pl.pallas_call(
        functools.partial(_gdn_kernel, rows=rows),
        out_shape=jax.ShapeDtypeStruct((T, GDN_WIDTH), BF16),
        grid=(batch, pairs, nt),
        in_specs=[col(_GQ), col(_GK), col(_GV), col(_GZ),
                  wcol(0), wcol(GDN_HEADS), wcol(2 * GDN_HEADS),
                  pl.BlockSpec((rows, GATE_LANES), lambda b, hp, t: (b * nt + t, 0)),
                  pl.BlockSpec((n, GATE_LANES, CHUNK), lambda b, hp, t: (b * nt + t, 0, 0)),
                  pl.BlockSpec((1, HEAD_DIM), lambda b, hp, t: (0, 0))],
        out_specs=pl.BlockSpec((rows, W), lambda b, hp, t: (b * nt + t, hp)),
        scratch_shapes=[pltpu.VMEM((_HALO + rows, W), F32),
                        pltpu.VMEM((_HALO + rows, W), F32),
                        pltpu.VMEM((_HALO + rows, W), F32),
                        pltpu.VMEM((HEAD_DIM, W), F32),
                        pltpu.VMEM((rows, W), F32)],
        compiler_params=_params(("parallel", "parallel", "arbitrary")),
        name="gdn",
    )(proj, proj, proj, proj_z, conv_w, conv_w, conv_w, gates_c, gates_r,
      out_gain.reshape(1, HEAD_DIM))


def _small_gate_weights(w_in_l):
    ff = w_in_l[:, _A_WIDTH:_B_COL]
    gab = w_in_l[:, _B_COL + _B_WIDTH:_C_COL]
    small = jnp.concatenate([ff, gab], axis=1)
    small = jnp.pad(small, ((0, 0), (0, GATE_LANES - small.shape[1])))
    return small.T.astype(BF16)


def _gate_columns(fox_forget_bias, gdn_a_log, gdn_dt_bias):
    pad = GATE_LANES - FOX_HEADS - GDN_HEADS
    bias = jnp.concatenate([fox_forget_bias, gdn_dt_bias, jnp.zeros((pad,), F32)])
    alog = jnp.concatenate([jnp.zeros((FOX_HEADS,), F32), gdn_a_log, jnp.zeros((pad,), F32)])
    return bias.reshape(GATE_LANES, 1), alog.reshape(GATE_LANES, 1)


def _layer(x, u, batch, seq_len, layer, w_in, fox_forget_bias, fox_out_norm, sc_conv_w, gdn_conv_w,
           gdn_a_log, gdn_dt_bias, gdn_out_norm, w_out, mix_post_norm, ffn_pre_norm,
           w_gate, w_up, w_down, ffn_post_norm, next_pre_norm):
    T, D = x.shape
    w_small_t = _small_gate_weights(w_in[layer])
    bias_col, alog_col = _gate_columns(fox_forget_bias, gdn_a_log, gdn_dt_bias)

    q_scale = jnp.concatenate([jnp.full((FOX_WIDTH,), FOX_Q_SCALE, F32),
                               jnp.ones((_A_WIDTH - FOX_WIDTH,), F32)]).reshape(1, _A_WIDTH)
    proj_a = _matmul_scaled(u, _cast_window(w_in, layer, _A_COL, _A_WIDTH), q_scale, BF16,
                            tm=1024, tn=768, name="in_proj_a")
    proj_b = _matmul(u, _cast_window(w_in, layer, _B_COL, _B_WIDTH), BF16, tm=1024, tn=768,
                     name="in_proj_b")
    proj_c = _matmul(u, _cast_window(w_in, layer, _C_COL, _C_WIDTH), BF16, tm=1024, tn=768,
                     name="in_proj_c")
    gates_t = _gates(u, w_small_t, bias_col, alog_col, seq_len)
    gates_c = gates_t.T
    gates_r = gates_t.reshape(GATE_LANES, T // CHUNK, CHUNK).transpose(1, 0, 2)

    fox_out = _fox_attention(proj_a, gates_c, fox_out_norm, batch, seq_len)
    sc_out = _short_conv(proj_b, sc_conv_w, seq_len)
    gdn_out = _gdn(proj_b, proj_c, gdn_conv_w, gates_c, gates_r, gdn_out_norm, batch, seq_len)

    w_out_b = _cast_weight(w_out, layer)
    y = _out_proj(fox_out, sc_out, gdn_out,
                  w_out_b[:FOX_WIDTH], w_out_b[FOX_WIDTH:FOX_WIDTH + SC_WIDTH],
                  w_out_b[FOX_WIDTH + SC_WIDTH:])
    h, v = _norm_residual(y, x, mix_post_norm, ffn_pre_norm)

    hidden = w_gate.shape[2]
    kd = hidden + (-hidden) % FFN_PAD
    wg = _cast_weight(w_gate, layer, cols_pad=kd)
    wu = _cast_weight(w_up, layer, cols_pad=kd)
    wd = _cast_weight(w_down, layer, rows_pad=kd)
    act = _ffn_up(v, wg, wu)
    y2 = _matmul(act, wd, F32, tm=1024, tn=1024, tk=kd // 4, name="ffn_down")
    return _norm_residual(y2, h, ffn_post_norm, next_pre_norm)


def kernel(x, mix_pre_norm, w_in, fox_forget_bias, fox_out_norm, sc_conv_w, gdn_conv_w, gdn_a_log,
           gdn_dt_bias, gdn_out_norm, w_out, mix_post_norm, ffn_pre_norm, w_gate, w_up, w_down,
           ffn_post_norm):
    B, S, D = x.shape
    depth = w_in.shape[0]
    h = x.reshape(B * S, D)
    u = _norm_cast(h, mix_pre_norm[0])
    for l in range(depth):
        nxt = mix_pre_norm[l + 1] if l + 1 < depth else None
        h, u = _layer(h, u, B, S, l, w_in, fox_forget_bias[l], fox_out_norm[l], sc_conv_w[l],
                      gdn_conv_w[l], gdn_a_log[l], gdn_dt_bias[l], gdn_out_norm[l], w_out,
                      mix_post_norm[l], ffn_pre_norm[l], w_gate, w_up, w_down,
                      ffn_post_norm[l], nxt)
    return h.reshape(B, S, D)
```

```python
import functools
import math

import jax
import jax.numpy as jnp
from jax import lax
from jax.experimental import pallas as pl
from jax.experimental.pallas import tpu as pltpu

F32 = jnp.float32
BF16 = jnp.bfloat16

HEAD_DIM = 128
FOX_HEADS = 12
FOX_WIDTH = FOX_HEADS * HEAD_DIM
SC_WIDTH = 8 * HEAD_DIM
SC_KERNEL = 3
GDN_HEADS = 12
GDN_WIDTH = GDN_HEADS * HEAD_DIM
GDN_CONV = 4
CHUNK = 64
RMS_EPS = 1e-6
GATE_LANES = 128
FFN_TN = 256

VMEM_LIMIT_BYTES = 56 * 1024 * 1024

_FQ, _FK, _FV = 0, 12, 24
_SB, _SC, _SH = 0, 8, 16
_GQ, _GK, _GV = 24, 36, 48
_GZ = 0
_A_COL, _A_WIDTH = 0, 3 * FOX_WIDTH
_B_COL, _B_WIDTH = 3 * FOX_WIDTH + FOX_HEADS, 3 * SC_WIDTH + 3 * GDN_WIDTH
_C_COL, _C_WIDTH = _B_COL + _B_WIDTH + 2 * GDN_HEADS, GDN_WIDTH

_NT = (((1,), (1,)), ((), ()))
_TN = (((0,), (0,)), ((), ()))
_HI = lax.Precision.HIGHEST


def _params(sem):
    return pltpu.CompilerParams(dimension_semantics=sem, vmem_limit_bytes=VMEM_LIMIT_BYTES)


def _norm_cast_kernel(x_ref, g_ref, o_ref):
    x = x_ref[...]
    ms = jnp.mean(x * x, axis=-1, keepdims=True)
    o_ref[...] = (x * lax.rsqrt(ms + RMS_EPS) * g_ref[...]).astype(o_ref.dtype)


def _norm_cast(x, gain, tm=256):
    T, D = x.shape
    tm = min(tm, T)
    return pl.pallas_call(
        _norm_cast_kernel,
        out_shape=jax.ShapeDtypeStruct((T, D), BF16),
        grid=(T // tm,),
        in_specs=[pl.BlockSpec((tm, D), lambda i: (i, 0)),
                  pl.BlockSpec((1, D), lambda i: (0, 0))],
        out_specs=pl.BlockSpec((tm, D), lambda i: (i, 0)),
        compiler_params=_params(("parallel",)),
        name="norm_cast",
    )(x, gain.reshape(1, D))


def _norm_residual_kernel(y_ref, x_ref, g_ref, gn_ref, h_ref, *maybe_u_ref):
    y = y_ref[...]
    ms = jnp.mean(y * y, axis=-1, keepdims=True)
    h = x_ref[...] + y * lax.rsqrt(ms + RMS_EPS) * g_ref[...]
    h_ref[...] = h
    if maybe_u_ref:
        ms2 = jnp.mean(h * h, axis=-1, keepdims=True)
        maybe_u_ref[0][...] = (h * lax.rsqrt(ms2 + RMS_EPS) * gn_ref[...]).astype(BF16)


def _norm_residual(y, x, gain, next_gain, tm=256):
    T, D = x.shape
    tm = min(tm, T)
    emit_next = next_gain is not None
    gn = (next_gain if emit_next else gain).reshape(1, D)
    row = pl.BlockSpec((tm, D), lambda i: (i, 0))
    vec = pl.BlockSpec((1, D), lambda i: (0, 0))
    out_shape = [jax.ShapeDtypeStruct((T, D), F32)]
    out_specs = [row]
    if emit_next:
        out_shape.append(jax.ShapeDtypeStruct((T, D), BF16))
        out_specs.append(row)
    res = pl.pallas_call(
        _norm_residual_kernel,
        out_shape=out_shape,
        grid=(T // tm,),
        in_specs=[row, row, vec, vec],
        out_specs=out_specs,
        compiler_params=_params(("parallel",)),
        name="norm_residual",
    )(y, x, gain.reshape(1, D), gn)
    return (res[0], res[1]) if emit_next else (res[0], None)


def _cast_weight_kernel(w_ref, o_ref):
    o_ref[...] = w_ref[0].astype(o_ref.dtype)


def _cast_weight(w_stack, layer, tr, tc):
    _, rows, cols = w_stack.shape
    assert rows % tr == 0 and cols % tc == 0
    return pl.pallas_call(
        _cast_weight_kernel,
        out_shape=jax.ShapeDtypeStruct((rows, cols), BF16),
        grid=(rows // tr, cols // tc),
        in_specs=[pl.BlockSpec((1, tr, tc), lambda i, j: (layer, i, j))],
        out_specs=pl.BlockSpec((tr, tc), lambda i, j: (i, j)),
        compiler_params=_params(("parallel", "parallel")),
        name="cast_weight",
    )(w_stack)


def _cast_window_kernel(a_ref, b_ref, o_ref, *, shift):
    if shift == 0:
        o_ref[...] = a_ref[0].astype(o_ref.dtype)
        return
    x = jnp.concatenate([a_ref[0], b_ref[0]], axis=1)
    width = x.shape[1]
    y = pltpu.roll(x, width - shift, axis=1)
    o_ref[...] = y[:, :o_ref.shape[1]].astype(o_ref.dtype)


def _cast_window(w_stack, layer, src_col, width, tr=512, tc=512):
    rows = w_stack.shape[1]
    base = (src_col // HEAD_DIM) * HEAD_DIM
    shift = src_col - base
    assert base % tc == 0 and width % tc == 0 and rows % tr == 0
    return pl.pallas_call(
        functools.partial(_cast_window_kernel, shift=shift),
        out_shape=jax.ShapeDtypeStruct((rows, width), BF16),
        grid=(rows // tr, width // tc),
        in_specs=[pl.BlockSpec((1, tr, tc), lambda i, j: (layer, i, base // tc + j)),
                  pl.BlockSpec((1, tr, HEAD_DIM),
                               lambda i, j: (layer, i, (base + (j + 1) * tc) // HEAD_DIM))],
        out_specs=pl.BlockSpec((tr, tc), lambda i, j: (i, j)),
        compiler_params=_params(("parallel", "parallel")),
        name="cast_window",
    )(w_stack, w_stack)


def _matmul_scaled_kernel(x_ref, w_ref, s_ref, o_ref):
    acc = jnp.dot(x_ref[...], w_ref[...], preferred_element_type=F32)
    o_ref[...] = (acc * s_ref[...]).astype(o_ref.dtype)


def _matmul_scaled(x, w, col_scale, out_dtype, tm, tn, name):
    M, K = x.shape
    _, N = w.shape
    tm, tn = min(tm, M), min(tn, N)
    return pl.pallas_call(
        _matmul_scaled_kernel,
        out_shape=jax.ShapeDtypeStruct((M, N), out_dtype),
        grid=(M // tm, N // tn),
        in_specs=[pl.BlockSpec((tm, K), lambda i, j: (i, 0)),
                  pl.BlockSpec((K, tn), lambda i, j: (0, j)),
                  pl.BlockSpec((1, tn), lambda i, j: (0, j))],
        out_specs=pl.BlockSpec((tm, tn), lambda i, j: (i, j)),
        compiler_params=_params(("parallel", "parallel")),
        name=name,
    )(x, w, col_scale)


def _matmul_kernel(x_ref, w_ref, o_ref, acc_ref, *, nk):
    part = jnp.dot(x_ref[...], w_ref[...], preferred_element_type=F32)
    if nk == 1:
        o_ref[...] = part.astype(o_ref.dtype)
        return
    k = pl.program_id(2)

    @pl.when(k == 0)
    def _():
        acc_ref[...] = part

    @pl.when(k > 0)
    def _():
        acc_ref[...] += part

    @pl.when(k == nk - 1)
    def _():
        o_ref[...] = acc_ref[...].astype(o_ref.dtype)


def _matmul(x, w, out_dtype, tm, tn, tk=None, name="matmul"):
    M, K = x.shape
    _, N = w.shape
    tm, tn = min(tm, M), min(tn, N)
    tk = K if tk is None else tk
    nk = K // tk
    acc_shape = (tm, tn) if nk > 1 else (8, 128)
    return pl.pallas_call(
        functools.partial(_matmul_kernel, nk=nk),
        out_shape=jax.ShapeDtypeStruct((M, N), out_dtype),
        grid=(M // tm, N // tn, nk),
        in_specs=[pl.BlockSpec((tm, tk), lambda i, j, k: (i, k)),
                  pl.BlockSpec((tk, tn), lambda i, j, k: (k, j))],
        out_specs=pl.BlockSpec((tm, tn), lambda i, j, k: (i, j)),
        scratch_shapes=[pltpu.VMEM(acc_shape, F32)],
        compiler_params=_params(("parallel", "parallel", "arbitrary")),
        name=name,
    )(x, w)


def _out_proj_kernel(a1_ref, a2_ref, a3_ref, w1_ref, w2_ref, w3_ref, o_ref):
    acc = jnp.dot(a1_ref[...], w1_ref[...], preferred_element_type=F32)
    acc += jnp.dot(a2_ref[...], w2_ref[...], preferred_element_type=F32)
    acc += jnp.dot(a3_ref[...], w3_ref[...], preferred_element_type=F32)
    o_ref[...] = acc


def _out_proj(a1, a2, a3, w1, w2, w3, tm=1024, tn=1024):
    T = a1.shape[0]
    N = w1.shape[1]
    tm, tn = min(tm, T), min(tn, N)

    def lhs(a):
        return pl.BlockSpec((tm, a.shape[1]), lambda i, j: (i, 0))

    def rhs(w):
        return pl.BlockSpec((w.shape[0], tn), lambda i, j: (0, j))

    return pl.pallas_call(
        _out_proj_kernel,
        out_shape=jax.ShapeDtypeStruct((T, N), F32),
        grid=(T // tm, N // tn),
        in_specs=[lhs(a1), lhs(a2), lhs(a3), rhs(w1), rhs(w2), rhs(w3)],
        out_specs=pl.BlockSpec((tm, tn), lambda i, j: (i, j)),
        compiler_params=_params(("parallel", "parallel")),
        name="out_proj",
    )(a1, a2, a3, w1, w2, w3)


def _ffn_up_kernel(x_ref, wg_ref, wu_ref, o_ref):
    x = x_ref[...]
    g = jnp.dot(x, wg_ref[...], preferred_element_type=F32)
    u = jnp.dot(x, wu_ref[...], preferred_element_type=F32)
    o_ref[...] = (g * jax.nn.sigmoid(g) * u).astype(o_ref.dtype)


def _ffn_up(x, wg, wu, tm=1024, tn=512):
    T, K = x.shape
    N = wg.shape[1]
    tm, tn = min(tm, T), min(tn, N)
    return pl.pallas_call(
        _ffn_up_kernel,
        out_shape=jax.ShapeDtypeStruct((T, N), BF16),
        grid=(T // tm, N // tn),
        in_specs=[pl.BlockSpec((tm, K), lambda i, j: (i, 0)),
                  pl.BlockSpec((K, tn), lambda i, j: (0, j)),
                  pl.BlockSpec((K, tn), lambda i, j: (0, j))],
        out_specs=pl.BlockSpec((tm, tn), lambda i, j: (i, j)),
        compiler_params=_params(("parallel", "parallel")),
        name="ffn_up",
    )(x, wg, wu)


def _gates_kernel(w_ref, u_ref, bias_ref, alog_ref, gt_ref, carry_ref, *, steps_per_seq, ts):
    t = pl.program_id(0)

    @pl.when(t % steps_per_seq == 0)
    def _():
        carry_ref[...] = jnp.zeros_like(carry_ref)

    z = lax.dot_general(w_ref[...], u_ref[...], _NT, preferred_element_type=F32) + bias_ref[...]
    row = lax.broadcasted_iota(jnp.int32, z.shape, 0)
    tail = jnp.log1p(jnp.exp(-jnp.abs(z)))
    log_sig = jnp.minimum(z, 0.0) - tail
    softplus = jnp.maximum(z, 0.0) + tail
    sig = 1.0 / (1.0 + jnp.exp(-z))
    decay = -jnp.exp(alog_ref[...]) * softplus
    val = jnp.where(row < FOX_HEADS, log_sig, jnp.where(row < FOX_HEADS + GDN_HEADS, decay, sig))

    src = lax.broadcasted_iota(jnp.int32, (ts, ts), 0)
    dst = lax.broadcasted_iota(jnp.int32, (ts, ts), 1)
    upper = src <= dst
    same_chunk = (src // CHUNK) == (dst // CHUNK)
    cum_all = jnp.dot(val, upper.astype(F32), precision=_HI, preferred_element_type=F32)
    cum_chunk = jnp.dot(val, (upper & same_chunk).astype(F32), precision=_HI,
                        preferred_element_type=F32)
    cum_all = cum_all + carry_ref[...]
    carry_ref[...] = cum_all[:, ts - 1:ts]
    gt_ref[...] = jnp.where(row < FOX_HEADS, cum_all,
                            jnp.where(row < FOX_HEADS + GDN_HEADS, cum_chunk, val))


def _gates(u, w_small_t, bias_col, alog_col, seq_len, ts=512):
    T, D = u.shape
    ts = min(ts, seq_len)
    return pl.pallas_call(
        functools.partial(_gates_kernel, steps_per_seq=seq_len // ts, ts=ts),
        out_shape=jax.ShapeDtypeStruct((GATE_LANES, T), F32),
        grid=(T // ts,),
        in_specs=[pl.BlockSpec((GATE_LANES, D), lambda t: (0, 0)),
                  pl.BlockSpec((ts, D), lambda t: (t, 0)),
                  pl.BlockSpec((GATE_LANES, 1), lambda t: (0, 0)),
                  pl.BlockSpec((GATE_LANES, 1), lambda t: (0, 0))],
        out_specs=pl.BlockSpec((GATE_LANES, ts), lambda t: (0, t)),
        scratch_shapes=[pltpu.VMEM((GATE_LANES, 1), F32)],
        compiler_params=_params(("arbitrary",)),
        name="gates",
    )(w_small_t, u, bias_col, alog_col)


FOX_Q_SCALE = (HEAD_DIM ** -0.5) * math.log2(math.e)
FOX_PAIR = 2
_FOX_BUILD_ROWS = 512


def _fox_kernel(q_ref, k_ref, v_ref, gc_ref, g_ref, o_ref, kaug, vaug, m_ref, acc_ref, sa_ref,
                sb_ref, *,
                tq, tk, seq_len, pairs):
    D = HEAD_DIM
    pair = pl.program_id(0) % pairs
    qi = pl.program_id(1)

    @pl.when(qi == 0)
    def _build():
        ri = lax.broadcasted_iota(jnp.int32, (3 * D, D), 0)
        ci = lax.broadcasted_iota(jnp.int32, (3 * D, D), 1)
        ones = jnp.ones((_FOX_BUILD_ROWS, D), BF16)

        def chunk(i, carry):
            r0 = pl.multiple_of(i * _FOX_BUILD_ROWS, _FOX_BUILD_ROWS)
            rows = pl.ds(r0, _FOX_BUILD_ROWS)
            g = gc_ref[rows, :] * (-math.log2(math.e))
            hi = g.astype(BF16)
            r1 = g - hi.astype(F32)
            mid = r1.astype(BF16)
            lo = (r1 - mid.astype(F32)).astype(BF16)
            pieces = jnp.concatenate([hi, mid, lo], axis=1)
            for gg in range(FOX_PAIR):
                h = pair * FOX_PAIR + gg
                sel = (((ri == h) & (ci == 0)) | ((ri == D + h) & (ci == 1))
                       | ((ri == 2 * D + h) & (ci == 2)))
                aug = jnp.dot(pieces, sel.astype(BF16), preferred_element_type=F32)
                kaug[gg, rows, 0:D] = k_ref[rows, gg * D:(gg + 1) * D]
                kaug[gg, rows, D:2 * D] = aug.astype(BF16)
                vaug[gg, rows, 0:D] = v_ref[rows, gg * D:(gg + 1) * D]
                vaug[gg, rows, D:2 * D] = ones
            return carry

        lax.fori_loop(0, seq_len // _FOX_BUILD_ROWS, chunk, 0)

    lane = lax.broadcasted_iota(jnp.int32, (tq, D), 1)
    ones3 = jnp.where(lane < 3, 1.0, 0.0).astype(BF16)
    q_aug = [jnp.concatenate([q_ref[:, gg * D:(gg + 1) * D], ones3], axis=1)
             for gg in range(FOX_PAIR)]
    m_ref[...] = jnp.full_like(m_ref, -jnp.inf)
    acc_ref[...] = jnp.zeros_like(acc_ref)

    def scores(kj, slot_ref):
        k0 = pl.multiple_of(kj * tk, tk)
        for gg in range(FOX_PAIR):
            slot_ref[gg] = lax.dot_general(q_aug[gg], kaug[gg, pl.ds(k0, tk), :], _NT,
                                           preferred_element_type=F32)

    def accumulate(kj, slot_ref, masked=False):
        k0 = pl.multiple_of(kj * tk, tk)
        for gg in range(FOX_PAIR):
            s = slot_ref[gg]
            if masked:
                r = qi * tq + lax.broadcasted_iota(jnp.int32, s.shape, 0)
                c = kj * tk + lax.broadcasted_iota(jnp.int32, s.shape, 1)
                s = jnp.where(c <= r, s, -jnp.inf)
            m_prev = m_ref[gg]
            m_new = jnp.maximum(m_prev, jnp.max(s, axis=-1, keepdims=True))
            alpha = jnp.exp2(m_prev - m_new)
            p = jnp.concatenate(
                [jnp.exp2(s[:, j * D:(j + 1) * D] - m_new) for j in range(tk // D)],
                axis=1).astype(BF16)
            pv = jnp.dot(p, vaug[gg, pl.ds(k0, tk), :], preferred_element_type=F32)
            acc_ref[gg] = jnp.concatenate([alpha, alpha], axis=1) * acc_ref[gg] + pv
            m_ref[gg] = m_new

    n_full = (qi * tq) // tk
    scores(0, sa_ref)

    def body(i, carry):
        scores(2 * i + 1, sb_ref)
        accumulate(2 * i, sa_ref)
        scores(2 * i + 2, sa_ref)
        accumulate(2 * i + 1, sb_ref)
        return carry

    lax.fori_loop(0, n_full // 2, body, 0)

    @pl.when(n_full % 2 == 0)
    def _():
        accumulate(n_full, sa_ref, masked=True)

    @pl.when(n_full % 2 == 1)
    def _():
        scores(n_full, sb_ref)
        accumulate(n_full - 1, sa_ref)
        accumulate(n_full, sb_ref, masked=True)

    for gg in range(FOX_PAIR):
        acc = acc_ref[gg]
        out = acc[:, 0:D] / acc[:, D:2 * D]
        ms = jnp.mean(out * out, axis=-1, keepdims=True)
        o_ref[:, gg * D:(gg + 1) * D] = (out * lax.rsqrt(ms + RMS_EPS) * g_ref[...]).astype(o_ref.dtype)


def _fox_attention(proj, gates_c, out_gain, batch, seq_len, tq=512, tk=1024):
    T = proj.shape[0]
    tq = min(tq, seq_len)
    tk = min(tk, seq_len)
    assert tk % tq == 0 and seq_len % tk == 0
    nq = seq_len // tq
    pairs = FOX_HEADS // FOX_PAIR
    W = FOX_PAIR * HEAD_DIM

    once = pl.Buffered(1)

    def rows_spec(base):
        return pl.BlockSpec((seq_len, W), lambda bp, qi: (bp // pairs, base // FOX_PAIR + bp % pairs),
                            pipeline_mode=once)

    return pl.pallas_call(
        functools.partial(_fox_kernel, tq=tq, tk=tk, seq_len=seq_len, pairs=pairs),
        out_shape=jax.ShapeDtypeStruct((T, FOX_WIDTH), BF16),
        grid=(batch * pairs, nq),
        in_specs=[pl.BlockSpec((tq, W), lambda bp, qi: ((bp // pairs) * nq + qi,
                                                         _FQ // FOX_PAIR + bp % pairs)),
                  rows_spec(_FK), rows_spec(_FV),
                  pl.BlockSpec((seq_len, GATE_LANES), lambda bp, qi: (bp // pairs, 0),
                               pipeline_mode=once),
                  pl.BlockSpec((1, HEAD_DIM), lambda bp, qi: (0, 0))],
        out_specs=pl.BlockSpec((tq, W), lambda bp, qi: ((bp // pairs) * nq + qi, bp % pairs)),
        scratch_shapes=[pltpu.VMEM((FOX_PAIR, seq_len, 2 * HEAD_DIM), BF16),
                        pltpu.VMEM((FOX_PAIR, seq_len, 2 * HEAD_DIM), BF16),
                        pltpu.VMEM((FOX_PAIR, tq, HEAD_DIM), F32),
                        pltpu.VMEM((FOX_PAIR, tq, 2 * HEAD_DIM), F32),
                        pltpu.VMEM((FOX_PAIR, tq, tk), F32),
                        pltpu.VMEM((FOX_PAIR, tq, tk), F32)],
        compiler_params=_params(("parallel", "arbitrary")),
        name="fox_attention",
    )(proj, proj, proj, gates_c, out_gain.reshape(1, HEAD_DIM))


_HALO = 8


def _short_conv_kernel(b_ref, c_ref, h_ref, w_ref, o_ref, buf_ref, *, steps_per_seq, ts):
    t = pl.program_id(1)

    @pl.when(t % steps_per_seq == 0)
    def _():
        buf_ref[0:_HALO, :] = jnp.zeros((_HALO, buf_ref.shape[1]), F32)

    buf_ref[_HALO:_HALO + ts, :] = c_ref[...].astype(F32) * h_ref[...].astype(F32)
    w = w_ref[...]
    y = w[0:1, :] * buf_ref[pl.ds(_HALO - 2, ts), :]
    y += w[1:2, :] * buf_ref[pl.ds(_HALO - 1, ts), :]
    y += w[2:3, :] * buf_ref[pl.ds(_HALO, ts), :]
    o_ref[...] = (b_ref[...].astype(F32) * y).astype(o_ref.dtype)
    buf_ref[0:_HALO, :] = buf_ref[ts:ts + _HALO, :]


def _short_conv(proj, conv_w, seq_len, ts=512, tc=512):
    T = proj.shape[0]
    ts = min(ts, seq_len)
    per = tc // HEAD_DIM

    def col(base):
        return pl.BlockSpec((ts, tc), lambda c, t: (t, base // per + c))

    return pl.pallas_call(
        functools.partial(_short_conv_kernel, steps_per_seq=seq_len // ts, ts=ts),
        out_shape=jax.ShapeDtypeStruct((T, SC_WIDTH), BF16),
        grid=(SC_WIDTH // tc, T // ts),
        in_specs=[col(_SB), col(_SC), col(_SH),
                  pl.BlockSpec((SC_KERNEL, tc), lambda c, t: (0, c))],
        out_specs=pl.BlockSpec((ts, tc), lambda c, t: (t, c)),
        scratch_shapes=[pltpu.VMEM((_HALO + ts, tc), F32)],
        compiler_params=_params(("parallel", "arbitrary")),
        name="short_conv",
    )(proj, proj, proj, conv_w)


def _bmm(a, b, dims):
    return lax.dot_general(a.astype(BF16), b.astype(BF16), dims, preferred_element_type=F32)


_B_NN = (((2,), (1,)), ((0,), (0,)))
_B_NT = (((2,), (2,)), ((0,), (0,)))


GDN_PAIR = 4


def _gdn_kernel(q_ref, k_ref, v_ref, z_ref, wq_ref, wk_ref, wv_ref, gc_ref, gr_ref, gain_ref,
                o_ref, qbuf, kbuf, vbuf, state_ref, obuf, *, rows):
    hp = pl.program_id(1)
    t = pl.program_id(2)
    n = rows // CHUNK
    D = HEAD_DIM

    @pl.when(t == 0)
    def _():
        zeros = jnp.zeros((_HALO, GDN_PAIR * D), F32)
        qbuf[0:_HALO, :] = zeros
        kbuf[0:_HALO, :] = zeros
        vbuf[0:_HALO, :] = zeros
        state_ref[...] = jnp.zeros_like(state_ref)

    def conv_silu(x_ref, w_ref, buf):
        buf[_HALO:_HALO + rows, :] = x_ref[...].astype(F32)
        w = w_ref[...]
        y = w[0:1, :] * buf[pl.ds(_HALO - 3, rows), :]
        y += w[1:2, :] * buf[pl.ds(_HALO - 2, rows), :]
        y += w[2:3, :] * buf[pl.ds(_HALO - 1, rows), :]
        y += w[3:4, :] * buf[pl.ds(_HALO, rows), :]
        buf[0:_HALO, :] = buf[rows:rows + _HALO, :]
        return y * jax.nn.sigmoid(y)

    q_all = conv_silu(q_ref, wq_ref, qbuf)
    k_all = conv_silu(k_ref, wk_ref, kbuf)
    v_all = conv_silu(v_ref, wv_ref, vbuf)

    gates = gc_ref[...]
    lane = lax.broadcasted_iota(jnp.int32, gates.shape, 1)
    ri = lax.broadcasted_iota(jnp.int32, (CHUNK, CHUNK), 0)
    ci = lax.broadcasted_iota(jnp.int32, (CHUNK, CHUNK), 1)
    tri_incl = (ci <= ri)[None]
    tri_strict = (ci < ri)[None]
    same16 = ((ri // 16) == (ci // 16))[None]
    same32 = ((ri // 32) == (ci // 32))[None]
    eye = (ri == ci).astype(F32)[None]

    def chunk_terms(gg):
        h = hp * GDN_PAIR + gg
        cols = slice(gg * D, (gg + 1) * D)
        q, k, v = q_all[:, cols], k_all[:, cols], v_all[:, cols]
        q = q * (lax.rsqrt(jnp.sum(q * q, axis=-1, keepdims=True) + RMS_EPS) * (D ** -0.5))
        k = k * lax.rsqrt(jnp.sum(k * k, axis=-1, keepdims=True) + RMS_EPS)
        g_cum = jnp.sum(jnp.where(lane == FOX_HEADS + h, gates, 0.0), axis=-1, keepdims=True)
        beta = jnp.sum(jnp.where(lane == FOX_HEADS + GDN_HEADS + h, gates, 0.0), axis=-1,
                       keepdims=True)
        q3 = q.reshape(n, CHUNK, D)
        k3 = k.reshape(n, CHUNK, D)
        v3 = v.reshape(n, CHUNK, D)
        g3 = g_cum.reshape(n, CHUNK, 1)
        b3 = beta.reshape(n, CHUNK, 1)
        g_row = gr_ref[:, pl.ds(FOX_HEADS + h, 1), :]
        g_last = g3[:, CHUNK - 1:CHUNK, :]

        decay = jnp.exp(jnp.where(tri_incl, g3 - g_row, -jnp.inf))
        kk = _bmm(k3, k3, _B_NT)
        L = jnp.where(tri_strict, b3 * kk * decay, 0.0)

        P = jnp.where(same16, L, 0.0)
        X = eye - P
        P2 = _bmm(P, P, _B_NN)
        X = _bmm(X, eye + P2, _B_NN)
        P4 = _bmm(P2, P2, _B_NN)
        X = _bmm(X, eye + P4, _B_NN)
        P8 = _bmm(P4, P4, _B_NN)
        X = _bmm(X, eye + P8, _B_NN)
        O32 = jnp.where(same32 & jnp.logical_not(same16), L, 0.0)
        X = X - _bmm(_bmm(X, O32, _B_NN), X, _B_NN)
        O64 = jnp.where(same32, 0.0, L)
        X = X - _bmm(_bmm(X, O64, _B_NN), X, _B_NN)

        e3 = jnp.exp(g3)
        rhs = jnp.concatenate([v3 * b3, k3 * (b3 * e3)], axis=-1)
        sol = _bmm(X, rhs, _B_NN)
        attn = jnp.where(tri_incl, _bmm(q3, k3, _B_NT) * decay, 0.0)
        return dict(u=sol[:, :, :D], w=sol[:, :, D:].astype(BF16), q=(q3 * e3).astype(BF16),
                    a=attn.astype(BF16), k=(k3 * jnp.exp(g_last - g3)).astype(BF16),
                    gl=jnp.exp(g_last))

    def pair_terms(p):
        t0, t1 = chunk_terms(2 * p), chunk_terms(2 * p + 1)
        return dict(
            u=jnp.concatenate([t0["u"], t1["u"]], axis=-1),
            wq=jnp.concatenate([jnp.concatenate([t0["w"], t1["w"]], axis=-1),
                                jnp.concatenate([t0["q"], t1["q"]], axis=-1)], axis=1),
            a=jnp.concatenate([t0["a"], t1["a"]], axis=-1),
            k=jnp.concatenate([t0["k"], t1["k"]], axis=-1),
            gl=jnp.concatenate([jnp.broadcast_to(t0["gl"], (n, 1, D)),
                                jnp.broadcast_to(t1["gl"], (n, 1, D))], axis=-1))

    def block_diag(x):
        first = lax.broadcasted_iota(jnp.int32, x.shape, 1) < D
        zero = jnp.zeros_like(x)
        return jnp.concatenate([jnp.where(first, x, zero), jnp.where(first, zero, x)], axis=0)

    n_pairs = GDN_PAIR // 2
    terms = [pair_terms(p) for p in range(n_pairs)]
    S = [state_ref[:, p * 2 * D:(p + 1) * 2 * D] for p in range(n_pairs)]
    for c in range(n):
        for p in range(n_pairs):
            tm = terms[p]
            r1 = jnp.dot(tm["wq"][c], block_diag(S[p].astype(BF16)), preferred_element_type=F32)
            v_b = (tm["u"][c] - r1[0:CHUNK]).astype(BF16)
            o_c = r1[CHUNK:2 * CHUNK] + jnp.dot(tm["a"][c], block_diag(v_b),
                                                preferred_element_type=F32)
            kv = lax.dot_general(tm["k"][c], v_b, _TN, preferred_element_type=F32)
            S[p] = S[p] * tm["gl"][c] + jnp.concatenate([kv[0:D, 0:D], kv[D:2 * D, D:2 * D]],
                                                         axis=1)
            obuf[c * CHUNK:(c + 1) * CHUNK, p * 2 * D:(p + 1) * 2 * D] = o_c
    for p in range(n_pairs):
        state_ref[:, p * 2 * D:(p + 1) * 2 * D] = S[p]

    for gg in range(GDN_PAIR):
        cols = slice(gg * D, (gg + 1) * D)
        o = obuf[:, cols]
        ms = jnp.mean(o * o, axis=-1, keepdims=True)
        z = z_ref[:, cols].astype(F32)
        o = o * lax.rsqrt(ms + RMS_EPS) * gain_ref[...] * (z * jax.nn.sigmoid(z))
        o_ref[:, cols] = o.astype(o_ref.dtype)


def _gdn(proj, proj_z, conv_w, gates_c, gates_r, out_gain, batch, seq_len, rows=512):
    T = proj.shape[0]
    rows = min(rows, seq_len)
    nt = seq_len // rows
    n = rows // CHUNK
    pairs = GDN_HEADS // GDN_PAIR
    W = GDN_PAIR * HEAD_DIM

    def col(base):
        return pl.BlockSpec((rows, W), lambda b, hp, t: (b * nt + t, base // GDN_PAIR + hp))

    def wcol(base):
        return pl.BlockSpec((GDN_CONV, W), lambda b, hp, t: (0, base // GDN_PAIR + hp))

    return pl.pallas_call(
        functools.partial(_gdn_kernel, rows=rows),
        out_shape=jax.ShapeDtypeStruct((T, GDN_WIDTH), BF16),
        grid=(batch, pairs, nt),
        in_specs=[col(_GQ), col(_GK), col(_GV), col(_GZ),
                  wcol(0), wcol(GDN_HEADS), wcol(2 * GDN_HEADS),
                  pl.BlockSpec((rows, GATE_LANES), lambda b, hp, t: (b * nt + t, 0)),
                  pl.BlockSpec((n, GATE_LANES, CHUNK), lambda b, hp, t: (b * nt + t, 0, 0)),
                  pl.BlockSpec((1, HEAD_DIM), lambda b, hp, t: (0, 0))],
        out_specs=pl.BlockSpec((rows, W), lambda b, hp, t: (b * nt + t, hp)),
        scratch_shapes=[pltpu.VMEM((_HALO + rows, W), F32),
                        pltpu.VMEM((_HALO + rows, W), F32),
                        pltpu.VMEM((_HALO + rows, W), F32),
                        pltpu.VMEM((HEAD_DIM, W), F32),
                        pltpu.VMEM((rows, W), F32)],
        compiler_params=_params(("parallel", "parallel", "arbitrary")),
        name="gdn",
    )(proj, proj, proj, proj_z, conv_w, conv_w, conv_w, gates_c, gates_r,
      out_gain.reshape(1, HEAD_DIM))


def _gate_weight_kernel(f_ref, ab_ref, o_ref):
    lane = lax.broadcasted_iota(jnp.int32, o_ref.shape, 1)
    w = jnp.where(lane < FOX_HEADS, f_ref[0],
                  jnp.where(lane < FOX_HEADS + 2 * GDN_HEADS, ab_ref[0], 0.0))
    o_ref[...] = w.astype(o_ref.dtype)


def _small_gate_weights(w_in, layer, tr=512):
    rows = w_in.shape[1]
    f_col, ab_col = _A_WIDTH, _B_COL + _B_WIDTH
    assert f_col % GATE_LANES == 0 and ab_col % GATE_LANES == FOX_HEADS
    return pl.pallas_call(
        _gate_weight_kernel,
        out_shape=jax.ShapeDtypeStruct((rows, GATE_LANES), BF16),
        grid=(rows // tr,),
        in_specs=[pl.BlockSpec((1, tr, GATE_LANES), lambda i: (layer, i, f_col // GATE_LANES)),
                  pl.BlockSpec((1, tr, GATE_LANES), lambda i: (layer, i, ab_col // GATE_LANES))],
        out_specs=pl.BlockSpec((tr, GATE_LANES), lambda i: (i, 0)),
        compiler_params=_params(("parallel",)),
        name="gate_weights",
    )(w_in, w_in)


def _gate_columns(fox_forget_bias, gdn_a_log, gdn_dt_bias):
    pad = GATE_LANES - FOX_HEADS - GDN_HEADS
    bias = jnp.concatenate([fox_forget_bias, gdn_dt_bias, jnp.zeros((pad,), F32)])
    alog = jnp.concatenate([jnp.zeros((FOX_HEADS,), F32), gdn_a_log, jnp.zeros((pad,), F32)])
    return bias.reshape(GATE_LANES, 1), alog.reshape(GATE_LANES, 1)


def _layer(x, u, batch, seq_len, layer, w_in, fox_forget_bias, fox_out_norm, sc_conv_w, gdn_conv_w,
           gdn_a_log, gdn_dt_bias, gdn_out_norm, w_out, mix_post_norm, ffn_pre_norm,
           w_gate, w_up, w_down, ffn_post_norm, next_pre_norm):
    T, D = x.shape
    w_small_t = _small_gate_weights(w_in, layer).T
    bias_col, alog_col = _gate_columns(fox_forget_bias, gdn_a_log, gdn_dt_bias)

    q_scale = jnp.concatenate([jnp.full((FOX_WIDTH,), FOX_Q_SCALE, F32),
                               jnp.ones((_A_WIDTH - FOX_WIDTH,), F32)]).reshape(1, _A_WIDTH)
    proj_a = _matmul_scaled(u, _cast_window(w_in, layer, _A_COL, _A_WIDTH), q_scale, BF16,
                            tm=1024, tn=768, name="in_proj_a")
    proj_b = _matmul(u, _cast_window(w_in, layer, _B_COL, _B_WIDTH), BF16, tm=1024, tn=768,
                     name="in_proj_b")
    proj_c = _matmul(u, _cast_window(w_in, layer, _C_COL, _C_WIDTH), BF16, tm=1024, tn=768,
                     name="in_proj_c")
    gates_t = _gates(u, w_small_t, bias_col, alog_col, seq_len)
    gates_c = gates_t.T
    gates_r = gates_t.reshape(GATE_LANES, T // CHUNK, CHUNK).transpose(1, 0, 2)

    fox_out = _fox_attention(proj_a, gates_c, fox_out_norm, batch, seq_len)
    sc_out = _short_conv(proj_b, sc_conv_w, seq_len)
    gdn_out = _gdn(proj_b, proj_c, gdn_conv_w, gates_c, gates_r, gdn_out_norm, batch, seq_len)

    w_out_b = _cast_weight(w_out, layer, tr=512, tc=1024)
    y = _out_proj(fox_out, sc_out, gdn_out,
                  w_out_b[:FOX_WIDTH], w_out_b[FOX_WIDTH:FOX_WIDTH + SC_WIDTH],
                  w_out_b[FOX_WIDTH + SC_WIDTH:])
    h, v = _norm_residual(y, x, mix_post_norm, ffn_pre_norm)

    hidden = w_gate.shape[2]
    wg = _cast_weight(w_gate, layer, tr=256, tc=hidden // 2)
    wu = _cast_weight(w_up, layer, tr=256, tc=hidden // 2)
    wd = _cast_weight(w_down, layer, tr=FFN_TN, tc=D)
    act = _ffn_up(v, wg, wu, tm=1024, tn=FFN_TN)
    y2 = _matmul(act, wd, F32, tm=512, tn=512, name="ffn_down")
    return _norm_residual(y2, h, ffn_post_norm, next_pre_norm)


def kernel(x, mix_pre_norm, w_in, fox_forget_bias, fox_out_norm, sc_conv_w, gdn_conv_w, gdn_a_log,
           gdn_dt_bias, gdn_out_norm, w_out, mix_post_norm, ffn_pre_norm, w_gate, w_up, w_down,
           ffn_post_norm):
    B, S, D = x.shape
    depth = w_in.shape[0]
    h = x.reshape(B * S, D)
    u = _norm_cast(h, mix_pre_norm[0])
    for l in range(depth):
        nxt = mix_pre_norm[l + 1] if l + 1 < depth else None
        h, u = _layer(h, u, B, S, l, w_in, fox_forget_bias[l], fox_out_norm[l], sc_conv_w[l],
                      gdn_conv_w[l], gdn_a_log[l], gdn_dt_bias[l], gdn_out_norm[l], w_out,
                      mix_post_norm[l], ffn_pre_norm[l], w_gate, w_up, w_down,
                      ffn_post_norm[l], nxt)
    return h.reshape(B, S, D)
```

```python
import functools
import math
from typing import NamedTuple

import jax
import jax.numpy as jnp
from jax import lax
from jax.experimental import pallas as pl
from jax.experimental.pallas import tpu as pltpu

F32 = jnp.float32
BF16 = jnp.bfloat16

HEAD_DIM = 128
FOX_HEADS = 12
FOX_WIDTH = FOX_HEADS * HEAD_DIM
SC_WIDTH = 8 * HEAD_DIM
SC_KERNEL = 3
GDN_HEADS = 12
GDN_WIDTH = GDN_HEADS * HEAD_DIM
GDN_CONV = 4
CHUNK = 64
RMS_EPS = 1e-6
GATE_LANES = 128

VMEM_LIMIT_BYTES = 56 * 1024 * 1024

_FQ, _FK, _FV = 0, 12, 24
_SB, _SC, _SH = 0, 8, 16
_GQ, _GK, _GV = 24, 36, 48
_GZ = 0
_A_COL, _A_WIDTH = 0, 3 * FOX_WIDTH
_B_COL, _B_WIDTH = 3 * FOX_WIDTH + FOX_HEADS, 3 * SC_WIDTH + 3 * GDN_WIDTH
_C_COL, _C_WIDTH = _B_COL + _B_WIDTH + 2 * GDN_HEADS, GDN_WIDTH

_NT = (((1,), (1,)), ((), ()))
_TN = (((0,), (0,)), ((), ()))
_HI = lax.Precision.HIGHEST


def _params(sem):
    return pltpu.CompilerParams(dimension_semantics=sem, vmem_limit_bytes=VMEM_LIMIT_BYTES)


def _norm_cast_kernel(x_ref, g_ref, o_ref):
    x = x_ref[...]
    ms = jnp.mean(x * x, axis=-1, keepdims=True)
    o_ref[...] = (x * lax.rsqrt(ms + RMS_EPS) * g_ref[...]).astype(o_ref.dtype)


def _norm_cast(x, gain, tm=256):
    T, D = x.shape
    tm = min(tm, T)
    return pl.pallas_call(
        _norm_cast_kernel,
        out_shape=jax.ShapeDtypeStruct((T, D), BF16),
        grid=(T // tm,),
        in_specs=[pl.BlockSpec((tm, D), lambda i: (i, 0)),
                  pl.BlockSpec((1, D), lambda i: (0, 0))],
        out_specs=pl.BlockSpec((tm, D), lambda i: (i, 0)),
        compiler_params=_params(("parallel",)),
        name="norm_cast",
    )(x, gain.reshape(1, D))


def _norm_residual_kernel(y_ref, x_ref, g_ref, gn_ref, h_ref, *maybe_u_ref):
    y = y_ref[...]
    ms = jnp.mean(y * y, axis=-1, keepdims=True)
    h = x_ref[...] + y * lax.rsqrt(ms + RMS_EPS) * g_ref[...]
    h_ref[...] = h
    if maybe_u_ref:
        ms2 = jnp.mean(h * h, axis=-1, keepdims=True)
        maybe_u_ref[0][...] = (h * lax.rsqrt(ms2 + RMS_EPS) * gn_ref[...]).astype(BF16)


def _norm_residual(y, x, gain, next_gain, tm=256):
    T, D = x.shape
    tm = min(tm, T)
    emit_next = next_gain is not None
    gn = (next_gain if emit_next else gain).reshape(1, D)
    row = pl.BlockSpec((tm, D), lambda i: (i, 0))
    vec = pl.BlockSpec((1, D), lambda i: (0, 0))
    out_shape = [jax.ShapeDtypeStruct((T, D), F32)]
    out_specs = [row]
    if emit_next:
        out_shape.append(jax.ShapeDtypeStruct((T, D), BF16))
        out_specs.append(row)
    res = pl.pallas_call(
        _norm_residual_kernel,
        out_shape=out_shape,
        grid=(T // tm,),
        in_specs=[row, row, vec, vec],
        out_specs=out_specs,
        compiler_params=_params(("parallel",)),
        name="norm_residual",
    )(y, x, gain.reshape(1, D), gn)
    return (res[0], res[1]) if emit_next else (res[0], None)


def _cast_window_kernel(a_ref, b_ref, o_ref, *, shift):
    if shift == 0:
        o_ref[...] = a_ref[0].astype(o_ref.dtype)
        return
    x = jnp.concatenate([a_ref[0], b_ref[0]], axis=1)
    width = x.shape[1]
    y = pltpu.roll(x, width - shift, axis=1)
    o_ref[...] = y[:, :o_ref.shape[1]].astype(o_ref.dtype)


def _cast_window(w_stack, layer, src_col, width, tr=512, tc=512):
    rows = w_stack.shape[1]
    base = (src_col // HEAD_DIM) * HEAD_DIM
    shift = src_col - base
    assert base % tc == 0 and width % tc == 0 and rows % tr == 0
    return pl.pallas_call(
        functools.partial(_cast_window_kernel, shift=shift),
        out_shape=jax.ShapeDtypeStruct((rows, width), BF16),
        grid=(rows // tr, width // tc),
        in_specs=[pl.BlockSpec((1, tr, tc), lambda i, j: (layer, i, base // tc + j)),
                  pl.BlockSpec((1, tr, HEAD_DIM),
                               lambda i, j: (layer, i, (base + (j + 1) * tc) // HEAD_DIM))],
        out_specs=pl.BlockSpec((tr, tc), lambda i, j: (i, j)),
        compiler_params=_params(("parallel", "parallel")),
        name="cast_window",
    )(w_stack, w_stack)


class _CarriedCast(NamedTuple):
    w_stack: jax.Array
    layer: int


_BF16_SUBLANES = 16


def _carried_tile_rows(rows, steps):
    for t in range(_BF16_SUBLANES, rows, _BF16_SUBLANES):
        if rows % t == 0 and rows // t <= steps:
            return t
    return rows


def _call(kernel, carried, grid, in_specs, out_specs, out_shape, args, name):
    n_in, n_out = len(in_specs), len(out_specs)
    body = kernel
    if carried is not None:
        _, rows, cols = carried.w_stack.shape
        tile_rows = _carried_tile_rows(rows, math.prod(grid))
        n_tiles = rows // tile_rows

        def tile(*g):
            step = 0
            for size, idx in zip(grid, g):
                step = step * size + idx
            return jnp.minimum(step, n_tiles - 1)

        in_specs = in_specs + [pl.BlockSpec((1, tile_rows, cols),
                                            lambda *g: (carried.layer, tile(*g), 0))]
        out_specs = out_specs + [pl.BlockSpec((tile_rows, cols), lambda *g: (tile(*g), 0))]
        out_shape = out_shape + [jax.ShapeDtypeStruct((rows, cols), BF16)]
        args = args + [carried.w_stack]

        def body(*refs):
            src, dst = refs[n_in], refs[n_in + 1 + n_out]
            dst[...] = src[0].astype(dst.dtype)
            kernel(*refs[:n_in], *refs[n_in + 1:n_in + 1 + n_out], *refs[n_in + 2 + n_out:])

    return pl.pallas_call(
        body, out_shape=out_shape, grid=grid, in_specs=in_specs, out_specs=out_specs,
        compiler_params=_params(("arbitrary",) * len(grid)), name=name)(*args)


def _matmul_kernel(x_ref, w_ref, *rest, scaled):
    acc = jnp.dot(x_ref[...], w_ref[...], preferred_element_type=F32)
    if scaled:
        s_ref, o_ref = rest
        acc = acc * s_ref[...]
    else:
        o_ref, = rest
    o_ref[...] = acc.astype(o_ref.dtype)


def _matmul(x, w, out_dtype, tm, tn, name, col_scale=None, carried=None):
    M, K = x.shape
    _, N = w.shape
    tm, tn = min(tm, M), min(tn, N)
    scaled = col_scale is not None
    in_specs = [pl.BlockSpec((tm, K), lambda i, j: (i, 0)),
                pl.BlockSpec((K, tn), lambda i, j: (0, j))]
    args = [x, w]
    if scaled:
        in_specs.append(pl.BlockSpec((1, tn), lambda i, j: (0, j)))
        args.append(col_scale)
    return _call(functools.partial(_matmul_kernel, scaled=scaled), carried,
                 (M // tm, N // tn), in_specs, [pl.BlockSpec((tm, tn), lambda i, j: (i, j))],
                 [jax.ShapeDtypeStruct((M, N), out_dtype)], args, name)


def _out_proj_kernel(a1_ref, a2_ref, a3_ref, w1_ref, w2_ref, w3_ref, o_ref):
    acc = jnp.dot(a1_ref[...], w1_ref[...], preferred_element_type=F32)
    acc += jnp.dot(a2_ref[...], w2_ref[...], preferred_element_type=F32)
    acc += jnp.dot(a3_ref[...], w3_ref[...], preferred_element_type=F32)
    o_ref[...] = acc


def _out_proj(a1, a2, a3, w1, w2, w3, carried, tm=1024, tn=1024):
    T = a1.shape[0]
    N = w1.shape[1]
    tm, tn = min(tm, T), min(tn, N)

    def lhs(a):
        return pl.BlockSpec((tm, a.shape[1]), lambda i, j: (i, 0))

    def rhs(w):
        return pl.BlockSpec((w.shape[0], tn), lambda i, j: (0, j))

    return _call(_out_proj_kernel, carried, (T // tm, N // tn),
                 [lhs(a1), lhs(a2), lhs(a3), rhs(w1), rhs(w2), rhs(w3)],
                 [pl.BlockSpec((tm, tn), lambda i, j: (i, j))],
                 [jax.ShapeDtypeStruct((T, N), F32)], [a1, a2, a3, w1, w2, w3], "out_proj")


def _ffn_up_kernel(x_ref, wg_ref, wu_ref, o_ref):
    x = x_ref[...]
    g = jnp.dot(x, wg_ref[...], preferred_element_type=F32)
    u = jnp.dot(x, wu_ref[...], preferred_element_type=F32)
    o_ref[...] = (g * jax.nn.sigmoid(g) * u).astype(o_ref.dtype)


def _ffn_up(x, wg, wu, carried, tm=1024, tn=512):
    T, K = x.shape
    N = wg.shape[1]
    tm = min(tm, T)
    return _call(_ffn_up_kernel, carried, (T // tm, pl.cdiv(N, tn)),
                 [pl.BlockSpec((tm, K), lambda i, j: (i, 0)),
                  pl.BlockSpec((K, tn), lambda i, j: (0, j)),
                  pl.BlockSpec((K, tn), lambda i, j: (0, j))],
                 [pl.BlockSpec((tm, tn), lambda i, j: (i, j))],
                 [jax.ShapeDtypeStruct((T, N), BF16)], [x, wg, wu], "ffn_up")


def _gates_kernel(w_ref, u_ref, bias_ref, alog_ref, gt_ref, carry_ref, *, steps_per_seq, ts):
    t = pl.program_id(0)

    @pl.when(t % steps_per_seq == 0)
    def _():
        carry_ref[...] = jnp.zeros_like(carry_ref)

    z = lax.dot_general(w_ref[...], u_ref[...], _NT, preferred_element_type=F32) + bias_ref[...]
    row = lax.broadcasted_iota(jnp.int32, z.shape, 0)
    tail = jnp.log1p(jnp.exp(-jnp.abs(z)))
    log_sig = jnp.minimum(z, 0.0) - tail
    softplus = jnp.maximum(z, 0.0) + tail
    sig = 1.0 / (1.0 + jnp.exp(-z))
    decay = -jnp.exp(alog_ref[...]) * softplus
    val = jnp.where(row < FOX_HEADS, log_sig, jnp.where(row < FOX_HEADS + GDN_HEADS, decay, sig))

    src = lax.broadcasted_iota(jnp.int32, (ts, ts), 0)
    dst = lax.broadcasted_iota(jnp.int32, (ts, ts), 1)
    upper = src <= dst
    same_chunk = (src // CHUNK) == (dst // CHUNK)
    cum_all = jnp.dot(val, upper.astype(F32), precision=_HI, preferred_element_type=F32)
    cum_chunk = jnp.dot(val, (upper & same_chunk).astype(F32), precision=_HI,
                        preferred_element_type=F32)
    cum_all = cum_all + carry_ref[...]
    carry_ref[...] = cum_all[:, ts - 1:ts]
    gt_ref[...] = jnp.where(row < FOX_HEADS, cum_all,
                            jnp.where(row < FOX_HEADS + GDN_HEADS, cum_chunk, val))


def _gates(u, w_small_t, bias_col, alog_col, seq_len, ts=512):
    T, D = u.shape
    ts = min(ts, seq_len)
    return pl.pallas_call(
        functools.partial(_gates_kernel, steps_per_seq=seq_len // ts, ts=ts),
        out_shape=jax.ShapeDtypeStruct((GATE_LANES, T), F32),
        grid=(T // ts,),
        in_specs=[pl.BlockSpec((GATE_LANES, D), lambda t: (0, 0)),
                  pl.BlockSpec((ts, D), lambda t: (t, 0)),
                  pl.BlockSpec((GATE_LANES, 1), lambda t: (0, 0)),
                  pl.BlockSpec((GATE_LANES, 1), lambda t: (0, 0))],
        out_specs=pl.BlockSpec((GATE_LANES, ts), lambda t: (0, t)),
        scratch_shapes=[pltpu.VMEM((GATE_LANES, 1), F32)],
        compiler_params=_params(("arbitrary",)),
        name="gates",
    )(w_small_t, u, bias_col, alog_col)


FOX_Q_SCALE = (HEAD_DIM ** -0.5) * math.log2(math.e)
FOX_PAIR = 2
_FOX_BUILD_ROWS = 512


def _fox_kernel(q_ref, k_ref, v_ref, gc_ref, g_ref, o_ref, kaug, vaug, m_ref, acc_ref, sa_ref,
                sb_ref, *,
                tq, tk, seq_len, pairs):
    D = HEAD_DIM
    pair = pl.program_id(0) % pairs
    qi = pl.program_id(1)

    @pl.when(qi == 0)
    def _build():
        ri = lax.broadcasted_iota(jnp.int32, (3 * D, D), 0)
        ci = lax.broadcasted_iota(jnp.int32, (3 * D, D), 1)
        ones = jnp.ones((_FOX_BUILD_ROWS, D), BF16)

        def chunk(i, carry):
            r0 = pl.multiple_of(i * _FOX_BUILD_ROWS, _FOX_BUILD_ROWS)
            rows = pl.ds(r0, _FOX_BUILD_ROWS)
            g = gc_ref[rows, :] * (-math.log2(math.e))
            hi = g.astype(BF16)
            r1 = g - hi.astype(F32)
            mid = r1.astype(BF16)
            lo = (r1 - mid.astype(F32)).astype(BF16)
            pieces = jnp.concatenate([hi, mid, lo], axis=1)
            for gg in range(FOX_PAIR):
                h = pair * FOX_PAIR + gg
                sel = (((ri == h) & (ci == 0)) | ((ri == D + h) & (ci == 1))
                       | ((ri == 2 * D + h) & (ci == 2)))
                aug = jnp.dot(pieces, sel.astype(BF16), preferred_element_type=F32)
                kaug[gg, rows, 0:D] = k_ref[rows, gg * D:(gg + 1) * D]
                kaug[gg, rows, D:2 * D] = aug.astype(BF16)
                vaug[gg, rows, 0:D] = v_ref[rows, gg * D:(gg + 1) * D]
                vaug[gg, rows, D:2 * D] = ones
            return carry

        lax.fori_loop(0, seq_len // _FOX_BUILD_ROWS, chunk, 0)

    lane = lax.broadcasted_iota(jnp.int32, (tq, D), 1)
    ones3 = jnp.where(lane < 3, 1.0, 0.0).astype(BF16)
    q_aug = [jnp.concatenate([q_ref[:, gg * D:(gg + 1) * D], ones3], axis=1)
             for gg in range(FOX_PAIR)]
    m_ref[...] = jnp.full_like(m_ref, -jnp.inf)
    acc_ref[...] = jnp.zeros_like(acc_ref)

    def scores(kj, slot_ref):
        k0 = pl.multiple_of(kj * tk, tk)
        for gg in range(FOX_PAIR):
            slot_ref[gg] = lax.dot_general(q_aug[gg], kaug[gg, pl.ds(k0, tk), :], _NT,
                                           preferred_element_type=F32)

    def accumulate(kj, slot_ref, masked=False):
        k0 = pl.multiple_of(kj * tk, tk)
        for gg in range(FOX_PAIR):
            s = slot_ref[gg]
            if masked:
                r = qi * tq + lax.broadcasted_iota(jnp.int32, s.shape, 0)
                c = kj * tk + lax.broadcasted_iota(jnp.int32, s.shape, 1)
                s = jnp.where(c <= r, s, -jnp.inf)
            m_prev = m_ref[gg]
            m_new = jnp.maximum(m_prev, jnp.max(s, axis=-1, keepdims=True))
            alpha = jnp.exp2(m_prev - m_new)
            p = jnp.concatenate(
                [jnp.exp2(s[:, j * D:(j + 1) * D] - m_new) for j in range(tk // D)],
                axis=1).astype(BF16)
            pv = jnp.dot(p, vaug[gg, pl.ds(k0, tk), :], preferred_element_type=F32)
            acc_ref[gg] = jnp.concatenate([alpha, alpha], axis=1) * acc_ref[gg] + pv
            m_ref[gg] = m_new

    n_full = (qi * tq) // tk
    scores(0, sa_ref)

    def body(i, carry):
        scores(2 * i + 1, sb_ref)
        accumulate(2 * i, sa_ref)
        scores(2 * i + 2, sa_ref)
        accumulate(2 * i + 1, sb_ref)
        return carry

    lax.fori_loop(0, n_full // 2, body, 0)

    @pl.when(n_full % 2 == 0)
    def _():
        accumulate(n_full, sa_ref, masked=True)

    @pl.when(n_full % 2 == 1)
    def _():
        scores(n_full, sb_ref)
        accumulate(n_full - 1, sa_ref)
        accumulate(n_full, sb_ref, masked=True)

    for gg in range(FOX_PAIR):
        acc = acc_ref[gg]
        out = acc[:, 0:D] / acc[:, D:2 * D]
        ms = jnp.mean(out * out, axis=-1, keepdims=True)
        o_ref[:, gg * D:(gg + 1) * D] = (out * lax.rsqrt(ms + RMS_EPS) * g_ref[...]).astype(o_ref.dtype)


def _fox_attention(proj, gates_c, out_gain, batch, seq_len, tq=512, tk=1024):
    T = proj.shape[0]
    tq = min(tq, seq_len)
    tk = min(tk, seq_len)
    assert tk % tq == 0 and seq_len % tk == 0
    nq = seq_len // tq
    pairs = FOX_HEADS // FOX_PAIR
    W = FOX_PAIR * HEAD_DIM

    once = pl.Buffered(1)

    def rows_spec(base):
        return pl.BlockSpec((seq_len, W), lambda bp, qi: (bp // pairs, base // FOX_PAIR + bp % pairs),
                            pipeline_mode=once)

    return pl.pallas_call(
        functools.partial(_fox_kernel, tq=tq, tk=tk, seq_len=seq_len, pairs=pairs),
        out_shape=jax.ShapeDtypeStruct((T, FOX_WIDTH), BF16),
        grid=(batch * pairs, nq),
        in_specs=[pl.BlockSpec((tq, W), lambda bp, qi: ((bp // pairs) * nq + qi,
                                                         _FQ // FOX_PAIR + bp % pairs)),
                  rows_spec(_FK), rows_spec(_FV),
                  pl.BlockSpec((seq_len, GATE_LANES), lambda bp, qi: (bp // pairs, 0),
                               pipeline_mode=once),
                  pl.BlockSpec((1, HEAD_DIM), lambda bp, qi: (0, 0))],
        out_specs=pl.BlockSpec((tq, W), lambda bp, qi: ((bp // pairs) * nq + qi, bp % pairs)),
        scratch_shapes=[pltpu.VMEM((FOX_PAIR, seq_len, 2 * HEAD_DIM), BF16),
                        pltpu.VMEM((FOX_PAIR, seq_len, 2 * HEAD_DIM), BF16),
                        pltpu.VMEM((FOX_PAIR, tq, HEAD_DIM), F32),
                        pltpu.VMEM((FOX_PAIR, tq, 2 * HEAD_DIM), F32),
                        pltpu.VMEM((FOX_PAIR, tq, tk), F32),
                        pltpu.VMEM((FOX_PAIR, tq, tk), F32)],
        compiler_params=_params(("parallel", "arbitrary")),
        name="fox_attention",
    )(proj, proj, proj, gates_c, out_gain.reshape(1, HEAD_DIM))


_HALO = 8


def _short_conv_kernel(b_ref, c_ref, h_ref, w_ref, o_ref, buf_ref, *, steps_per_seq, ts):
    t = pl.program_id(1)

    @pl.when(t % steps_per_seq == 0)
    def _():
        buf_ref[0:_HALO, :] = jnp.zeros((_HALO, buf_ref.shape[1]), F32)

    buf_ref[_HALO:_HALO + ts, :] = c_ref[...].astype(F32) * h_ref[...].astype(F32)
    w = w_ref[...]
    y = w[0:1, :] * buf_ref[pl.ds(_HALO - 2, ts), :]
    y += w[1:2, :] * buf_ref[pl.ds(_HALO - 1, ts), :]
    y += w[2:3, :] * buf_ref[pl.ds(_HALO, ts), :]
    o_ref[...] = (b_ref[...].astype(F32) * y).astype(o_ref.dtype)
    buf_ref[0:_HALO, :] = buf_ref[ts:ts + _HALO, :]


def _short_conv(proj, conv_w, seq_len, ts=512, tc=512):
    T = proj.shape[0]
    ts = min(ts, seq_len)
    per = tc // HEAD_DIM

    def col(base):
        return pl.BlockSpec((ts, tc), lambda c, t: (t, base // per + c))

    return pl.pallas_call(
        functools.partial(_short_conv_kernel, steps_per_seq=seq_len // ts, ts=ts),
        out_shape=jax.ShapeDtypeStruct((T, SC_WIDTH), BF16),
        grid=(SC_WIDTH // tc, T // ts),
        in_specs=[col(_SB), col(_SC), col(_SH),
                  pl.BlockSpec((SC_KERNEL, tc), lambda c, t: (0, c))],
        out_specs=pl.BlockSpec((ts, tc), lambda c, t: (t, c)),
        scratch_shapes=[pltpu.VMEM((_HALO + ts, tc), F32)],
        compiler_params=_params(("parallel", "arbitrary")),
        name="short_conv",
    )(proj, proj, proj, conv_w)


def _bmm(a, b, dims):
    return lax.dot_general(a.astype(BF16), b.astype(BF16), dims, preferred_element_type=F32)


_B_NN = (((2,), (1,)), ((0,), (0,)))
_B_NT = (((2,), (2,)), ((0,), (0,)))


GDN_PAIR = 4


def _gdn_kernel(q_ref, k_ref, v_ref, z_ref, wq_ref, wk_ref, wv_ref, gc_ref, gr_ref, gain_ref,
                o_ref, qbuf, kbuf, vbuf, state_ref, obuf, *, rows):
    hp = pl.program_id(1)
    t = pl.program_id(2)
    n = rows // CHUNK
    D = HEAD_DIM

    @pl.when(t == 0)
    def _():
        zeros = jnp.zeros((_HALO, GDN_PAIR * D), F32)
        qbuf[0:_HALO, :] = zeros
        kbuf[0:_HALO, :] = zeros
        vbuf[0:_HALO, :] = zeros
        state_ref[...] = jnp.zeros_like(state_ref)

    def conv_silu(x_ref, w_ref, buf):
        buf[_HALO:_HALO + rows, :] = x_ref[...].astype(F32)
        w = w_ref[...]
        y = w[0:1, :] * buf[pl.ds(_HALO - 3, rows), :]
        y += w[1:2, :] * buf[pl.ds(_HALO - 2, rows), :]
        y += w[2:3, :] * buf[pl.ds(_HALO - 1, rows), :]
        y += w[3:4, :] * buf[pl.ds(_HALO, rows), :]
        buf[0:_HALO, :] = buf[rows:rows + _HALO, :]
        return y * jax.nn.sigmoid(y)

    q_all = conv_silu(q_ref, wq_ref, qbuf)
    k_all = conv_silu(k_ref, wk_ref, kbuf)
    v_all = conv_silu(v_ref, wv_ref, vbuf)

    gates = gc_ref[...]
    lane = lax.broadcasted_iota(jnp.int32, gates.shape, 1)
    ri = lax.broadcasted_iota(jnp.int32, (CHUNK, CHUNK), 0)
    ci = lax.broadcasted_iota(jnp.int32, (CHUNK, CHUNK), 1)
    tri_incl = (ci <= ri)[None]
    tri_strict = (ci < ri)[None]
    same16 = ((ri // 16) == (ci // 16))[None]
    same32 = ((ri // 32) == (ci // 32))[None]
    eye = (ri == ci).astype(F32)[None]

    def chunk_terms(gg):
        h = hp * GDN_PAIR + gg
        cols = slice(gg * D, (gg + 1) * D)
        q, k, v = q_all[:, cols], k_all[:, cols], v_all[:, cols]
        q = q * (lax.rsqrt(jnp.sum(q * q, axis=-1, keepdims=True) + RMS_EPS) * (D ** -0.5))
        k = k * lax.rsqrt(jnp.sum(k * k, axis=-1, keepdims=True) + RMS_EPS)
        g_cum = jnp.sum(jnp.where(lane == FOX_HEADS + h, gates, 0.0), axis=-1, keepdims=True)
        beta = jnp.sum(jnp.where(lane == FOX_HEADS + GDN_HEADS + h, gates, 0.0), axis=-1,
                       keepdims=True)
        q3 = q.reshape(n, CHUNK, D)
        k3 = k.reshape(n, CHUNK, D)
        v3 = v.reshape(n, CHUNK, D)
        g3 = g_cum.reshape(n, CHUNK, 1)
        b3 = beta.reshape(n, CHUNK, 1)
        g_row = gr_ref[:, pl.ds(FOX_HEADS + h, 1), :]
        g_last = g3[:, CHUNK - 1:CHUNK, :]

        decay = jnp.exp(jnp.where(tri_incl, g3 - g_row, -jnp.inf))
        kk = _bmm(k3, k3, _B_NT)
        L = jnp.where(tri_strict, b3 * kk * decay, 0.0)

        P = jnp.where(same16, L, 0.0)
        X = eye - P
        P2 = _bmm(P, P, _B_NN)
        X = _bmm(X, eye + P2, _B_NN)
        P4 = _bmm(P2, P2, _B_NN)
        X = _bmm(X, eye + P4, _B_NN)
        P8 = _bmm(P4, P4, _B_NN)
        X = _bmm(X, eye + P8, _B_NN)
        O32 = jnp.where(same32 & jnp.logical_not(same16), L, 0.0)
        X = X - _bmm(_bmm(X, O32, _B_NN), X, _B_NN)
        O64 = jnp.where(same32, 0.0, L)
        X = X - _bmm(_bmm(X, O64, _B_NN), X, _B_NN)

        e3 = jnp.exp(g3)
        rhs = jnp.concatenate([v3 * b3, k3 * (b3 * e3)], axis=-1)
        sol = _bmm(X, rhs, _B_NN)
        attn = jnp.where(tri_incl, _bmm(q3, k3, _B_NT) * decay, 0.0)
        return dict(u=sol[:, :, :D], w=sol[:, :, D:].astype(BF16), q=(q3 * e3).astype(BF16),
                    a=attn.astype(BF16), k=(k3 * jnp.exp(g_last - g3)).astype(BF16),
                    gl=jnp.exp(g_last))

    def pair_terms(p):
        t0, t1 = chunk_terms(2 * p), chunk_terms(2 * p + 1)
        return dict(
            u=jnp.concatenate([t0["u"], t1["u"]], axis=-1),
            wq=jnp.concatenate([jnp.concatenate([t0["w"], t1["w"]], axis=-1),
                                jnp.concatenate([t0["q"], t1["q"]], axis=-1)], axis=1),
            a=jnp.concatenate([t0["a"], t1["a"]], axis=-1),
            k=jnp.concatenate([t0["k"], t1["k"]], axis=-1),
            gl=jnp.concatenate([jnp.broadcast_to(t0["gl"], (n, 1, D)),
                                jnp.broadcast_to(t1["gl"], (n, 1, D))], axis=-1))

    def block_diag(x):
        first = lax.broadcasted_iota(jnp.int32, x.shape, 1) < D
        zero = jnp.zeros_like(x)
        return jnp.concatenate([jnp.where(first, x, zero), jnp.where(first, zero, x)], axis=0)

    n_pairs = GDN_PAIR // 2
    terms = [pair_terms(p) for p in range(n_pairs)]
    S = [state_ref[:, p * 2 * D:(p + 1) * 2 * D] for p in range(n_pairs)]
    for c in range(n):
        for p in range(n_pairs):
            tm = terms[p]
            r1 = jnp.dot(tm["wq"][c], block_diag(S[p].astype(BF16)), preferred_element_type=F32)
            v_b = (tm["u"][c] - r1[0:CHUNK]).astype(BF16)
            o_c = r1[CHUNK:2 * CHUNK] + jnp.dot(tm["a"][c], block_diag(v_b),
                                                preferred_element_type=F32)
            kv = lax.dot_general(tm["k"][c], v_b, _TN, preferred_element_type=F32)
            S[p] = S[p] * tm["gl"][c] + jnp.concatenate([kv[0:D, 0:D], kv[D:2 * D, D:2 * D]],
                                                         axis=1)
            obuf[c * CHUNK:(c + 1) * CHUNK, p * 2 * D:(p + 1) * 2 * D] = o_c
    for p in range(n_pairs):
        state_ref[:, p * 2 * D:(p + 1) * 2 * D] = S[p]

    for gg in range(GDN_PAIR):
        cols = slice(gg * D, (gg + 1) * D)
        o = obuf[:, cols]
        ms = jnp.mean(o * o, axis=-1, keepdims=True)
        z = z_ref[:, cols].astype(F32)
        o = o * lax.rsqrt(ms + RMS_EPS) * gain_ref[...] * (z * jax.nn.sigmoid(z))
        o_ref[:, cols] = o.astype(o_ref.dtype)


def _gdn(proj, proj_z, conv_w, gates_c, gates_r, out_gain, batch, seq_len, rows=512):
    T = proj.shape[0]
    rows = min(rows, seq_len)
    nt = seq_len // rows
    n = rows // CHUNK
    pairs = GDN_HEADS // GDN_PAIR
    W = GDN_PAIR * HEAD_DIM

    def col(base):
        return pl.BlockSpec((rows, W), lambda b, hp, t: (b * nt + t, base // GDN_PAIR + hp))

    def wcol(base):
        return pl.BlockSpec((GDN_CONV, W), lambda b, hp, t: (0, base // GDN_PAIR + hp))

    return pl.pallas_call(
        functools.partial(_gdn_kernel, rows=rows),
        out_shape=jax.ShapeDtypeStruct((T, GDN_WIDTH), BF16),
        grid=(batch, pairs, nt),
        in_specs=[col(_GQ), col(_GK), col(_GV), col(_GZ),
                  wcol(0), wcol(GDN_HEADS), wcol(2 * GDN_HEADS),
                  pl.BlockSpec((rows, GATE_LANES), lambda b, hp, t: (b * nt + t, 0)),
                  pl.BlockSpec((n, GATE_LANES, CHUNK), lambda b, hp, t: (b * nt + t, 0, 0)),
                  pl.BlockSpec((1, HEAD_DIM), lambda b, hp, t: (0, 0))],
        out_specs=pl.BlockSpec((rows, W), lambda b, hp, t: (b * nt + t, hp)),
        scratch_shapes=[pltpu.VMEM((_HALO + rows, W), F32),
                        pltpu.VMEM((_HALO + rows, W), F32),
                        pltpu.VMEM((_HALO + rows, W), F32),
                        pltpu.VMEM((HEAD_DIM, W), F32),
                        pltpu.VMEM((rows, W), F32)],
        compiler_params=_params(("parallel", "parallel", "arbitrary")),
        name="gdn",
    )(proj, proj, proj, proj_z, conv_w, conv_w, conv_w, gates_c, gates_r,
      out_gain.reshape(1, HEAD_DIM))


def _gate_weight_kernel(f_ref, ab_ref, o_ref):
    lane = lax.broadcasted_iota(jnp.int32, o_ref.shape, 1)
    w = jnp.where(lane < FOX_HEADS, f_ref[0],
                  jnp.where(lane < FOX_HEADS + 2 * GDN_HEADS, ab_ref[0], 0.0))
    o_ref[...] = w.astype(o_ref.dtype)


def _small_gate_weights(w_in, layer, tr=512):
    rows = w_in.shape[1]
    f_col, ab_col = _A_WIDTH, _B_COL + _B_WIDTH
    assert f_col % GATE_LANES == 0 and ab_col % GATE_LANES == FOX_HEADS
    return pl.pallas_call(
        _gate_weight_kernel,
        out_shape=jax.ShapeDtypeStruct((rows, GATE_LANES), BF16),
        grid=(rows // tr,),
        in_specs=[pl.BlockSpec((1, tr, GATE_LANES), lambda i: (layer, i, f_col // GATE_LANES)),
                  pl.BlockSpec((1, tr, GATE_LANES), lambda i: (layer, i, ab_col // GATE_LANES))],
        out_specs=pl.BlockSpec((tr, GATE_LANES), lambda i: (i, 0)),
        compiler_params=_params(("parallel",)),
        name="gate_weights",
    )(w_in, w_in)


def _gate_columns(fox_forget_bias, gdn_a_log, gdn_dt_bias):
    pad = GATE_LANES - FOX_HEADS - GDN_HEADS
    bias = jnp.concatenate([fox_forget_bias, gdn_dt_bias, jnp.zeros((pad,), F32)])
    alog = jnp.concatenate([jnp.zeros((FOX_HEADS,), F32), gdn_a_log, jnp.zeros((pad,), F32)])
    return bias.reshape(GATE_LANES, 1), alog.reshape(GATE_LANES, 1)


def _layer(x, u, batch, seq_len, layer, w_in, fox_forget_bias, fox_out_norm, sc_conv_w, gdn_conv_w,
           gdn_a_log, gdn_dt_bias, gdn_out_norm, w_out, mix_post_norm, ffn_pre_norm,
           w_gate, w_up, w_down, ffn_post_norm, next_pre_norm):
    T, D = x.shape
    w_small_t = _small_gate_weights(w_in, layer).T
    bias_col, alog_col = _gate_columns(fox_forget_bias, gdn_a_log, gdn_dt_bias)

    q_scale = jnp.concatenate([jnp.full((FOX_WIDTH,), FOX_Q_SCALE, F32),
                               jnp.ones((_A_WIDTH - FOX_WIDTH,), F32)]).reshape(1, _A_WIDTH)
    proj_a, w_out_b = _matmul(u, _cast_window(w_in, layer, _A_COL, _A_WIDTH), BF16, tm=1024,
                              tn=768, name="in_proj_a", col_scale=q_scale,
                              carried=_CarriedCast(w_out, layer))
    proj_b, wg = _matmul(u, _cast_window(w_in, layer, _B_COL, _B_WIDTH), BF16, tm=1024, tn=768,
                         name="in_proj_b", carried=_CarriedCast(w_gate, layer))
    proj_c, = _matmul(u, _cast_window(w_in, layer, _C_COL, _C_WIDTH), BF16, tm=1024, tn=768,
                      name="in_proj_c")
    gates_t = _gates(u, w_small_t, bias_col, alog_col, seq_len)
    gates_c = gates_t.T
    gates_r = gates_t.reshape(GATE_LANES, T // CHUNK, CHUNK).transpose(1, 0, 2)

    fox_out = _fox_attention(proj_a, gates_c, fox_out_norm, batch, seq_len)
    sc_out = _short_conv(proj_b, sc_conv_w, seq_len)
    gdn_out = _gdn(proj_b, proj_c, gdn_conv_w, gates_c, gates_r, gdn_out_norm, batch, seq_len)

    y, wu = _out_proj(fox_out, sc_out, gdn_out,
                      w_out_b[:FOX_WIDTH], w_out_b[FOX_WIDTH:FOX_WIDTH + SC_WIDTH],
                      w_out_b[FOX_WIDTH + SC_WIDTH:], _CarriedCast(w_up, layer))
    h, v = _norm_residual(y, x, mix_post_norm, ffn_pre_norm)

    act, wd = _ffn_up(v, wg, wu, _CarriedCast(w_down, layer))
    y2, = _matmul(act, wd, F32, tm=512, tn=512, name="ffn_down")
    return _norm_residual(y2, h, ffn_post_norm, next_pre_norm)


def kernel(x, mix_pre_norm, w_in, fox_forget_bias, fox_out_norm, sc_conv_w, gdn_conv_w, gdn_a_log,
           gdn_dt_bias, gdn_out_norm, w_out, mix_post_norm, ffn_pre_norm, w_gate, w_up, w_down,
           ffn_post_norm):
    B, S, D = x.shape
    depth = w_in.shape[0]
    h = x.reshape(B * S, D)
    u = _norm_cast(h, mix_pre_norm[0])
    for l in range(depth):
        nxt = mix_pre_norm[l + 1] if l + 1 < depth else None
        h, u = _layer(h, u, B, S, l, w_in, fox_forget_bias[l], fox_out_norm[l], sc_conv_w[l],
                      gdn_conv_w[l], gdn_a_log[l], gdn_dt_bias[l], gdn_out_norm[l], w_out,
                      mix_post_norm[l], ffn_pre_norm[l], w_gate, w_up, w_down,
                      ffn_post_norm[l], nxt)
    return h.reshape(B, S, D)
```

```python
import functools
import math
from typing import NamedTuple

import jax
import jax.numpy as jnp
from jax import lax
from jax.experimental import pallas as pl
from jax.experimental.pallas import tpu as pltpu

F32 = jnp.float32
BF16 = jnp.bfloat16

HEAD_DIM = 128
FOX_HEADS = 12
FOX_WIDTH = FOX_HEADS * HEAD_DIM
SC_WIDTH = 8 * HEAD_DIM
SC_KERNEL = 3
GDN_HEADS = 12
GDN_WIDTH = GDN_HEADS * HEAD_DIM
GDN_CONV = 4
CHUNK = 64
RMS_EPS = 1e-6
GATE_LANES = 128

VMEM_LIMIT_BYTES = 56 * 1024 * 1024

_FQ, _FK, _FV = 0, 12, 24
_SB, _SC, _SH = 0, 8, 16
_GQ, _GK, _GV = 24, 36, 48
_GZ = 0
_A_COL, _A_WIDTH = 0, 3 * FOX_WIDTH
_B_COL, _B_WIDTH = 3 * FOX_WIDTH + FOX_HEADS, 3 * SC_WIDTH + 3 * GDN_WIDTH
_C_COL, _C_WIDTH = _B_COL + _B_WIDTH + 2 * GDN_HEADS, GDN_WIDTH

_NT = (((1,), (1,)), ((), ()))
_TN = (((0,), (0,)), ((), ()))
_HI = lax.Precision.HIGHEST


def _params(sem):
    return pltpu.CompilerParams(dimension_semantics=sem, vmem_limit_bytes=VMEM_LIMIT_BYTES)


def _norm_cast_kernel(x_ref, g_ref, o_ref):
    x = x_ref[...]
    ms = jnp.mean(x * x, axis=-1, keepdims=True)
    o_ref[...] = (x * lax.rsqrt(ms + RMS_EPS) * g_ref[...]).astype(o_ref.dtype)


def _norm_cast(x, gain, tm=256):
    T, D = x.shape
    tm = min(tm, T)
    return pl.pallas_call(
        _norm_cast_kernel,
        out_shape=jax.ShapeDtypeStruct((T, D), BF16),
        grid=(T // tm,),
        in_specs=[pl.BlockSpec((tm, D), lambda i: (i, 0)),
                  pl.BlockSpec((1, D), lambda i: (0, 0))],
        out_specs=pl.BlockSpec((tm, D), lambda i: (i, 0)),
        compiler_params=_params(("parallel",)),
        name="norm_cast",
    )(x, gain.reshape(1, D))


def _norm_residual_kernel(y_ref, x_ref, g_ref, gn_ref, h_ref, *maybe_u_ref):
    y = y_ref[...]
    ms = jnp.mean(y * y, axis=-1, keepdims=True)
    h = x_ref[...] + y * lax.rsqrt(ms + RMS_EPS) * g_ref[...]
    h_ref[...] = h
    if maybe_u_ref:
        ms2 = jnp.mean(h * h, axis=-1, keepdims=True)
        maybe_u_ref[0][...] = (h * lax.rsqrt(ms2 + RMS_EPS) * gn_ref[...]).astype(BF16)


def _norm_residual(y, x, gain, next_gain, tm=256):
    T, D = x.shape
    tm = min(tm, T)
    emit_next = next_gain is not None
    gn = (next_gain if emit_next else gain).reshape(1, D)
    row = pl.BlockSpec((tm, D), lambda i: (i, 0))
    vec = pl.BlockSpec((1, D), lambda i: (0, 0))
    out_shape = [jax.ShapeDtypeStruct((T, D), F32)]
    out_specs = [row]
    if emit_next:
        out_shape.append(jax.ShapeDtypeStruct((T, D), BF16))
        out_specs.append(row)
    res = pl.pallas_call(
        _norm_residual_kernel,
        out_shape=out_shape,
        grid=(T // tm,),
        in_specs=[row, row, vec, vec],
        out_specs=out_specs,
        compiler_params=_params(("parallel",)),
        name="norm_residual",
    )(y, x, gain.reshape(1, D), gn)
    return (res[0], res[1]) if emit_next else (res[0], None)


def _window_t_kernel(a_ref, b_ref, *o_refs, off, depth):
    tn = o_refs[0].shape[0]
    kt = a_ref.shape[1] // depth
    for l in range(depth):
        x = a_ref[:, pl.ds(l, kt, stride=depth), :]
        if off:
            x = jnp.concatenate([x, b_ref[:, pl.ds(l, kt, stride=depth), :]], axis=0)[off:off + tn]
        xt = pltpu.einshape("nkl->knl", x)
        for k in range(kt):
            o_refs[l][:, k * HEAD_DIM:(k + 1) * HEAD_DIM] = xt[k].astype(BF16)


def _column_major_view(w_in):
    depth, d, p = w_in.shape
    kt = d // HEAD_DIM
    v = jnp.transpose(w_in, (2, 0, 1)).reshape(p, depth, kt, HEAD_DIM)
    return jnp.transpose(v, (0, 2, 1, 3)).reshape(p, kt * depth, HEAD_DIM)


def _window_t(w_view, depth, src_col, width, tn=256):
    p, rows, lanes = w_view.shape
    d = rows // depth * lanes
    base, off = src_col // tn, src_col % tn
    assert width % tn == 0
    return pl.pallas_call(
        functools.partial(_window_t_kernel, off=off, depth=depth),
        out_shape=[jax.ShapeDtypeStruct((width, d), BF16)] * depth,
        grid=(width // tn,),
        in_specs=[pl.BlockSpec((tn, rows, lanes), lambda j: (base + j, 0, 0)),
                  pl.BlockSpec((tn, rows, lanes), lambda j: (base + j + 1, 0, 0))],
        out_specs=[pl.BlockSpec((tn, d), lambda j: (j, 0))] * depth,
        compiler_params=_params(("parallel",)),
        name="window_t",
    )(w_view, w_view)


class _CarriedCast(NamedTuple):
    w_stack: jax.Array
    layer: int


_BF16_SUBLANES = 16


def _carried_tile_rows(rows, steps):
    for t in range(_BF16_SUBLANES, rows, _BF16_SUBLANES):
        if rows % t == 0 and rows // t <= steps:
            return t
    return rows


def _call(kernel, carried, grid, in_specs, out_specs, out_shape, args, name):
    n_in, n_out = len(in_specs), len(out_specs)
    body = kernel
    if carried is not None:
        _, rows, cols = carried.w_stack.shape
        tile_rows = _carried_tile_rows(rows, math.prod(grid))
        n_tiles = rows // tile_rows

        def tile(*g):
            step = 0
            for size, idx in zip(grid, g):
                step = step * size + idx
            return jnp.minimum(step, n_tiles - 1)

        in_specs = in_specs + [pl.BlockSpec((1, tile_rows, cols),
                                            lambda *g: (carried.layer, tile(*g), 0))]
        out_specs = out_specs + [pl.BlockSpec((tile_rows, cols), lambda *g: (tile(*g), 0))]
        out_shape = out_shape + [jax.ShapeDtypeStruct((rows, cols), BF16)]
        args = args + [carried.w_stack]

        def body(*refs):
            src, dst = refs[n_in], refs[n_in + 1 + n_out]
            dst[...] = src[0].astype(dst.dtype)
            kernel(*refs[:n_in], *refs[n_in + 1:n_in + 1 + n_out], *refs[n_in + 2 + n_out:])

    return pl.pallas_call(
        body, out_shape=out_shape, grid=grid, in_specs=in_specs, out_specs=out_specs,
        compiler_params=_params(("arbitrary",) * len(grid)), name=name)(*args)


def _matmul_kernel(x_ref, w_ref, *rest, scaled, w_is_t):
    if w_is_t:
        acc = lax.dot_general(x_ref[...], w_ref[...], _NT, preferred_element_type=F32)
    else:
        acc = jnp.dot(x_ref[...], w_ref[...], preferred_element_type=F32)
    if scaled:
        s_ref, o_ref = rest
        acc = acc * s_ref[...]
    else:
        o_ref, = rest
    o_ref[...] = acc.astype(o_ref.dtype)


def _matmul(x, w, out_dtype, tm, tn, name, col_scale=None, carried=None, w_is_t=False):
    M, K = x.shape
    N = w.shape[0] if w_is_t else w.shape[1]
    tm, tn = min(tm, M), min(tn, N)
    scaled = col_scale is not None
    in_specs = [pl.BlockSpec((tm, K), lambda i, j: (i, 0)),
                pl.BlockSpec((tn, K), lambda i, j: (j, 0)) if w_is_t
                else pl.BlockSpec((K, tn), lambda i, j: (0, j))]
    args = [x, w]
    if scaled:
        in_specs.append(pl.BlockSpec((1, tn), lambda i, j: (0, j)))
        args.append(col_scale)
    return _call(functools.partial(_matmul_kernel, scaled=scaled, w_is_t=w_is_t), carried,
                 (M // tm, N // tn), in_specs, [pl.BlockSpec((tm, tn), lambda i, j: (i, j))],
                 [jax.ShapeDtypeStruct((M, N), out_dtype)], args, name)


def _out_proj_kernel(a1_ref, a2_ref, a3_ref, w1_ref, w2_ref, w3_ref, o_ref):
    acc = jnp.dot(a1_ref[...], w1_ref[...], preferred_element_type=F32)
    acc += jnp.dot(a2_ref[...], w2_ref[...], preferred_element_type=F32)
    acc += jnp.dot(a3_ref[...], w3_ref[...], preferred_element_type=F32)
    o_ref[...] = acc


def _out_proj(a1, a2, a3, w1, w2, w3, carried, tm=1024, tn=1024):
    T = a1.shape[0]
    N = w1.shape[1]
    tm, tn = min(tm, T), min(tn, N)

    def lhs(a):
        return pl.BlockSpec((tm, a.shape[1]), lambda i, j: (i, 0))

    def rhs(w):
        return pl.BlockSpec((w.shape[0], tn), lambda i, j: (0, j))

    return _call(_out_proj_kernel, carried, (T // tm, N // tn),
                 [lhs(a1), lhs(a2), lhs(a3), rhs(w1), rhs(w2), rhs(w3)],
                 [pl.BlockSpec((tm, tn), lambda i, j: (i, j))],
                 [jax.ShapeDtypeStruct((T, N), F32)], [a1, a2, a3, w1, w2, w3], "out_proj")


def _ffn_up_kernel(x_ref, wg_ref, wu_ref, o_ref):
    x = x_ref[...]
    g = jnp.dot(x, wg_ref[...], preferred_element_type=F32)
    u = jnp.dot(x, wu_ref[...], preferred_element_type=F32)
    o_ref[...] = (g * jax.nn.sigmoid(g) * u).astype(o_ref.dtype)


def _ffn_up(x, wg, wu, carried, tm=1024, tn=512):
    T, K = x.shape
    N = wg.shape[1]
    tm = min(tm, T)
    return _call(_ffn_up_kernel, carried, (T // tm, pl.cdiv(N, tn)),
                 [pl.BlockSpec((tm, K), lambda i, j: (i, 0)),
                  pl.BlockSpec((K, tn), lambda i, j: (0, j)),
                  pl.BlockSpec((K, tn), lambda i, j: (0, j))],
                 [pl.BlockSpec((tm, tn), lambda i, j: (i, j))],
                 [jax.ShapeDtypeStruct((T, N), BF16)], [x, wg, wu], "ffn_up")


def _gates_kernel(w_ref, u_ref, bias_ref, alog_ref, gt_ref, carry_ref, *, steps_per_seq, ts):
    t = pl.program_id(0)

    @pl.when(t % steps_per_seq == 0)
    def _():
        carry_ref[...] = jnp.zeros_like(carry_ref)

    z = lax.dot_general(w_ref[...], u_ref[...], _NT, preferred_element_type=F32) + bias_ref[...]
    row = lax.broadcasted_iota(jnp.int32, z.shape, 0)
    tail = jnp.log1p(jnp.exp(-jnp.abs(z)))
    log_sig = jnp.minimum(z, 0.0) - tail
    softplus = jnp.maximum(z, 0.0) + tail
    sig = 1.0 / (1.0 + jnp.exp(-z))
    decay = -jnp.exp(alog_ref[...]) * softplus
    val = jnp.where(row < FOX_HEADS, log_sig, jnp.where(row < FOX_HEADS + GDN_HEADS, decay, sig))

    src = lax.broadcasted_iota(jnp.int32, (ts, ts), 0)
    dst = lax.broadcasted_iota(jnp.int32, (ts, ts), 1)
    upper = src <= dst
    same_chunk = (src // CHUNK) == (dst // CHUNK)
    cum_all = jnp.dot(val, upper.astype(F32), precision=_HI, preferred_element_type=F32)
    cum_chunk = jnp.dot(val, (upper & same_chunk).astype(F32), precision=_HI,
                        preferred_element_type=F32)
    cum_all = cum_all + carry_ref[...]
    carry_ref[...] = cum_all[:, ts - 1:ts]
    gt_ref[...] = jnp.where(row < FOX_HEADS, cum_all,
                            jnp.where(row < FOX_HEADS + GDN_HEADS, cum_chunk, val))


def _gates(u, w_small_t, bias_col, alog_col, seq_len, ts=512):
    T, D = u.shape
    ts = min(ts, seq_len)
    return pl.pallas_call(
        functools.partial(_gates_kernel, steps_per_seq=seq_len // ts, ts=ts),
        out_shape=jax.ShapeDtypeStruct((GATE_LANES, T), F32),
        grid=(T // ts,),
        in_specs=[pl.BlockSpec((GATE_LANES, D), lambda t: (0, 0)),
                  pl.BlockSpec((ts, D), lambda t: (t, 0)),
                  pl.BlockSpec((GATE_LANES, 1), lambda t: (0, 0)),
                  pl.BlockSpec((GATE_LANES, 1), lambda t: (0, 0))],
        out_specs=pl.BlockSpec((GATE_LANES, ts), lambda t: (0, t)),
        scratch_shapes=[pltpu.VMEM((GATE_LANES, 1), F32)],
        compiler_params=_params(("arbitrary",)),
        name="gates",
    )(w_small_t, u, bias_col, alog_col)


FOX_Q_SCALE = (HEAD_DIM ** -0.5) * math.log2(math.e)
FOX_PAIR = 2
_FOX_BUILD_ROWS = 512


def _fox_kernel(q_ref, k_ref, v_ref, gc_ref, g_ref, o_ref, kaug, vaug, m_ref, acc_ref, sa_ref,
                sb_ref, *,
                tq, tk, seq_len, pairs):
    D = HEAD_DIM
    pair = pl.program_id(0) % pairs
    qi = pl.program_id(1)

    @pl.when(qi == 0)
    def _build():
        ri = lax.broadcasted_iota(jnp.int32, (3 * D, D), 0)
        ci = lax.broadcasted_iota(jnp.int32, (3 * D, D), 1)
        ones = jnp.ones((_FOX_BUILD_ROWS, D), BF16)

        def chunk(i, carry):
            r0 = pl.multiple_of(i * _FOX_BUILD_ROWS, _FOX_BUILD_ROWS)
            rows = pl.ds(r0, _FOX_BUILD_ROWS)
            g = gc_ref[rows, :] * (-math.log2(math.e))
            hi = g.astype(BF16)
            r1 = g - hi.astype(F32)
            mid = r1.astype(BF16)
            lo = (r1 - mid.astype(F32)).astype(BF16)
            pieces = jnp.concatenate([hi, mid, lo], axis=1)
            for gg in range(FOX_PAIR):
                h = pair * FOX_PAIR + gg
                sel = (((ri == h) & (ci == 0)) | ((ri == D + h) & (ci == 1))
                       | ((ri == 2 * D + h) & (ci == 2)))
                aug = jnp.dot(pieces, sel.astype(BF16), preferred_element_type=F32)
                kaug[gg, rows, 0:D] = k_ref[rows, gg * D:(gg + 1) * D]
                kaug[gg, rows, D:2 * D] = aug.astype(BF16)
                vaug[gg, rows, 0:D] = v_ref[rows, gg * D:(gg + 1) * D]
                vaug[gg, rows, D:2 * D] = ones
            return carry

        lax.fori_loop(0, seq_len // _FOX_BUILD_ROWS, chunk, 0)

    lane = lax.broadcasted_iota(jnp.int32, (tq, D), 1)
    ones3 = jnp.where(lane < 3, 1.0, 0.0).astype(BF16)
    q_aug = [jnp.concatenate([q_ref[:, gg * D:(gg + 1) * D], ones3], axis=1)
             for gg in range(FOX_PAIR)]
    m_ref[...] = jnp.full_like(m_ref, -jnp.inf)
    acc_ref[...] = jnp.zeros_like(acc_ref)

    def scores(kj, slot_ref):
        k0 = pl.multiple_of(kj * tk, tk)
        for gg in range(FOX_PAIR):
            slot_ref[gg] = lax.dot_general(q_aug[gg], kaug[gg, pl.ds(k0, tk), :], _NT,
                                           preferred_element_type=F32)

    def accumulate(kj, slot_ref, masked=False):
        k0 = pl.multiple_of(kj * tk, tk)
        for gg in range(FOX_PAIR):
            s = slot_ref[gg]
            if masked:
                r = qi * tq + lax.broadcasted_iota(jnp.int32, s.shape, 0)
                c = kj * tk + lax.broadcasted_iota(jnp.int32, s.shape, 1)
                s = jnp.where(c <= r, s, -jnp.inf)
            m_prev = m_ref[gg]
            m_new = jnp.maximum(m_prev, jnp.max(s, axis=-1, keepdims=True))
            alpha = jnp.exp2(m_prev - m_new)
            p = jnp.concatenate(
                [jnp.exp2(s[:, j * D:(j + 1) * D] - m_new) for j in range(tk // D)],
                axis=1).astype(BF16)
            pv = jnp.dot(p, vaug[gg, pl.ds(k0, tk), :], preferred_element_type=F32)
            acc_ref[gg] = jnp.concatenate([alpha, alpha], axis=1) * acc_ref[gg] + pv
            m_ref[gg] = m_new

    n_full = (qi * tq) // tk
    scores(0, sa_ref)

    def body(i, carry):
        scores(2 * i + 1, sb_ref)
        accumulate(2 * i, sa_ref)
        scores(2 * i + 2, sa_ref)
        accumulate(2 * i + 1, sb_ref)
        return carry

    lax.fori_loop(0, n_full // 2, body, 0)

    @pl.when(n_full % 2 == 0)
    def _():
        accumulate(n_full, sa_ref, masked=True)

    @pl.when(n_full % 2 == 1)
    def _():
        scores(n_full, sb_ref)
        accumulate(n_full - 1, sa_ref)
        accumulate(n_full, sb_ref, masked=True)

    for gg in range(FOX_PAIR):
        acc = acc_ref[gg]
        out = acc[:, 0:D] / acc[:, D:2 * D]
        ms = jnp.mean(out * out, axis=-1, keepdims=True)
        o_ref[:, gg * D:(gg + 1) * D] = (out * lax.rsqrt(ms + RMS_EPS) * g_ref[...]).astype(o_ref.dtype)


def _fox_attention(proj, gates_c, out_gain, batch, seq_len, tq=512, tk=1024):
    T = proj.shape[0]
    tq = min(tq, seq_len)
    tk = min(tk, seq_len)
    assert tk % tq == 0 and seq_len % tk == 0
    nq = seq_len // tq
    pairs = FOX_HEADS // FOX_PAIR
    W = FOX_PAIR * HEAD_DIM

    once = pl.Buffered(1)

    def rows_spec(base):
        return pl.BlockSpec((seq_len, W), lambda bp, qi: (bp // pairs, base // FOX_PAIR + bp % pairs),
                            pipeline_mode=once)

    return pl.pallas_call(
        functools.partial(_fox_kernel, tq=tq, tk=tk, seq_len=seq_len, pairs=pairs),
        out_shape=jax.ShapeDtypeStruct((T, FOX_WIDTH), BF16),
        grid=(batch * pairs, nq),
        in_specs=[pl.BlockSpec((tq, W), lambda bp, qi: ((bp // pairs) * nq + qi,
                                                         _FQ // FOX_PAIR + bp % pairs)),
                  rows_spec(_FK), rows_spec(_FV),
                  pl.BlockSpec((seq_len, GATE_LANES), lambda bp, qi: (bp // pairs, 0),
                               pipeline_mode=once),
                  pl.BlockSpec((1, HEAD_DIM), lambda bp, qi: (0, 0))],
        out_specs=pl.BlockSpec((tq, W), lambda bp, qi: ((bp // pairs) * nq + qi, bp % pairs)),
        scratch_shapes=[pltpu.VMEM((FOX_PAIR, seq_len, 2 * HEAD_DIM), BF16),
                        pltpu.VMEM((FOX_PAIR, seq_len, 2 * HEAD_DIM), BF16),
                        pltpu.VMEM((FOX_PAIR, tq, HEAD_DIM), F32),
                        pltpu.VMEM((FOX_PAIR, tq, 2 * HEAD_DIM), F32),
                        pltpu.VMEM((FOX_PAIR, tq, tk), F32),
                        pltpu.VMEM((FOX_PAIR, tq, tk), F32)],
        compiler_params=_params(("parallel", "arbitrary")),
        name="fox_attention",
    )(proj, proj, proj, gates_c, out_gain.reshape(1, HEAD_DIM))


_HALO = 8


def _short_conv_kernel(b_ref, c_ref, h_ref, w_ref, o_ref, buf_ref, *, steps_per_seq, ts):
    t = pl.program_id(1)

    @pl.when(t % steps_per_seq == 0)
    def _():
        buf_ref[0:_HALO, :] = jnp.zeros((_HALO, buf_ref.shape[1]), F32)

    buf_ref[_HALO:_HALO + ts, :] = c_ref[...].astype(F32) * h_ref[...].astype(F32)
    w = w_ref[...]
    y = w[0:1, :] * buf_ref[pl.ds(_HALO - 2, ts), :]
    y += w[1:2, :] * buf_ref[pl.ds(_HALO - 1, ts), :]
    y += w[2:3, :] * buf_ref[pl.ds(_HALO, ts), :]
    o_ref[...] = (b_ref[...].astype(F32) * y).astype(o_ref.dtype)
    buf_ref[0:_HALO, :] = buf_ref[ts:ts + _HALO, :]


def _short_conv(proj, conv_w, seq_len, ts=512, tc=512):
    T = proj.shape[0]
    ts = min(ts, seq_len)
    per = tc // HEAD_DIM

    def col(base):
        return pl.BlockSpec((ts, tc), lambda c, t: (t, base // per + c))

    return pl.pallas_call(
        functools.partial(_short_conv_kernel, steps_per_seq=seq_len // ts, ts=ts),
        out_shape=jax.ShapeDtypeStruct((T, SC_WIDTH), BF16),
        grid=(SC_WIDTH // tc, T // ts),
        in_specs=[col(_SB), col(_SC), col(_SH),
                  pl.BlockSpec((SC_KERNEL, tc), lambda c, t: (0, c))],
        out_specs=pl.BlockSpec((ts, tc), lambda c, t: (t, c)),
        scratch_shapes=[pltpu.VMEM((_HALO + ts, tc), F32)],
        compiler_params=_params(("parallel", "arbitrary")),
        name="short_conv",
    )(proj, proj, proj, conv_w)


def _bmm(a, b, dims):
    return lax.dot_general(a.astype(BF16), b.astype(BF16), dims, preferred_element_type=F32)


_B_NN = (((2,), (1,)), ((0,), (0,)))
_B_NT = (((2,), (2,)), ((0,), (0,)))


GDN_PAIR = 4


def _gdn_kernel(q_ref, k_ref, v_ref, z_ref, wq_ref, wk_ref, wv_ref, gc_ref, gr_ref, gain_ref,
                o_ref, qbuf, kbuf, vbuf, state_ref, obuf, *, rows):
    hp = pl.program_id(1)
    t = pl.program_id(2)
    n = rows // CHUNK
    D = HEAD_DIM

    @pl.when(t == 0)
    def _():
        zeros = jnp.zeros((_HALO, GDN_PAIR * D), F32)
        qbuf[0:_HALO, :] = zeros
        kbuf[0:_HALO, :] = zeros
        vbuf[0:_HALO, :] = zeros
        state_ref[...] = jnp.zeros_like(state_ref)

    def conv_silu(x_ref, w_ref, buf):
        buf[_HALO:_HALO + rows, :] = x_ref[...].astype(F32)
        w = w_ref[...]
        y = w[0:1, :] * buf[pl.ds(_HALO - 3, rows), :]
        y += w[1:2, :] * buf[pl.ds(_HALO - 2, rows), :]
        y += w[2:3, :] * buf[pl.ds(_HALO - 1, rows), :]
        y += w[3:4, :] * buf[pl.ds(_HALO, rows), :]
        buf[0:_HALO, :] = buf[rows:rows + _HALO, :]
        return y * jax.nn.sigmoid(y)

    q_all = conv_silu(q_ref, wq_ref, qbuf)
    k_all = conv_silu(k_ref, wk_ref, kbuf)
    v_all = conv_silu(v_ref, wv_ref, vbuf)

    gates = gc_ref[...]
    lane = lax.broadcasted_iota(jnp.int32, gates.shape, 1)
    ri = lax.broadcasted_iota(jnp.int32, (CHUNK, CHUNK), 0)
    ci = lax.broadcasted_iota(jnp.int32, (CHUNK, CHUNK), 1)
    tri_incl = (ci <= ri)[None]
    tri_strict = (ci < ri)[None]
    same16 = ((ri // 16) == (ci // 16))[None]
    same32 = ((ri // 32) == (ci // 32))[None]
    eye = (ri == ci).astype(F32)[None]

    def chunk_terms(gg):
        h = hp * GDN_PAIR + gg
        cols = slice(gg * D, (gg + 1) * D)
        q, k, v = q_all[:, cols], k_all[:, cols], v_all[:, cols]
        q = q * (lax.rsqrt(jnp.sum(q * q, axis=-1, keepdims=True) + RMS_EPS) * (D ** -0.5))
        k = k * lax.rsqrt(jnp.sum(k * k, axis=-1, keepdims=True) + RMS_EPS)
        g_cum = jnp.sum(jnp.where(lane == FOX_HEADS + h, gates, 0.0), axis=-1, keepdims=True)
        beta = jnp.sum(jnp.where(lane == FOX_HEADS + GDN_HEADS + h, gates, 0.0), axis=-1,
                       keepdims=True)
        q3 = q.reshape(n, CHUNK, D)
        k3 = k.reshape(n, CHUNK, D)
        v3 = v.reshape(n, CHUNK, D)
        g3 = g_cum.reshape(n, CHUNK, 1)
        b3 = beta.reshape(n, CHUNK, 1)
        g_row = gr_ref[:, pl.ds(FOX_HEADS + h, 1), :]
        g_last = g3[:, CHUNK - 1:CHUNK, :]

        decay = jnp.exp(jnp.where(tri_incl, g3 - g_row, -jnp.inf))
        kk = _bmm(k3, k3, _B_NT)
        L = jnp.where(tri_strict, b3 * kk * decay, 0.0)

        P = jnp.where(same16, L, 0.0)
        X = eye - P
        P2 = _bmm(P, P, _B_NN)
        X = _bmm(X, eye + P2, _B_NN)
        P4 = _bmm(P2, P2, _B_NN)
        X = _bmm(X, eye + P4, _B_NN)
        P8 = _bmm(P4, P4, _B_NN)
        X = _bmm(X, eye + P8, _B_NN)
        O32 = jnp.where(same32 & jnp.logical_not(same16), L, 0.0)
        X = X - _bmm(_bmm(X, O32, _B_NN), X, _B_NN)
        O64 = jnp.where(same32, 0.0, L)
        X = X - _bmm(_bmm(X, O64, _B_NN), X, _B_NN)

        e3 = jnp.exp(g3)
        rhs = jnp.concatenate([v3 * b3, k3 * (b3 * e3)], axis=-1)
        sol = _bmm(X, rhs, _B_NN)
        attn = jnp.where(tri_incl, _bmm(q3, k3, _B_NT) * decay, 0.0)
        return dict(u=sol[:, :, :D], w=sol[:, :, D:].astype(BF16), q=(q3 * e3).astype(BF16),
                    a=attn.astype(BF16), k=(k3 * jnp.exp(g_last - g3)).astype(BF16),
                    gl=jnp.exp(g_last))

    def pair_terms(p):
        t0, t1 = chunk_terms(2 * p), chunk_terms(2 * p + 1)
        return dict(
            u=jnp.concatenate([t0["u"], t1["u"]], axis=-1),
            wq=jnp.concatenate([jnp.concatenate([t0["w"], t1["w"]], axis=-1),
                                jnp.concatenate([t0["q"], t1["q"]], axis=-1)], axis=1),
            a=jnp.concatenate([t0["a"], t1["a"]], axis=-1),
            k=jnp.concatenate([t0["k"], t1["k"]], axis=-1),
            gl=jnp.concatenate([jnp.broadcast_to(t0["gl"], (n, 1, D)),
                                jnp.broadcast_to(t1["gl"], (n, 1, D))], axis=-1))

    def block_diag(x):
        first = lax.broadcasted_iota(jnp.int32, x.shape, 1) < D
        zero = jnp.zeros_like(x)
        return jnp.concatenate([jnp.where(first, x, zero), jnp.where(first, zero, x)], axis=0)

    n_pairs = GDN_PAIR // 2
    terms = [pair_terms(p) for p in range(n_pairs)]
    S = [state_ref[:, p * 2 * D:(p + 1) * 2 * D] for p in range(n_pairs)]
    for c in range(n):
        for p in range(n_pairs):
            tm = terms[p]
            r1 = jnp.dot(tm["wq"][c], block_diag(S[p].astype(BF16)), preferred_element_type=F32)
            v_b = (tm["u"][c] - r1[0:CHUNK]).astype(BF16)
            o_c = r1[CHUNK:2 * CHUNK] + jnp.dot(tm["a"][c], block_diag(v_b),
                                                preferred_element_type=F32)
            kv = lax.dot_general(tm["k"][c], v_b, _TN, preferred_element_type=F32)
            S[p] = S[p] * tm["gl"][c] + jnp.concatenate([kv[0:D, 0:D], kv[D:2 * D, D:2 * D]],
                                                         axis=1)
            obuf[c * CHUNK:(c + 1) * CHUNK, p * 2 * D:(p + 1) * 2 * D] = o_c
    for p in range(n_pairs):
        state_ref[:, p * 2 * D:(p + 1) * 2 * D] = S[p]

    for gg in range(GDN_PAIR):
        cols = slice(gg * D, (gg + 1) * D)
        o = obuf[:, cols]
        ms = jnp.mean(o * o, axis=-1, keepdims=True)
        z = z_ref[:, cols].astype(F32)
        o = o * lax.rsqrt(ms + RMS_EPS) * gain_ref[...] * (z * jax.nn.sigmoid(z))
        o_ref[:, cols] = o.astype(o_ref.dtype)


def _gdn(proj, proj_z, conv_w, gates_c, gates_r, out_gain, batch, seq_len, rows=512):
    T = proj.shape[0]
    rows = min(rows, seq_len)
    nt = seq_len // rows
    n = rows // CHUNK
    pairs = GDN_HEADS // GDN_PAIR
    W = GDN_PAIR * HEAD_DIM

    def col(base):
        return pl.BlockSpec((rows, W), lambda b, hp, t: (b * nt + t, base // GDN_PAIR + hp))

    def wcol(base):
        return pl.BlockSpec((GDN_CONV, W), lambda b, hp, t: (0, base // GDN_PAIR + hp))

    return pl.pallas_call(
        functools.partial(_gdn_kernel, rows=rows),
        out_shape=jax.ShapeDtypeStruct((T, GDN_WIDTH), BF16),
        grid=(batch, pairs, nt),
        in_specs=[col(_GQ), col(_GK), col(_GV), col(_GZ),
                  wcol(0), wcol(GDN_HEADS), wcol(2 * GDN_HEADS),
                  pl.BlockSpec((rows, GATE_LANES), lambda b, hp, t: (b * nt + t, 0)),
                  pl.BlockSpec((n, GATE_LANES, CHUNK), lambda b, hp, t: (b * nt + t, 0, 0)),
                  pl.BlockSpec((1, HEAD_DIM), lambda b, hp, t: (0, 0))],
        out_specs=pl.BlockSpec((rows, W), lambda b, hp, t: (b * nt + t, hp)),
        scratch_shapes=[pltpu.VMEM((_HALO + rows, W), F32),
                        pltpu.VMEM((_HALO + rows, W), F32),
                        pltpu.VMEM((_HALO + rows, W), F32),
                        pltpu.VMEM((HEAD_DIM, W), F32),
                        pltpu.VMEM((rows, W), F32)],
        compiler_params=_params(("parallel", "parallel", "arbitrary")),
        name="gdn",
    )(proj, proj, proj, proj_z, conv_w, conv_w, conv_w, gates_c, gates_r,
      out_gain.reshape(1, HEAD_DIM))


_GATE_F_ROWS, _GATE_AB_ROWS = 16, 64


def _gate_rows_kernel(f_ref, ab_ref, *o_refs, depth):
    kt = f_ref.shape[1] // depth
    row = lax.broadcasted_iota(jnp.int32, (GATE_LANES, HEAD_DIM), 0)
    for l in range(depth):
        f = pltpu.einshape("nkl->knl", f_ref[:, pl.ds(l, kt, stride=depth), :])
        ab = pltpu.einshape("nkl->knl", ab_ref[:, pl.ds(l, kt, stride=depth), :])
        for k in range(kt):
            fk = jnp.concatenate([f[k], jnp.zeros((GATE_LANES - _GATE_F_ROWS, HEAD_DIM), F32)], axis=0)
            abk = jnp.concatenate([ab[k], jnp.zeros((GATE_LANES - _GATE_AB_ROWS, HEAD_DIM), F32)],
                                  axis=0)
            w = jnp.where(row < FOX_HEADS, fk,
                          jnp.where(row < FOX_HEADS + 2 * GDN_HEADS, abk, 0.0))
            o_refs[l][:, k * HEAD_DIM:(k + 1) * HEAD_DIM] = w.astype(BF16)


def _small_gate_weights_t(w_view, depth):
    p, rows, lanes = w_view.shape
    d = rows // depth * lanes
    f_col, ab_col = _A_WIDTH, _B_COL + _B_WIDTH
    assert f_col % _GATE_F_ROWS == 0 and ab_col % _GATE_AB_ROWS == FOX_HEADS
    return pl.pallas_call(
        functools.partial(_gate_rows_kernel, depth=depth),
        out_shape=[jax.ShapeDtypeStruct((GATE_LANES, d), BF16)] * depth,
        grid=(1,),
        in_specs=[pl.BlockSpec((_GATE_F_ROWS, rows, lanes), lambda i: (f_col // _GATE_F_ROWS, 0, 0)),
                  pl.BlockSpec((_GATE_AB_ROWS, rows, lanes), lambda i: (ab_col // _GATE_AB_ROWS, 0, 0))],
        out_specs=[pl.BlockSpec((GATE_LANES, d), lambda i: (0, 0))] * depth,
        compiler_params=_params(("arbitrary",)),
        name="gate_rows",
    )(w_view, w_view)


def _gate_columns(fox_forget_bias, gdn_a_log, gdn_dt_bias):
    pad = GATE_LANES - FOX_HEADS - GDN_HEADS
    bias = jnp.concatenate([fox_forget_bias, gdn_dt_bias, jnp.zeros((pad,), F32)])
    alog = jnp.concatenate([jnp.zeros((FOX_HEADS,), F32), gdn_a_log, jnp.zeros((pad,), F32)])
    return bias.reshape(GATE_LANES, 1), alog.reshape(GATE_LANES, 1)


def _layer(x, u, batch, seq_len, layer, w_in, fox_forget_bias, fox_out_norm, sc_conv_w, gdn_conv_w,
           gdn_a_log, gdn_dt_bias, gdn_out_norm, w_out, mix_post_norm, ffn_pre_norm,
           w_gate, w_up, w_down, ffn_post_norm, next_pre_norm):
    T, D = x.shape
    wa_t, wb_t, wc_t, w_small_t = w_in
    bias_col, alog_col = _gate_columns(fox_forget_bias, gdn_a_log, gdn_dt_bias)

    q_scale = jnp.concatenate([jnp.full((FOX_WIDTH,), FOX_Q_SCALE, F32),
                               jnp.ones((_A_WIDTH - FOX_WIDTH,), F32)]).reshape(1, _A_WIDTH)
    proj_a, w_out_b = _matmul(u, wa_t, BF16, tm=1024, tn=768, name="in_proj_a", w_is_t=True,
                              col_scale=q_scale, carried=_CarriedCast(w_out, layer))
    proj_b, wg = _matmul(u, wb_t, BF16, tm=1024, tn=768, name="in_proj_b", w_is_t=True,
                         carried=_CarriedCast(w_gate, layer))
    proj_c, = _matmul(u, wc_t, BF16, tm=1024, tn=768, name="in_proj_c", w_is_t=True)
    gates_t = _gates(u, w_small_t, bias_col, alog_col, seq_len)
    gates_c = gates_t.T
    gates_r = gates_t.reshape(GATE_LANES, T // CHUNK, CHUNK).transpose(1, 0, 2)

    fox_out = _fox_attention(proj_a, gates_c, fox_out_norm, batch, seq_len)
    sc_out = _short_conv(proj_b, sc_conv_w, seq_len)
    gdn_out = _gdn(proj_b, proj_c, gdn_conv_w, gates_c, gates_r, gdn_out_norm, batch, seq_len)

    y, wu = _out_proj(fox_out, sc_out, gdn_out,
                      w_out_b[:FOX_WIDTH], w_out_b[FOX_WIDTH:FOX_WIDTH + SC_WIDTH],
                      w_out_b[FOX_WIDTH + SC_WIDTH:], _CarriedCast(w_up, layer))
    h, v = _norm_residual(y, x, mix_post_norm, ffn_pre_norm)

    act, wd = _ffn_up(v, wg, wu, _CarriedCast(w_down, layer))
    y2, = _matmul(act, wd, F32, tm=512, tn=512, name="ffn_down")
    return _norm_residual(y2, h, ffn_post_norm, next_pre_norm)


def kernel(x, mix_pre_norm, w_in, fox_forget_bias, fox_out_norm, sc_conv_w, gdn_conv_w, gdn_a_log,
           gdn_dt_bias, gdn_out_norm, w_out, mix_post_norm, ffn_pre_norm, w_gate, w_up, w_down,
           ffn_post_norm):
    B, S, D = x.shape
    depth = w_in.shape[0]
    h = x.reshape(B * S, D)
    u = _norm_cast(h, mix_pre_norm[0])
    w_view = _column_major_view(w_in)
    windows = [_window_t(w_view, depth, col, width)
               for col, width in ((_A_COL, _A_WIDTH), (_B_COL, _B_WIDTH), (_C_COL, _C_WIDTH))]
    windows.append(_small_gate_weights_t(w_view, depth))
    for l in range(depth):
        nxt = mix_pre_norm[l + 1] if l + 1 < depth else None
        w_in_l = tuple(win[l] for win in windows)
        h, u = _layer(h, u, B, S, l, w_in_l, fox_forget_bias[l], fox_out_norm[l], sc_conv_w[l],
                      gdn_conv_w[l], gdn_a_log[l], gdn_dt_bias[l], gdn_out_norm[l], w_out,
                      mix_post_norm[l], ffn_pre_norm[l], w_gate, w_up, w_down,
                      ffn_post_norm[l], nxt)
    return h.reshape(B, S, D)
```

```python
import functools
import math
from typing import NamedTuple

import jax
import jax.numpy as jnp
from jax import lax
from jax.experimental import pallas as pl
from jax.experimental.pallas import tpu as pltpu

F32 = jnp.float32
BF16 = jnp.bfloat16

HEAD_DIM = 128
FOX_HEADS = 12
FOX_WIDTH = FOX_HEADS * HEAD_DIM
SC_WIDTH = 8 * HEAD_DIM
SC_KERNEL = 3
GDN_HEADS = 12
GDN_WIDTH = GDN_HEADS * HEAD_DIM
GDN_CONV = 4
CHUNK = 64
RMS_EPS = 1e-6
GATE_LANES = 128

VMEM_LIMIT_BYTES = 56 * 1024 * 1024

_FQ, _FK, _FV = 0, 12, 24
_SB, _SC, _SH = 0, 8, 16
_GQ, _GK, _GV = 24, 36, 48
_GZ = 0
_A_COL, _A_WIDTH = 0, 3 * FOX_WIDTH
_B_COL, _B_WIDTH = 3 * FOX_WIDTH + FOX_HEADS, 3 * SC_WIDTH + 3 * GDN_WIDTH
_C_COL, _C_WIDTH = _B_COL + _B_WIDTH + 2 * GDN_HEADS, GDN_WIDTH

_NT = (((1,), (1,)), ((), ()))
_TN = (((0,), (0,)), ((), ()))
_HI = lax.Precision.HIGHEST


def _params(sem):
    return pltpu.CompilerParams(dimension_semantics=sem, vmem_limit_bytes=VMEM_LIMIT_BYTES)


def _norm_cast_kernel(x_ref, g_ref, o_ref):
    x = x_ref[...]
    ms = jnp.mean(x * x, axis=-1, keepdims=True)
    o_ref[...] = (x * lax.rsqrt(ms + RMS_EPS) * g_ref[...]).astype(o_ref.dtype)


def _norm_cast(x, gain, tm=256):
    T, D = x.shape
    tm = min(tm, T)
    return pl.pallas_call(
        _norm_cast_kernel,
        out_shape=jax.ShapeDtypeStruct((T, D), BF16),
        grid=(T // tm,),
        in_specs=[pl.BlockSpec((tm, D), lambda i: (i, 0)),
                  pl.BlockSpec((1, D), lambda i: (0, 0))],
        out_specs=pl.BlockSpec((tm, D), lambda i: (i, 0)),
        compiler_params=_params(("parallel",)),
        name="norm_cast",
    )(x, gain.reshape(1, D))


def _norm_residual_kernel(y_ref, x_ref, g_ref, gn_ref, h_ref, *maybe_u_ref):
    y = y_ref[...].astype(F32)
    ms = jnp.mean(y * y, axis=-1, keepdims=True)
    h = x_ref[...] + y * lax.rsqrt(ms + RMS_EPS) * g_ref[...]
    h_ref[...] = h
    if maybe_u_ref:
        ms2 = jnp.mean(h * h, axis=-1, keepdims=True)
        maybe_u_ref[0][...] = (h * lax.rsqrt(ms2 + RMS_EPS) * gn_ref[...]).astype(BF16)


def _norm_residual(y, x, gain, next_gain, tm=256):
    T, D = x.shape
    tm = min(tm, T)
    emit_next = next_gain is not None
    gn = (next_gain if emit_next else gain).reshape(1, D)
    row = pl.BlockSpec((tm, D), lambda i: (i, 0))
    vec = pl.BlockSpec((1, D), lambda i: (0, 0))
    out_shape = [jax.ShapeDtypeStruct((T, D), F32)]
    out_specs = [row]
    if emit_next:
        out_shape.append(jax.ShapeDtypeStruct((T, D), BF16))
        out_specs.append(row)
    res = pl.pallas_call(
        _norm_residual_kernel,
        out_shape=out_shape,
        grid=(T // tm,),
        in_specs=[row, row, vec, vec],
        out_specs=out_specs,
        compiler_params=_params(("parallel",)),
        name="norm_residual",
    )(y, x, gain.reshape(1, D), gn)
    return (res[0], res[1]) if emit_next else (res[0], None)


def _window_t_kernel(a_ref, b_ref, *o_refs, off, depth):
    tn = o_refs[0].shape[0]
    kt = a_ref.shape[1] // depth
    for l in range(depth):
        x = a_ref[:, pl.ds(l, kt, stride=depth), :]
        if off:
            x = jnp.concatenate([x, b_ref[:, pl.ds(l, kt, stride=depth), :]], axis=0)[off:off + tn]
        xt = pltpu.einshape("nkl->knl", x)
        for k in range(kt):
            o_refs[l][:, k * HEAD_DIM:(k + 1) * HEAD_DIM] = xt[k].astype(BF16)


def _column_major_view(w_in):
    depth, d, p = w_in.shape
    kt = d // HEAD_DIM
    v = jnp.transpose(w_in, (2, 0, 1)).reshape(p, depth, kt, HEAD_DIM)
    return jnp.transpose(v, (0, 2, 1, 3)).reshape(p, kt * depth, HEAD_DIM)


def _window_t(w_view, depth, src_col, width, tn=256):
    p, rows, lanes = w_view.shape
    d = rows // depth * lanes
    base, off = src_col // tn, src_col % tn
    assert width % tn == 0
    return pl.pallas_call(
        functools.partial(_window_t_kernel, off=off, depth=depth),
        out_shape=[jax.ShapeDtypeStruct((width, d), BF16)] * depth,
        grid=(width // tn,),
        in_specs=[pl.BlockSpec((tn, rows, lanes), lambda j: (base + j, 0, 0)),
                  pl.BlockSpec((tn, rows, lanes), lambda j: (base + j + 1, 0, 0))],
        out_specs=[pl.BlockSpec((tn, d), lambda j: (j, 0))] * depth,
        compiler_params=_params(("parallel",)),
        name="window_t",
    )(w_view, w_view)


class _CarriedCast(NamedTuple):
    w_stack: jax.Array
    layer: int


_BF16_SUBLANES = 16


def _carried_tile_rows(rows, steps):
    for t in range(_BF16_SUBLANES, rows, _BF16_SUBLANES):
        if rows % t == 0 and rows // t <= steps:
            return t
    return rows


def _call(kernel, carried, grid, in_specs, out_specs, out_shape, args, name):
    n_in, n_out = len(in_specs), len(out_specs)
    body = kernel
    if carried is not None:
        _, rows, cols = carried.w_stack.shape
        tile_rows = _carried_tile_rows(rows, math.prod(grid))
        n_tiles = rows // tile_rows

        def tile(*g):
            step = 0
            for size, idx in zip(grid, g):
                step = step * size + idx
            return jnp.minimum(step, n_tiles - 1)

        in_specs = in_specs + [pl.BlockSpec((1, tile_rows, cols),
                                            lambda *g: (carried.layer, tile(*g), 0))]
        out_specs = out_specs + [pl.BlockSpec((tile_rows, cols), lambda *g: (tile(*g), 0))]
        out_shape = out_shape + [jax.ShapeDtypeStruct((rows, cols), BF16)]
        args = args + [carried.w_stack]

        def body(*refs):
            src, dst = refs[n_in], refs[n_in + 1 + n_out]
            dst[...] = src[0].astype(dst.dtype)
            kernel(*refs[:n_in], *refs[n_in + 1:n_in + 1 + n_out], *refs[n_in + 2 + n_out:])

    return pl.pallas_call(
        body, out_shape=out_shape, grid=grid, in_specs=in_specs, out_specs=out_specs,
        compiler_params=_params(("arbitrary",) * len(grid)), name=name)(*args)


def _matmul_kernel(x_ref, w_ref, *rest, scaled, w_is_t):
    if w_is_t:
        acc = lax.dot_general(x_ref[...], w_ref[...], _NT, preferred_element_type=F32)
    else:
        acc = jnp.dot(x_ref[...], w_ref[...], preferred_element_type=F32)
    if scaled:
        s_ref, o_ref = rest
        acc = acc * s_ref[...]
    else:
        o_ref, = rest
    o_ref[...] = acc.astype(o_ref.dtype)


def _matmul(x, w, out_dtype, tm, tn, name, col_scale=None, carried=None, w_is_t=False):
    M, K = x.shape
    N = w.shape[0] if w_is_t else w.shape[1]
    tm, tn = min(tm, M), min(tn, N)
    scaled = col_scale is not None
    in_specs = [pl.BlockSpec((tm, K), lambda i, j: (i, 0)),
                pl.BlockSpec((tn, K), lambda i, j: (j, 0)) if w_is_t
                else pl.BlockSpec((K, tn), lambda i, j: (0, j))]
    args = [x, w]
    if scaled:
        in_specs.append(pl.BlockSpec((1, tn), lambda i, j: (0, j)))
        args.append(col_scale)
    return _call(functools.partial(_matmul_kernel, scaled=scaled, w_is_t=w_is_t), carried,
                 (M // tm, N // tn), in_specs, [pl.BlockSpec((tm, tn), lambda i, j: (i, j))],
                 [jax.ShapeDtypeStruct((M, N), out_dtype)], args, name)


def _out_proj_kernel(a1_ref, a2_ref, a3_ref, w1_ref, w2_ref, w3_ref, o_ref):
    acc = jnp.dot(a1_ref[...], w1_ref[...], preferred_element_type=F32)
    acc += jnp.dot(a2_ref[...], w2_ref[...], preferred_element_type=F32)
    acc += jnp.dot(a3_ref[...], w3_ref[...], preferred_element_type=F32)
    o_ref[...] = acc.astype(o_ref.dtype)


def _out_proj(a1, a2, a3, w1, w2, w3, carried, tm=1024, tn=1024):
    T = a1.shape[0]
    N = w1.shape[1]
    tm, tn = min(tm, T), min(tn, N)

    def lhs(a):
        return pl.BlockSpec((tm, a.shape[1]), lambda i, j: (i, 0))

    def rhs(w):
        return pl.BlockSpec((w.shape[0], tn), lambda i, j: (0, j))

    return _call(_out_proj_kernel, carried, (T // tm, N // tn),
                 [lhs(a1), lhs(a2), lhs(a3), rhs(w1), rhs(w2), rhs(w3)],
                 [pl.BlockSpec((tm, tn), lambda i, j: (i, j))],
                 [jax.ShapeDtypeStruct((T, N), BF16)], [a1, a2, a3, w1, w2, w3], "out_proj")


def _ffn_up_kernel(x_ref, wg_ref, wu_ref, o_ref):
    x = x_ref[...]
    g = jnp.dot(x, wg_ref[...], preferred_element_type=F32)
    u = jnp.dot(x, wu_ref[...], preferred_element_type=F32)
    o_ref[...] = (g * jax.nn.sigmoid(g) * u).astype(o_ref.dtype)


def _ffn_up(x, wg, wu, carried, tm=1024, tn=512):
    T, K = x.shape
    N = wg.shape[1]
    tm = min(tm, T)
    return _call(_ffn_up_kernel, carried, (T // tm, pl.cdiv(N, tn)),
                 [pl.BlockSpec((tm, K), lambda i, j: (i, 0)),
                  pl.BlockSpec((K, tn), lambda i, j: (0, j)),
                  pl.BlockSpec((K, tn), lambda i, j: (0, j))],
                 [pl.BlockSpec((tm, tn), lambda i, j: (i, j))],
                 [jax.ShapeDtypeStruct((T, N), BF16)], [x, wg, wu], "ffn_up")


def _gates_kernel(w_ref, u_ref, bias_ref, alog_ref, gt_ref, carry_ref, *, steps_per_seq, ts):
    t = pl.program_id(0)

    @pl.when(t % steps_per_seq == 0)
    def _():
        carry_ref[...] = jnp.zeros_like(carry_ref)

    z = lax.dot_general(w_ref[...], u_ref[...], _NT, preferred_element_type=F32) + bias_ref[...]
    row = lax.broadcasted_iota(jnp.int32, z.shape, 0)
    tail = jnp.log1p(jnp.exp(-jnp.abs(z)))
    log_sig = jnp.minimum(z, 0.0) - tail
    softplus = jnp.maximum(z, 0.0) + tail
    sig = 1.0 / (1.0 + jnp.exp(-z))
    decay = -jnp.exp(alog_ref[...]) * softplus
    val = jnp.where(row < FOX_HEADS, log_sig, jnp.where(row < FOX_HEADS + GDN_HEADS, decay, sig))

    src = lax.broadcasted_iota(jnp.int32, (ts, ts), 0)
    dst = lax.broadcasted_iota(jnp.int32, (ts, ts), 1)
    upper = src <= dst
    same_chunk = (src // CHUNK) == (dst // CHUNK)
    cum_all = jnp.dot(val, upper.astype(F32), precision=_HI, preferred_element_type=F32)
    cum_chunk = jnp.dot(val, (upper & same_chunk).astype(F32), precision=_HI,
                        preferred_element_type=F32)
    cum_all = cum_all + carry_ref[...]
    carry_ref[...] = cum_all[:, ts - 1:ts]
    gt_ref[...] = jnp.where(row < FOX_HEADS, cum_all,
                            jnp.where(row < FOX_HEADS + GDN_HEADS, cum_chunk, val))


def _gates(u, w_small_t, bias_col, alog_col, seq_len, ts=512):
    T, D = u.shape
    ts = min(ts, seq_len)
    return pl.pallas_call(
        functools.partial(_gates_kernel, steps_per_seq=seq_len // ts, ts=ts),
        out_shape=jax.ShapeDtypeStruct((GATE_LANES, T), F32),
        grid=(T // ts,),
        in_specs=[pl.BlockSpec((GATE_LANES, D), lambda t: (0, 0)),
                  pl.BlockSpec((ts, D), lambda t: (t, 0)),
                  pl.BlockSpec((GATE_LANES, 1), lambda t: (0, 0)),
                  pl.BlockSpec((GATE_LANES, 1), lambda t: (0, 0))],
        out_specs=pl.BlockSpec((GATE_LANES, ts), lambda t: (0, t)),
        scratch_shapes=[pltpu.VMEM((GATE_LANES, 1), F32)],
        compiler_params=_params(("arbitrary",)),
        name="gates",
    )(w_small_t, u, bias_col, alog_col)


FOX_Q_SCALE = (HEAD_DIM ** -0.5) * math.log2(math.e)
FOX_PAIR = 2
_FOX_BUILD_ROWS = 512


def _fox_kernel(q_ref, qn_ref, k_ref, v_ref, gc_ref, g_ref, o_ref, kaug, vaug, m_ref, acc_ref,
                s0_ref, sa_ref, sb_ref, *, tq, tk, seq_len, pairs):
    D = HEAD_DIM
    pair = pl.program_id(0) % pairs
    qi = pl.program_id(1)

    @pl.when(qi == 0)
    def _build():
        ri = lax.broadcasted_iota(jnp.int32, (3 * D, D), 0)
        ci = lax.broadcasted_iota(jnp.int32, (3 * D, D), 1)
        ones = jnp.ones((_FOX_BUILD_ROWS, D), BF16)

        def chunk(i, carry):
            r0 = pl.multiple_of(i * _FOX_BUILD_ROWS, _FOX_BUILD_ROWS)
            rows = pl.ds(r0, _FOX_BUILD_ROWS)
            g = gc_ref[rows, :] * (-math.log2(math.e))
            hi = g.astype(BF16)
            r1 = g - hi.astype(F32)
            mid = r1.astype(BF16)
            lo = (r1 - mid.astype(F32)).astype(BF16)
            pieces = jnp.concatenate([hi, mid, lo], axis=1)
            for gg in range(FOX_PAIR):
                h = pair * FOX_PAIR + gg
                sel = (((ri == h) & (ci == 0)) | ((ri == D + h) & (ci == 1))
                       | ((ri == 2 * D + h) & (ci == 2)))
                aug = jnp.dot(pieces, sel.astype(BF16), preferred_element_type=F32)
                kaug[gg, rows, 0:D] = k_ref[rows, gg * D:(gg + 1) * D]
                kaug[gg, rows, D:2 * D] = aug.astype(BF16)
                vaug[gg, rows, 0:D] = v_ref[rows, gg * D:(gg + 1) * D]
                vaug[gg, rows, D:2 * D] = ones
            return carry

        lax.fori_loop(0, seq_len // _FOX_BUILD_ROWS, chunk, 0)

    lane = lax.broadcasted_iota(jnp.int32, (tq, D), 1)
    ones3 = jnp.where(lane < 3, 1.0, 0.0).astype(BF16)
    m_ref[...] = jnp.full_like(m_ref, -jnp.inf)
    acc_ref[...] = jnp.zeros_like(acc_ref)

    def scores(kj, slot_ref, queries=q_ref):
        k0 = pl.multiple_of(kj * tk, tk)
        for gg in range(FOX_PAIR):
            q_aug = jnp.concatenate([queries[:, gg * D:(gg + 1) * D], ones3], axis=1)
            slot_ref[gg] = lax.dot_general(q_aug, kaug[gg, pl.ds(k0, tk), :], _NT,
                                           preferred_element_type=F32)

    def accumulate(kj, slot_ref, masked=False):
        k0 = pl.multiple_of(kj * tk, tk)
        for gg in range(FOX_PAIR):
            s = slot_ref[gg]
            if masked:
                r = qi * tq + lax.broadcasted_iota(jnp.int32, s.shape, 0)
                c = kj * tk + lax.broadcasted_iota(jnp.int32, s.shape, 1)
                s = jnp.where(c <= r, s, -jnp.inf)
            m_prev = m_ref[gg]
            m_new = jnp.maximum(m_prev, jnp.max(s, axis=-1, keepdims=True))
            alpha = jnp.exp2(m_prev - m_new)
            p = jnp.concatenate(
                [jnp.exp2(s[:, j * D:(j + 1) * D] - m_new) for j in range(tk // D)],
                axis=1).astype(BF16)
            pv = jnp.dot(p, vaug[gg, pl.ds(k0, tk), :], preferred_element_type=F32)
            acc_ref[gg] = jnp.concatenate([alpha, alpha], axis=1) * acc_ref[gg] + pv
            m_ref[gg] = m_new

    n_full = (qi * tq) // tk
    first_prefetching_step = tk // tq

    @pl.when(qi <= first_prefetching_step)
    def _():
        scores(0, s0_ref)

    @pl.when(n_full == 0)
    def _():
        accumulate(0, s0_ref, masked=True)

    @pl.when(n_full >= 1)
    def _():
        scores(1, sa_ref)
        accumulate(0, s0_ref)

    def body(i, carry):
        scores(2 * i + 2, sb_ref)
        accumulate(2 * i + 1, sa_ref)
        scores(2 * i + 3, sa_ref)
        accumulate(2 * i + 2, sb_ref)
        return carry

    lax.fori_loop(0, (n_full - 1) // 2, body, 0)

    @pl.when((n_full >= 1) & (n_full % 2 == 1))
    def _():
        scores(0, s0_ref, qn_ref)
        accumulate(n_full, sa_ref, masked=True)

    @pl.when((n_full >= 1) & (n_full % 2 == 0))
    def _():
        scores(n_full, sb_ref)
        accumulate(n_full - 1, sa_ref)
        scores(0, s0_ref, qn_ref)
        accumulate(n_full, sb_ref, masked=True)

    for gg in range(FOX_PAIR):
        acc = acc_ref[gg]
        out = acc[:, 0:D] / acc[:, D:2 * D]
        ms = jnp.mean(out * out, axis=-1, keepdims=True)
        o_ref[:, gg * D:(gg + 1) * D] = (out * lax.rsqrt(ms + RMS_EPS) * g_ref[...]).astype(o_ref.dtype)


def _fox_attention(proj, gates_c, out_gain, batch, seq_len, tq=512, tk=1024):
    T = proj.shape[0]
    tq = min(tq, seq_len)
    tk = min(tk, seq_len)
    assert tk % tq == 0 and seq_len % tk == 0
    nq = seq_len // tq
    pairs = FOX_HEADS // FOX_PAIR
    W = FOX_PAIR * HEAD_DIM

    once = pl.Buffered(1)

    def rows_spec(base):
        return pl.BlockSpec((seq_len, W), lambda bp, qi: (bp // pairs, base // FOX_PAIR + bp % pairs),
                            pipeline_mode=once)

    return pl.pallas_call(
        functools.partial(_fox_kernel, tq=tq, tk=tk, seq_len=seq_len, pairs=pairs),
        out_shape=jax.ShapeDtypeStruct((T, FOX_WIDTH), BF16),
        grid=(batch * pairs, nq),
        in_specs=[pl.BlockSpec((tq, W), lambda bp, qi: ((bp // pairs) * nq + qi,
                                                         _FQ // FOX_PAIR + bp % pairs)),
                  pl.BlockSpec((tq, W), lambda bp, qi: ((bp // pairs) * nq + jnp.minimum(qi + 1, nq - 1),
                                                         _FQ // FOX_PAIR + bp % pairs)),
                  rows_spec(_FK), rows_spec(_FV),
                  pl.BlockSpec((seq_len, GATE_LANES), lambda bp, qi: (bp // pairs, 0),
                               pipeline_mode=once),
                  pl.BlockSpec((1, HEAD_DIM), lambda bp, qi: (0, 0))],
        out_specs=pl.BlockSpec((tq, W), lambda bp, qi: ((bp // pairs) * nq + qi, bp % pairs)),
        scratch_shapes=[pltpu.VMEM((FOX_PAIR, seq_len, 2 * HEAD_DIM), BF16),
                        pltpu.VMEM((FOX_PAIR, seq_len, 2 * HEAD_DIM), BF16),
                        pltpu.VMEM((FOX_PAIR, tq, HEAD_DIM), F32),
                        pltpu.VMEM((FOX_PAIR, tq, 2 * HEAD_DIM), F32),
                        pltpu.VMEM((FOX_PAIR, tq, tk), F32),
                        pltpu.VMEM((FOX_PAIR, tq, tk), F32),
                        pltpu.VMEM((FOX_PAIR, tq, tk), F32)],
        compiler_params=_params(("parallel", "arbitrary")),
        name="fox_attention",
    )(proj, proj, proj, proj, gates_c, out_gain.reshape(1, HEAD_DIM))


_HALO = 8


def _short_conv_kernel(b_ref, c_ref, h_ref, w_ref, o_ref, buf_ref, *, steps_per_seq, ts):
    t = pl.program_id(1)

    @pl.when(t % steps_per_seq == 0)
    def _():
        buf_ref[0:_HALO, :] = jnp.zeros((_HALO, buf_ref.shape[1]), F32)

    buf_ref[_HALO:_HALO + ts, :] = c_ref[...].astype(F32) * h_ref[...].astype(F32)
    w = w_ref[...]
    y = w[0:1, :] * buf_ref[pl.ds(_HALO - 2, ts), :]
    y += w[1:2, :] * buf_ref[pl.ds(_HALO - 1, ts), :]
    y += w[2:3, :] * buf_ref[pl.ds(_HALO, ts), :]
    o_ref[...] = (b_ref[...].astype(F32) * y).astype(o_ref.dtype)
    buf_ref[0:_HALO, :] = buf_ref[ts:ts + _HALO, :]


def _short_conv(proj, conv_w, seq_len, ts=512, tc=512):
    T = proj.shape[0]
    ts = min(ts, seq_len)
    per = tc // HEAD_DIM

    def col(base):
        return pl.BlockSpec((ts, tc), lambda c, t: (t, base // per + c))

    return pl.pallas_call(
        functools.partial(_short_conv_kernel, steps_per_seq=seq_len // ts, ts=ts),
        out_shape=jax.ShapeDtypeStruct((T, SC_WIDTH), BF16),
        grid=(SC_WIDTH // tc, T // ts),
        in_specs=[col(_SB), col(_SC), col(_SH),
                  pl.BlockSpec((SC_KERNEL, tc), lambda c, t: (0, c))],
        out_specs=pl.BlockSpec((ts, tc), lambda c, t: (t, c)),
        scratch_shapes=[pltpu.VMEM((_HALO + ts, tc), F32)],
        compiler_params=_params(("parallel", "arbitrary")),
        name="short_conv",
    )(proj, proj, proj, conv_w)


def _bmm(a, b, dims):
    return lax.dot_general(a.astype(BF16), b.astype(BF16), dims, preferred_element_type=F32)


_B_NN = (((2,), (1,)), ((0,), (0,)))
_B_NT = (((2,), (2,)), ((0,), (0,)))


GDN_PAIR = 4


def _gdn_kernel(q_ref, k_ref, v_ref, z_ref, wq_ref, wk_ref, wv_ref, gc_ref, gr_ref, gain_ref,
                o_ref, qbuf, kbuf, vbuf, state_ref, obuf, *, rows):
    hp = pl.program_id(1)
    t = pl.program_id(2)
    n = rows // CHUNK
    D = HEAD_DIM

    @pl.when(t == 0)
    def _():
        zeros = jnp.zeros((_HALO, GDN_PAIR * D), F32)
        qbuf[0:_HALO, :] = zeros
        kbuf[0:_HALO, :] = zeros
        vbuf[0:_HALO, :] = zeros
        state_ref[...] = jnp.zeros_like(state_ref)

    def conv_silu(x_ref, w_ref, buf):
        buf[_HALO:_HALO + rows, :] = x_ref[...].astype(F32)
        w = w_ref[...]
        y = w[0:1, :] * buf[pl.ds(_HALO - 3, rows), :]
        y += w[1:2, :] * buf[pl.ds(_HALO - 2, rows), :]
        y += w[2:3, :] * buf[pl.ds(_HALO - 1, rows), :]
        y += w[3:4, :] * buf[pl.ds(_HALO, rows), :]
        buf[0:_HALO, :] = buf[rows:rows + _HALO, :]
        return y * jax.nn.sigmoid(y)

    q_all = conv_silu(q_ref, wq_ref, qbuf)
    k_all = conv_silu(k_ref, wk_ref, kbuf)
    v_all = conv_silu(v_ref, wv_ref, vbuf)

    gates = gc_ref[...]
    lane = lax.broadcasted_iota(jnp.int32, gates.shape, 1)
    ri = lax.broadcasted_iota(jnp.int32, (CHUNK, CHUNK), 0)
    ci = lax.broadcasted_iota(jnp.int32, (CHUNK, CHUNK), 1)
    tri_incl = (ci <= ri)[None]
    tri_strict = (ci < ri)[None]
    same16 = ((ri // 16) == (ci // 16))[None]
    same32 = ((ri // 32) == (ci // 32))[None]
    eye = (ri == ci).astype(F32)[None]

    def chunk_terms(gg):
        h = hp * GDN_PAIR + gg
        cols = slice(gg * D, (gg + 1) * D)
        q, k, v = q_all[:, cols], k_all[:, cols], v_all[:, cols]
        q = q * (lax.rsqrt(jnp.sum(q * q, axis=-1, keepdims=True) + RMS_EPS) * (D ** -0.5))
        k = k * lax.rsqrt(jnp.sum(k * k, axis=-1, keepdims=True) + RMS_EPS)
        g_cum = jnp.sum(jnp.where(lane == FOX_HEADS + h, gates, 0.0), axis=-1, keepdims=True)
        beta = jnp.sum(jnp.where(lane == FOX_HEADS + GDN_HEADS + h, gates, 0.0), axis=-1,
                       keepdims=True)
        q3 = q.reshape(n, CHUNK, D)
        k3 = k.reshape(n, CHUNK, D)
        v3 = v.reshape(n, CHUNK, D)
        g3 = g_cum.reshape(n, CHUNK, 1)
        b3 = beta.reshape(n, CHUNK, 1)
        g_row = gr_ref[:, pl.ds(FOX_HEADS + h, 1), :]
        g_last = g3[:, CHUNK - 1:CHUNK, :]

        decay = jnp.exp(jnp.where(tri_incl, g3 - g_row, -jnp.inf))
        kk = _bmm(k3, k3, _B_NT)
        L = jnp.where(tri_strict, b3 * kk * decay, 0.0)

        P = jnp.where(same16, L, 0.0)
        X = eye - P
        P2 = _bmm(P, P, _B_NN)
        X = _bmm(X, eye + P2, _B_NN)
        P4 = _bmm(P2, P2, _B_NN)
        X = _bmm(X, eye + P4, _B_NN)
        P8 = _bmm(P4, P4, _B_NN)
        X = _bmm(X, eye + P8, _B_NN)
        O32 = jnp.where(same32 & jnp.logical_not(same16), L, 0.0)
        X = X - _bmm(_bmm(X, O32, _B_NN), X, _B_NN)
        O64 = jnp.where(same32, 0.0, L)
        X = X - _bmm(_bmm(X, O64, _B_NN), X, _B_NN)

        e3 = jnp.exp(g3)
        rhs = jnp.concatenate([v3 * b3, k3 * (b3 * e3)], axis=-1)
        sol = _bmm(X, rhs, _B_NN)
        attn = jnp.where(tri_incl, _bmm(q3, k3, _B_NT) * decay, 0.0)
        return dict(u=sol[:, :, :D], w=sol[:, :, D:].astype(BF16), q=(q3 * e3).astype(BF16),
                    a=attn.astype(BF16), k=(k3 * jnp.exp(g_last - g3)).astype(BF16),
                    gl=jnp.exp(g_last))

    def pair_terms(p):
        t0, t1 = chunk_terms(2 * p), chunk_terms(2 * p + 1)
        return dict(
            u=jnp.concatenate([t0["u"], t1["u"]], axis=-1),
            wq=jnp.concatenate([jnp.concatenate([t0["w"], t1["w"]], axis=-1),
                                jnp.concatenate([t0["q"], t1["q"]], axis=-1)], axis=1),
            a=jnp.concatenate([t0["a"], t1["a"]], axis=-1),
            k=jnp.concatenate([t0["k"], t1["k"]], axis=-1),
            gl=jnp.concatenate([jnp.broadcast_to(t0["gl"], (n, 1, D)),
                                jnp.broadcast_to(t1["gl"], (n, 1, D))], axis=-1))

    def block_diag(x):
        first = lax.broadcasted_iota(jnp.int32, x.shape, 1) < D
        zero = jnp.zeros_like(x)
        return jnp.concatenate([jnp.where(first, x, zero), jnp.where(first, zero, x)], axis=0)

    n_pairs = GDN_PAIR // 2
    terms = [pair_terms(p) for p in range(n_pairs)]
    S = [state_ref[:, p * 2 * D:(p + 1) * 2 * D] for p in range(n_pairs)]
    for c in range(n):
        for p in range(n_pairs):
            tm = terms[p]
            r1 = jnp.dot(tm["wq"][c], block_diag(S[p].astype(BF16)), preferred_element_type=F32)
            v_b = (tm["u"][c] - r1[0:CHUNK]).astype(BF16)
            o_c = r1[CHUNK:2 * CHUNK] + jnp.dot(tm["a"][c], block_diag(v_b),
                                                preferred_element_type=F32)
            kv = lax.dot_general(tm["k"][c], v_b, _TN, preferred_element_type=F32)
            S[p] = S[p] * tm["gl"][c] + jnp.concatenate([kv[0:D, 0:D], kv[D:2 * D, D:2 * D]],
                                                         axis=1)
            obuf[c * CHUNK:(c + 1) * CHUNK, p * 2 * D:(p + 1) * 2 * D] = o_c
    for p in range(n_pairs):
        state_ref[:, p * 2 * D:(p + 1) * 2 * D] = S[p]

    for gg in range(GDN_PAIR):
        cols = slice(gg * D, (gg + 1) * D)
        o = obuf[:, cols]
        ms = jnp.mean(o * o, axis=-1, keepdims=True)
        z = z_ref[:, cols].astype(F32)
        o = o * lax.rsqrt(ms + RMS_EPS) * gain_ref[...] * (z * jax.nn.sigmoid(z))
        o_ref[:, cols] = o.astype(o_ref.dtype)


def _gdn(proj, proj_z, conv_w, gates_c, gates_r, out_gain, batch, seq_len, rows=512):
    T = proj.shape[0]
    rows = min(rows, seq_len)
    nt = seq_len // rows
    n = rows // CHUNK
    pairs = GDN_HEADS // GDN_PAIR
    W = GDN_PAIR * HEAD_DIM

    def col(base):
        return pl.BlockSpec((rows, W), lambda b, hp, t: (b * nt + t, base // GDN_PAIR + hp))

    def wcol(base):
        return pl.BlockSpec((GDN_CONV, W), lambda b, hp, t: (0, base // GDN_PAIR + hp))

    return pl.pallas_call(
        functools.partial(_gdn_kernel, rows=rows),
        out_shape=jax.ShapeDtypeStruct((T, GDN_WIDTH), BF16),
        grid=(batch, pairs, nt),
        in_specs=[col(_GQ), col(_GK), col(_GV), col(_GZ),
                  wcol(0), wcol(GDN_HEADS), wcol(2 * GDN_HEADS),
                  pl.BlockSpec((rows, GATE_LANES), lambda b, hp, t: (b * nt + t, 0)),
                  pl.BlockSpec((n, GATE_LANES, CHUNK), lambda b, hp, t: (b * nt + t, 0, 0)),
                  pl.BlockSpec((1, HEAD_DIM), lambda b, hp, t: (0, 0))],
        out_specs=pl.BlockSpec((rows, W), lambda b, hp, t: (b * nt + t, hp)),
        scratch_shapes=[pltpu.VMEM((_HALO + rows, W), F32),
                        pltpu.VMEM((_HALO + rows, W), F32),
                        pltpu.VMEM((_HALO + rows, W), F32),
                        pltpu.VMEM((HEAD_DIM, W), F32),
                        pltpu.VMEM((rows, W), F32)],
        compiler_params=_params(("parallel", "parallel", "arbitrary")),
        name="gdn",
    )(proj, proj, proj, proj_z, conv_w, conv_w, conv_w, gates_c, gates_r,
      out_gain.reshape(1, HEAD_DIM))


_GATE_F_ROWS, _GATE_AB_ROWS = 16, 64


def _gate_rows_kernel(f_ref, ab_ref, *o_refs, depth):
    kt = f_ref.shape[1] // depth
    row = lax.broadcasted_iota(jnp.int32, (GATE_LANES, HEAD_DIM), 0)
    for l in range(depth):
        f = pltpu.einshape("nkl->knl", f_ref[:, pl.ds(l, kt, stride=depth), :])
        ab = pltpu.einshape("nkl->knl", ab_ref[:, pl.ds(l, kt, stride=depth), :])
        for k in range(kt):
            fk = jnp.concatenate([f[k], jnp.zeros((GATE_LANES - _GATE_F_ROWS, HEAD_DIM), F32)], axis=0)
            abk = jnp.concatenate([ab[k], jnp.zeros((GATE_LANES - _GATE_AB_ROWS, HEAD_DIM), F32)],
                                  axis=0)
            w = jnp.where(row < FOX_HEADS, fk,
                          jnp.where(row < FOX_HEADS + 2 * GDN_HEADS, abk, 0.0))
            o_refs[l][:, k * HEAD_DIM:(k + 1) * HEAD_DIM] = w.astype(BF16)


def _small_gate_weights_t(w_view, depth):
    p, rows, lanes = w_view.shape
    d = rows // depth * lanes
    f_col, ab_col = _A_WIDTH, _B_COL + _B_WIDTH
    assert f_col % _GATE_F_ROWS == 0 and ab_col % _GATE_AB_ROWS == FOX_HEADS
    return pl.pallas_call(
        functools.partial(_gate_rows_kernel, depth=depth),
        out_shape=[jax.ShapeDtypeStruct((GATE_LANES, d), BF16)] * depth,
        grid=(1,),
        in_specs=[pl.BlockSpec((_GATE_F_ROWS, rows, lanes), lambda i: (f_col // _GATE_F_ROWS, 0, 0)),
                  pl.BlockSpec((_GATE_AB_ROWS, rows, lanes), lambda i: (ab_col // _GATE_AB_ROWS, 0, 0))],
        out_specs=[pl.BlockSpec((GATE_LANES, d), lambda i: (0, 0))] * depth,
        compiler_params=_params(("arbitrary",)),
        name="gate_rows",
    )(w_view, w_view)


def _gate_columns(fox_forget_bias, gdn_a_log, gdn_dt_bias):
    pad = GATE_LANES - FOX_HEADS - GDN_HEADS
    bias = jnp.concatenate([fox_forget_bias, gdn_dt_bias, jnp.zeros((pad,), F32)])
    alog = jnp.concatenate([jnp.zeros((FOX_HEADS,), F32), gdn_a_log, jnp.zeros((pad,), F32)])
    return bias.reshape(GATE_LANES, 1), alog.reshape(GATE_LANES, 1)


def _layer(x, u, batch, seq_len, layer, w_in, fox_forget_bias, fox_out_norm, sc_conv_w, gdn_conv_w,
           gdn_a_log, gdn_dt_bias, gdn_out_norm, w_out, mix_post_norm, ffn_pre_norm,
           w_gate, w_up, w_down, ffn_post_norm, next_pre_norm):
    T, D = x.shape
    wa_t, wb_t, wc_t, w_small_t = w_in
    bias_col, alog_col = _gate_columns(fox_forget_bias, gdn_a_log, gdn_dt_bias)

    q_scale = jnp.concatenate([jnp.full((FOX_WIDTH,), FOX_Q_SCALE, F32),
                               jnp.ones((_A_WIDTH - FOX_WIDTH,), F32)]).reshape(1, _A_WIDTH)
    proj_a, w_out_b = _matmul(u, wa_t, BF16, tm=1024, tn=768, name="in_proj_a", w_is_t=True,
                              col_scale=q_scale, carried=_CarriedCast(w_out, layer))
    proj_b, wg = _matmul(u, wb_t, BF16, tm=1024, tn=768, name="in_proj_b", w_is_t=True,
                         carried=_CarriedCast(w_gate, layer))
    proj_c, = _matmul(u, wc_t, BF16, tm=1024, tn=768, name="in_proj_c", w_is_t=True)
    gates_t = _gates(u, w_small_t, bias_col, alog_col, seq_len)
    gates_c = gates_t.T
    gates_r = gates_t.reshape(GATE_LANES, T // CHUNK, CHUNK).transpose(1, 0, 2)

    fox_out = _fox_attention(proj_a, gates_c, fox_out_norm, batch, seq_len)
    sc_out = _short_conv(proj_b, sc_conv_w, seq_len)
    gdn_out = _gdn(proj_b, proj_c, gdn_conv_w, gates_c, gates_r, gdn_out_norm, batch, seq_len)

    y, wu = _out_proj(fox_out, sc_out, gdn_out,
                      w_out_b[:FOX_WIDTH], w_out_b[FOX_WIDTH:FOX_WIDTH + SC_WIDTH],
                      w_out_b[FOX_WIDTH + SC_WIDTH:], _CarriedCast(w_up, layer))
    h, v = _norm_residual(y, x, mix_post_norm, ffn_pre_norm)

    act, wd = _ffn_up(v, wg, wu, _CarriedCast(w_down, layer))
    y2, = _matmul(act, wd, BF16, tm=512, tn=512, name="ffn_down")
    return _norm_residual(y2, h, ffn_post_norm, next_pre_norm)


def kernel(x, mix_pre_norm, w_in, fox_forget_bias, fox_out_norm, sc_conv_w, gdn_conv_w, gdn_a_log,
           gdn_dt_bias, gdn_out_norm, w_out, mix_post_norm, ffn_pre_norm, w_gate, w_up, w_down,
           ffn_post_norm):
    B, S, D = x.shape
    depth = w_in.shape[0]
    h = x.reshape(B * S, D)
    u = _norm_cast(h, mix_pre_norm[0])
    w_view = _column_major_view(w_in)
    windows = [_window_t(w_view, depth, col, width)
               for col, width in ((_A_COL, _A_WIDTH), (_B_COL, _B_WIDTH), (_C_COL, _C_WIDTH))]
    windows.append(_small_gate_weights_t(w_view, depth))
    for l in range(depth):
        nxt = mix_pre_norm[l + 1] if l + 1 < depth else None
        w_in_l = tuple(win[l] for win in windows)
        h, u = _layer(h, u, B, S, l, w_in_l, fox_forget_bias[l], fox_out_norm[l], sc_conv_w[l],
                      gdn_conv_w[l], gdn_a_log[l], gdn_dt_bias[l], gdn_out_norm[l], w_out,
                      mix_post_norm[l], ffn_pre_norm[l], w_gate, w_up, w_down,
                      ffn_post_norm[l], nxt)
    return h.reshape(B, S, D)
```

```python
import functools
import math
from typing import NamedTuple

import jax
import jax.numpy as jnp
from jax import lax
from jax.experimental import pallas as pl
from jax.experimental.pallas import tpu as pltpu

F32 = jnp.float32
BF16 = jnp.bfloat16

HEAD_DIM = 128
FOX_HEADS = 12
FOX_WIDTH = FOX_HEADS * HEAD_DIM
SC_WIDTH = 8 * HEAD_DIM
SC_KERNEL = 3
GDN_HEADS = 12
GDN_WIDTH = GDN_HEADS * HEAD_DIM
GDN_CONV = 4
CHUNK = 64
RMS_EPS = 1e-6
GATE_LANES = 128

VMEM_LIMIT_BYTES = 56 * 1024 * 1024

_FQ, _FK, _FV = 0, 12, 24
_SB, _SC, _SH = 0, 8, 16
_GQ, _GK, _GV = 24, 36, 48
_GZ = 0
_A_COL, _A_WIDTH = 0, 3 * FOX_WIDTH
_B_COL, _B_WIDTH = 3 * FOX_WIDTH + FOX_HEADS, 3 * SC_WIDTH + 3 * GDN_WIDTH
_C_COL, _C_WIDTH = _B_COL + _B_WIDTH + 2 * GDN_HEADS, GDN_WIDTH

_NT = (((1,), (1,)), ((), ()))
_TN = (((0,), (0,)), ((), ()))
_HI = lax.Precision.HIGHEST


def _params(sem):
    return pltpu.CompilerParams(dimension_semantics=sem, vmem_limit_bytes=VMEM_LIMIT_BYTES)


def _norm_cast_kernel(x_ref, g_ref, o_ref):
    x = x_ref[...]
    ms = jnp.mean(x * x, axis=-1, keepdims=True)
    o_ref[...] = (x * lax.rsqrt(ms + RMS_EPS) * g_ref[...]).astype(o_ref.dtype)


def _norm_cast(x, gain, tm=256):
    T, D = x.shape
    tm = min(tm, T)
    return pl.pallas_call(
        _norm_cast_kernel,
        out_shape=jax.ShapeDtypeStruct((T, D), BF16),
        grid=(T // tm,),
        in_specs=[pl.BlockSpec((tm, D), lambda i: (i, 0)),
                  pl.BlockSpec((1, D), lambda i: (0, 0))],
        out_specs=pl.BlockSpec((tm, D), lambda i: (i, 0)),
        compiler_params=_params(("parallel",)),
        name="norm_cast",
    )(x, gain.reshape(1, D))


def _norm_residual_kernel(y_ref, x_ref, g_ref, gn_ref, h_ref, *maybe_u_ref):
    y = y_ref[...].astype(F32)
    ms = jnp.mean(y * y, axis=-1, keepdims=True)
    h = x_ref[...] + y * lax.rsqrt(ms + RMS_EPS) * g_ref[...]
    h_ref[...] = h
    if maybe_u_ref:
        ms2 = jnp.mean(h * h, axis=-1, keepdims=True)
        maybe_u_ref[0][...] = (h * lax.rsqrt(ms2 + RMS_EPS) * gn_ref[...]).astype(BF16)


def _norm_residual(y, x, gain, next_gain, tm=256):
    T, D = x.shape
    tm = min(tm, T)
    emit_next = next_gain is not None
    gn = (next_gain if emit_next else gain).reshape(1, D)
    row = pl.BlockSpec((tm, D), lambda i: (i, 0))
    vec = pl.BlockSpec((1, D), lambda i: (0, 0))
    out_shape = [jax.ShapeDtypeStruct((T, D), F32)]
    out_specs = [row]
    if emit_next:
        out_shape.append(jax.ShapeDtypeStruct((T, D), BF16))
        out_specs.append(row)
    res = pl.pallas_call(
        _norm_residual_kernel,
        out_shape=out_shape,
        grid=(T // tm,),
        in_specs=[row, row, vec, vec],
        out_specs=out_specs,
        compiler_params=_params(("parallel",)),
        name="norm_residual",
    )(y, x, gain.reshape(1, D), gn)
    return (res[0], res[1]) if emit_next else (res[0], None)


def _window_t_kernel(a_ref, b_ref, *o_refs, off, depth):
    tn = o_refs[0].shape[0]
    kt = a_ref.shape[1] // depth
    for l in range(depth):
        x = a_ref[:, pl.ds(l, kt, stride=depth), :]
        if off:
            x = jnp.concatenate([x, b_ref[:, pl.ds(l, kt, stride=depth), :]], axis=0)[off:off + tn]
        xt = pltpu.einshape("nkl->knl", x)
        for k in range(kt):
            o_refs[l][:, k * HEAD_DIM:(k + 1) * HEAD_DIM] = xt[k].astype(BF16)


def _column_major_view(w_in):
    depth, d, p = w_in.shape
    kt = d // HEAD_DIM
    v = jnp.transpose(w_in, (2, 0, 1)).reshape(p, depth, kt, HEAD_DIM)
    return jnp.transpose(v, (0, 2, 1, 3)).reshape(p, kt * depth, HEAD_DIM)


def _window_t(w_view, depth, src_col, width, tn=256, tail=64):
    p, rows, lanes = w_view.shape
    d = rows // depth * lanes
    base, off = src_col // tn, src_col % tn
    assert width % tn == 0 and tn % tail == 0 and off <= tail
    return pl.pallas_call(
        functools.partial(_window_t_kernel, off=off, depth=depth),
        out_shape=[jax.ShapeDtypeStruct((width, d), BF16)] * depth,
        grid=(width // tn,),
        in_specs=[pl.BlockSpec((tn, rows, lanes), lambda j: (base + j, 0, 0)),
                  pl.BlockSpec((tail, rows, lanes), lambda j: ((base + j + 1) * (tn // tail), 0, 0))],
        out_specs=[pl.BlockSpec((tn, d), lambda j: (j, 0))] * depth,
        compiler_params=_params(("parallel",)),
        name="window_t",
    )(w_view, w_view)


class _CarriedCast(NamedTuple):
    w_stack: jax.Array
    layer: int


_BF16_SUBLANES = 16


def _carried_tile_rows(rows, steps):
    for t in range(_BF16_SUBLANES, rows, _BF16_SUBLANES):
        if rows % t == 0 and rows // t <= steps:
            return t
    return rows


def _call(kernel, carried, grid, in_specs, out_specs, out_shape, args, name):
    n_in, n_out = len(in_specs), len(out_specs)
    body = kernel
    if carried is not None:
        _, rows, cols = carried.w_stack.shape
        tile_rows = _carried_tile_rows(rows, math.prod(grid))
        n_tiles = rows // tile_rows

        def tile(*g):
            step = 0
            for size, idx in zip(grid, g):
                step = step * size + idx
            return jnp.minimum(step, n_tiles - 1)

        in_specs = in_specs + [pl.BlockSpec((1, tile_rows, cols),
                                            lambda *g: (carried.layer, tile(*g), 0))]
        out_specs = out_specs + [pl.BlockSpec((tile_rows, cols), lambda *g: (tile(*g), 0))]
        out_shape = out_shape + [jax.ShapeDtypeStruct((rows, cols), BF16)]
        args = args + [carried.w_stack]

        def body(*refs):
            src, dst = refs[n_in], refs[n_in + 1 + n_out]
            dst[...] = src[0].astype(dst.dtype)
            kernel(*refs[:n_in], *refs[n_in + 1:n_in + 1 + n_out], *refs[n_in + 2 + n_out:])

    return pl.pallas_call(
        body, out_shape=out_shape, grid=grid, in_specs=in_specs, out_specs=out_specs,
        compiler_params=_params(("arbitrary",) * len(grid)), name=name)(*args)


def _matmul_kernel(x_ref, w_ref, *rest, scaled, w_is_t):
    if w_is_t:
        acc = lax.dot_general(x_ref[...], w_ref[...], _NT, preferred_element_type=F32)
    else:
        acc = jnp.dot(x_ref[...], w_ref[...], preferred_element_type=F32)
    if scaled:
        s_ref, o_ref = rest
        acc = acc * s_ref[...]
    else:
        o_ref, = rest
    o_ref[...] = acc.astype(o_ref.dtype)


def _matmul(x, w, out_dtype, tm, tn, name, col_scale=None, carried=None, w_is_t=False):
    M, K = x.shape
    N = w.shape[0] if w_is_t else w.shape[1]
    tm, tn = min(tm, M), min(tn, N)
    scaled = col_scale is not None
    in_specs = [pl.BlockSpec((tm, K), lambda i, j: (i, 0)),
                pl.BlockSpec((tn, K), lambda i, j: (j, 0)) if w_is_t
                else pl.BlockSpec((K, tn), lambda i, j: (0, j))]
    args = [x, w]
    if scaled:
        in_specs.append(pl.BlockSpec((1, tn), lambda i, j: (0, j)))
        args.append(col_scale)
    return _call(functools.partial(_matmul_kernel, scaled=scaled, w_is_t=w_is_t), carried,
                 (M // tm, N // tn), in_specs, [pl.BlockSpec((tm, tn), lambda i, j: (i, j))],
                 [jax.ShapeDtypeStruct((M, N), out_dtype)], args, name)


def _out_proj_kernel(a1_ref, a2_ref, a3_ref, w1_ref, w2_ref, w3_ref, o_ref):
    acc = jnp.dot(a1_ref[...], w1_ref[...], preferred_element_type=F32)
    acc += jnp.dot(a2_ref[...], w2_ref[...], preferred_element_type=F32)
    acc += jnp.dot(a3_ref[...], w3_ref[...], preferred_element_type=F32)
    o_ref[...] = acc.astype(o_ref.dtype)


def _out_proj(a1, a2, a3, w1, w2, w3, carried, tm=1024, tn=1024):
    T = a1.shape[0]
    N = w1.shape[1]
    tm, tn = min(tm, T), min(tn, N)

    def lhs(a):
        return pl.BlockSpec((tm, a.shape[1]), lambda i, j: (i, 0))

    def rhs(w):
        return pl.BlockSpec((w.shape[0], tn), lambda i, j: (0, j))

    return _call(_out_proj_kernel, carried, (T // tm, N // tn),
                 [lhs(a1), lhs(a2), lhs(a3), rhs(w1), rhs(w2), rhs(w3)],
                 [pl.BlockSpec((tm, tn), lambda i, j: (i, j))],
                 [jax.ShapeDtypeStruct((T, N), BF16)], [a1, a2, a3, w1, w2, w3], "out_proj")


def _ffn_up_kernel(x_ref, wg_ref, wu_ref, o_ref):
    x = x_ref[...]
    g = jnp.dot(x, wg_ref[...], preferred_element_type=F32)
    u = jnp.dot(x, wu_ref[...], preferred_element_type=F32)
    o_ref[...] = (g * jax.nn.sigmoid(g) * u).astype(o_ref.dtype)


def _ffn_up(x, wg, wu, carried, tm=1024, tn=512):
    T, K = x.shape
    N = wg.shape[1]
    tm = min(tm, T)
    return _call(_ffn_up_kernel, carried, (T // tm, pl.cdiv(N, tn)),
                 [pl.BlockSpec((tm, K), lambda i, j: (i, 0)),
                  pl.BlockSpec((K, tn), lambda i, j: (0, j)),
                  pl.BlockSpec((K, tn), lambda i, j: (0, j))],
                 [pl.BlockSpec((tm, tn), lambda i, j: (i, j))],
                 [jax.ShapeDtypeStruct((T, N), BF16)], [x, wg, wu], "ffn_up")


def _gates_kernel(w_ref, u_ref, bias_ref, alog_ref, gt_ref, carry_ref, *, steps_per_seq, ts):
    t = pl.program_id(0)

    @pl.when(t % steps_per_seq == 0)
    def _():
        carry_ref[...] = jnp.zeros_like(carry_ref)

    z = lax.dot_general(w_ref[...], u_ref[...], _NT, preferred_element_type=F32) + bias_ref[...]
    row = lax.broadcasted_iota(jnp.int32, z.shape, 0)
    tail = jnp.log1p(jnp.exp(-jnp.abs(z)))
    log_sig = jnp.minimum(z, 0.0) - tail
    softplus = jnp.maximum(z, 0.0) + tail
    sig = 1.0 / (1.0 + jnp.exp(-z))
    decay = -jnp.exp(alog_ref[...]) * softplus
    val = jnp.where(row < FOX_HEADS, log_sig, jnp.where(row < FOX_HEADS + GDN_HEADS, decay, sig))

    src = lax.broadcasted_iota(jnp.int32, (ts, ts), 0)
    dst = lax.broadcasted_iota(jnp.int32, (ts, ts), 1)
    upper = src <= dst
    same_chunk = (src // CHUNK) == (dst // CHUNK)
    cum_all = jnp.dot(val, upper.astype(F32), precision=_HI, preferred_element_type=F32)
    cum_chunk = jnp.dot(val, (upper & same_chunk).astype(F32), precision=_HI,
                        preferred_element_type=F32)
    cum_all = cum_all + carry_ref[...]
    carry_ref[...] = cum_all[:, ts - 1:ts]
    gt_ref[...] = jnp.where(row < FOX_HEADS, cum_all,
                            jnp.where(row < FOX_HEADS + GDN_HEADS, cum_chunk, val))


def _gates(u, w_small_t, bias_col, alog_col, seq_len, ts=512):
    T, D = u.shape
    ts = min(ts, seq_len)
    return pl.pallas_call(
        functools.partial(_gates_kernel, steps_per_seq=seq_len // ts, ts=ts),
        out_shape=jax.ShapeDtypeStruct((GATE_LANES, T), F32),
        grid=(T // ts,),
        in_specs=[pl.BlockSpec((GATE_LANES, D), lambda t: (0, 0)),
                  pl.BlockSpec((ts, D), lambda t: (t, 0)),
                  pl.BlockSpec((GATE_LANES, 1), lambda t: (0, 0)),
                  pl.BlockSpec((GATE_LANES, 1), lambda t: (0, 0))],
        out_specs=pl.BlockSpec((GATE_LANES, ts), lambda t: (0, t)),
        scratch_shapes=[pltpu.VMEM((GATE_LANES, 1), F32)],
        compiler_params=_params(("arbitrary",)),
        name="gates",
    )(w_small_t, u, bias_col, alog_col)


FOX_Q_SCALE = (HEAD_DIM ** -0.5) * math.log2(math.e)
FOX_PAIR = 2
_FOX_BUILD_ROWS = 512


def _fox_kernel(q_ref, qn_ref, k_ref, v_ref, gc_ref, g_ref, o_ref, kaug, vaug, m_ref, acc_ref,
                s0_ref, sa_ref, sb_ref, *, tq, tk, seq_len, pairs):
    D = HEAD_DIM
    pair = pl.program_id(0) % pairs
    qi = pl.program_id(1)

    @pl.when(qi == 0)
    def _build():
        ri = lax.broadcasted_iota(jnp.int32, (3 * D, D), 0)
        ci = lax.broadcasted_iota(jnp.int32, (3 * D, D), 1)
        ones = jnp.ones((_FOX_BUILD_ROWS, D), BF16)

        def chunk(i, carry):
            r0 = pl.multiple_of(i * _FOX_BUILD_ROWS, _FOX_BUILD_ROWS)
            rows = pl.ds(r0, _FOX_BUILD_ROWS)
            g = gc_ref[rows, :] * (-math.log2(math.e))
            hi = g.astype(BF16)
            r1 = g - hi.astype(F32)
            mid = r1.astype(BF16)
            lo = (r1 - mid.astype(F32)).astype(BF16)
            pieces = jnp.concatenate([hi, mid, lo], axis=1)
            for gg in range(FOX_PAIR):
                h = pair * FOX_PAIR + gg
                sel = (((ri == h) & (ci == 0)) | ((ri == D + h) & (ci == 1))
                       | ((ri == 2 * D + h) & (ci == 2)))
                aug = jnp.dot(pieces, sel.astype(BF16), preferred_element_type=F32)
                kaug[gg, rows, 0:D] = k_ref[rows, gg * D:(gg + 1) * D]
                kaug[gg, rows, D:2 * D] = aug.astype(BF16)
                vaug[gg, rows, 0:D] = v_ref[rows, gg * D:(gg + 1) * D]
                vaug[gg, rows, D:2 * D] = ones
            return carry

        lax.fori_loop(0, seq_len // _FOX_BUILD_ROWS, chunk, 0)

    lane = lax.broadcasted_iota(jnp.int32, (tq, D), 1)
    ones3 = jnp.where(lane < 3, 1.0, 0.0).astype(BF16)
    m_ref[...] = jnp.full_like(m_ref, -jnp.inf)
    acc_ref[...] = jnp.zeros_like(acc_ref)

    def scores(kj, slot_ref, queries=q_ref):
        k0 = pl.multiple_of(kj * tk, tk)
        for gg in range(FOX_PAIR):
            q_aug = jnp.concatenate([queries[:, gg * D:(gg + 1) * D], ones3], axis=1)
            slot_ref[gg] = lax.dot_general(q_aug, kaug[gg, pl.ds(k0, tk), :], _NT,
                                           preferred_element_type=F32)

    def accumulate(kj, slot_ref, masked=False):
        k0 = pl.multiple_of(kj * tk, tk)
        for gg in range(FOX_PAIR):
            s = slot_ref[gg]
            if masked:
                r = qi * tq + lax.broadcasted_iota(jnp.int32, s.shape, 0)
                c = kj * tk + lax.broadcasted_iota(jnp.int32, s.shape, 1)
                s = jnp.where(c <= r, s, -jnp.inf)
            m_prev = m_ref[gg]
            m_new = jnp.maximum(m_prev, jnp.max(s, axis=-1, keepdims=True))
            alpha = jnp.exp2(m_prev - m_new)
            p = jnp.concatenate(
                [jnp.exp2(s[:, j * D:(j + 1) * D] - m_new) for j in range(tk // D)],
                axis=1).astype(BF16)
            pv = jnp.dot(p, vaug[gg, pl.ds(k0, tk), :], preferred_element_type=F32)
            acc_ref[gg] = jnp.concatenate([alpha, alpha], axis=1) * acc_ref[gg] + pv
            m_ref[gg] = m_new

    n_full = (qi * tq) // tk
    first_prefetching_step = tk // tq

    @pl.when(qi <= first_prefetching_step)
    def _():
        scores(0, s0_ref)

    @pl.when(n_full == 0)
    def _():
        accumulate(0, s0_ref, masked=True)

    @pl.when(n_full >= 1)
    def _():
        scores(1, sa_ref)
        accumulate(0, s0_ref)

    def body(i, carry):
        scores(2 * i + 2, sb_ref)
        accumulate(2 * i + 1, sa_ref)
        scores(2 * i + 3, sa_ref)
        accumulate(2 * i + 2, sb_ref)
        return carry

    lax.fori_loop(0, (n_full - 1) // 2, body, 0)

    @pl.when((n_full >= 1) & (n_full % 2 == 1))
    def _():
        scores(0, s0_ref, qn_ref)
        accumulate(n_full, sa_ref, masked=True)

    @pl.when((n_full >= 1) & (n_full % 2 == 0))
    def _():
        scores(n_full, sb_ref)
        accumulate(n_full - 1, sa_ref)
        scores(0, s0_ref, qn_ref)
        accumulate(n_full, sb_ref, masked=True)

    for gg in range(FOX_PAIR):
        acc = acc_ref[gg]
        out = acc[:, 0:D] / acc[:, D:2 * D]
        ms = jnp.mean(out * out, axis=-1, keepdims=True)
        o_ref[:, gg * D:(gg + 1) * D] = (out * lax.rsqrt(ms + RMS_EPS) * g_ref[...]).astype(o_ref.dtype)


def _fox_attention(proj, gates_c, out_gain, batch, seq_len, tq=512, tk=1024):
    T = proj.shape[0]
    tq = min(tq, seq_len)
    tk = min(tk, seq_len)
    assert tk % tq == 0 and seq_len % tk == 0
    nq = seq_len // tq
    pairs = FOX_HEADS // FOX_PAIR
    W = FOX_PAIR * HEAD_DIM

    once = pl.Buffered(1)

    def rows_spec(base):
        return pl.BlockSpec((seq_len, W), lambda bp, qi: (bp // pairs, base // FOX_PAIR + bp % pairs),
                            pipeline_mode=once)

    return pl.pallas_call(
        functools.partial(_fox_kernel, tq=tq, tk=tk, seq_len=seq_len, pairs=pairs),
        out_shape=jax.ShapeDtypeStruct((T, FOX_WIDTH), BF16),
        grid=(batch * pairs, nq),
        in_specs=[pl.BlockSpec((tq, W), lambda bp, qi: ((bp // pairs) * nq + qi,
                                                         _FQ // FOX_PAIR + bp % pairs)),
                  pl.BlockSpec((tq, W), lambda bp, qi: ((bp // pairs) * nq + jnp.minimum(qi + 1, nq - 1),
                                                         _FQ // FOX_PAIR + bp % pairs)),
                  rows_spec(_FK), rows_spec(_FV),
                  pl.BlockSpec((seq_len, GATE_LANES), lambda bp, qi: (bp // pairs, 0),
                               pipeline_mode=once),
                  pl.BlockSpec((1, HEAD_DIM), lambda bp, qi: (0, 0))],
        out_specs=pl.BlockSpec((tq, W), lambda bp, qi: ((bp // pairs) * nq + qi, bp % pairs)),
        scratch_shapes=[pltpu.VMEM((FOX_PAIR, seq_len, 2 * HEAD_DIM), BF16),
                        pltpu.VMEM((FOX_PAIR, seq_len, 2 * HEAD_DIM), BF16),
                        pltpu.VMEM((FOX_PAIR, tq, HEAD_DIM), F32),
                        pltpu.VMEM((FOX_PAIR, tq, 2 * HEAD_DIM), F32),
                        pltpu.VMEM((FOX_PAIR, tq, tk), F32),
                        pltpu.VMEM((FOX_PAIR, tq, tk), F32),
                        pltpu.VMEM((FOX_PAIR, tq, tk), F32)],
        compiler_params=_params(("parallel", "arbitrary")),
        name="fox_attention",
    )(proj, proj, proj, proj, gates_c, out_gain.reshape(1, HEAD_DIM))


_HALO = 8


def _short_conv_kernel(b_ref, c_ref, h_ref, w_ref, o_ref, buf_ref, *, steps_per_seq, ts):
    t = pl.program_id(1)

    @pl.when(t % steps_per_seq == 0)
    def _():
        buf_ref[0:_HALO, :] = jnp.zeros((_HALO, buf_ref.shape[1]), F32)

    buf_ref[_HALO:_HALO + ts, :] = c_ref[...].astype(F32) * h_ref[...].astype(F32)
    w = w_ref[...]
    y = w[0:1, :] * buf_ref[pl.ds(_HALO - 2, ts), :]
    y += w[1:2, :] * buf_ref[pl.ds(_HALO - 1, ts), :]
    y += w[2:3, :] * buf_ref[pl.ds(_HALO, ts), :]
    o_ref[...] = (b_ref[...].astype(F32) * y).astype(o_ref.dtype)
    buf_ref[0:_HALO, :] = buf_ref[ts:ts + _HALO, :]


def _short_conv(proj, conv_w, seq_len, ts=512, tc=512):
    T = proj.shape[0]
    ts = min(ts, seq_len)
    per = tc // HEAD_DIM

    def col(base):
        return pl.BlockSpec((ts, tc), lambda c, t: (t, base // per + c))

    return pl.pallas_call(
        functools.partial(_short_conv_kernel, steps_per_seq=seq_len // ts, ts=ts),
        out_shape=jax.ShapeDtypeStruct((T, SC_WIDTH), BF16),
        grid=(SC_WIDTH // tc, T // ts),
        in_specs=[col(_SB), col(_SC), col(_SH),
                  pl.BlockSpec((SC_KERNEL, tc), lambda c, t: (0, c))],
        out_specs=pl.BlockSpec((ts, tc), lambda c, t: (t, c)),
        scratch_shapes=[pltpu.VMEM((_HALO + ts, tc), F32)],
        compiler_params=_params(("parallel", "arbitrary")),
        name="short_conv",
    )(proj, proj, proj, conv_w)


def _bmm(a, b, dims):
    return lax.dot_general(a.astype(BF16), b.astype(BF16), dims, preferred_element_type=F32)


_B_NN = (((2,), (1,)), ((0,), (0,)))
_B_NT = (((2,), (2,)), ((0,), (0,)))


GDN_PAIR = 4


def _gdn_kernel(q_ref, k_ref, v_ref, z_ref, wq_ref, wk_ref, wv_ref, gc_ref, gr_ref, gain_ref,
                o_ref, qbuf, kbuf, vbuf, state_ref, obuf, *, rows):
    hp = pl.program_id(1)
    t = pl.program_id(2)
    n = rows // CHUNK
    D = HEAD_DIM

    @pl.when(t == 0)
    def _():
        zeros = jnp.zeros((_HALO, GDN_PAIR * D), F32)
        qbuf[0:_HALO, :] = zeros
        kbuf[0:_HALO, :] = zeros
        vbuf[0:_HALO, :] = zeros
        state_ref[...] = jnp.zeros_like(state_ref)

    def conv_silu(x_ref, w_ref, buf):
        buf[_HALO:_HALO + rows, :] = x_ref[...].astype(F32)
        w = w_ref[...]
        y = w[0:1, :] * buf[pl.ds(_HALO - 3, rows), :]
        y += w[1:2, :] * buf[pl.ds(_HALO - 2, rows), :]
        y += w[2:3, :] * buf[pl.ds(_HALO - 1, rows), :]
        y += w[3:4, :] * buf[pl.ds(_HALO, rows), :]
        buf[0:_HALO, :] = buf[rows:rows + _HALO, :]
        return y * jax.nn.sigmoid(y)

    q_all = conv_silu(q_ref, wq_ref, qbuf)
    k_all = conv_silu(k_ref, wk_ref, kbuf)
    v_all = conv_silu(v_ref, wv_ref, vbuf)

    gates = gc_ref[...]
    lane = lax.broadcasted_iota(jnp.int32, gates.shape, 1)
    ri = lax.broadcasted_iota(jnp.int32, (CHUNK, CHUNK), 0)
    ci = lax.broadcasted_iota(jnp.int32, (CHUNK, CHUNK), 1)
    tri_incl = (ci <= ri)[None]
    tri_strict = (ci < ri)[None]
    same16 = ((ri // 16) == (ci // 16))[None]
    same32 = ((ri // 32) == (ci // 32))[None]
    eye = (ri == ci).astype(F32)[None]

    def chunk_terms(gg):
        h = hp * GDN_PAIR + gg
        cols = slice(gg * D, (gg + 1) * D)
        q, k, v = q_all[:, cols], k_all[:, cols], v_all[:, cols]
        q = q * (lax.rsqrt(jnp.sum(q * q, axis=-1, keepdims=True) + RMS_EPS) * (D ** -0.5))
        k = k * lax.rsqrt(jnp.sum(k * k, axis=-1, keepdims=True) + RMS_EPS)
        g_cum = jnp.sum(jnp.where(lane == FOX_HEADS + h, gates, 0.0), axis=-1, keepdims=True)
        beta = jnp.sum(jnp.where(lane == FOX_HEADS + GDN_HEADS + h, gates, 0.0), axis=-1,
                       keepdims=True)
        q3 = q.reshape(n, CHUNK, D)
        k3 = k.reshape(n, CHUNK, D)
        v3 = v.reshape(n, CHUNK, D)
        g3 = g_cum.reshape(n, CHUNK, 1)
        b3 = beta.reshape(n, CHUNK, 1)
        g_row = gr_ref[:, pl.ds(FOX_HEADS + h, 1), :]
        g_last = g3[:, CHUNK - 1:CHUNK, :]

        decay = jnp.exp(jnp.where(tri_incl, g3 - g_row, -jnp.inf))
        kk = _bmm(k3, k3, _B_NT)
        L = jnp.where(tri_strict, b3 * kk * decay, 0.0)

        P = jnp.where(same16, L, 0.0)
        X = eye - P
        P2 = _bmm(P, P, _B_NN)
        X = _bmm(X, eye + P2, _B_NN)
        P4 = _bmm(P2, P2, _B_NN)
        X = _bmm(X, eye + P4, _B_NN)
        P8 = _bmm(P4, P4, _B_NN)
        X = _bmm(X, eye + P8, _B_NN)
        O32 = jnp.where(same32 & jnp.logical_not(same16), L, 0.0)
        X = X - _bmm(_bmm(X, O32, _B_NN), X, _B_NN)
        O64 = jnp.where(same32, 0.0, L)
        X = X - _bmm(_bmm(X, O64, _B_NN), X, _B_NN)

        e3 = jnp.exp(g3)
        rhs = jnp.concatenate([v3 * b3, k3 * (b3 * e3)], axis=-1)
        sol = _bmm(X, rhs, _B_NN)
        attn = jnp.where(tri_incl, _bmm(q3, k3, _B_NT) * decay, 0.0)
        return dict(u=sol[:, :, :D], w=sol[:, :, D:].astype(BF16), q=(q3 * e3).astype(BF16),
                    a=attn.astype(BF16), k=(k3 * jnp.exp(g_last - g3)).astype(BF16),
                    gl=jnp.exp(g_last))

    def pair_terms(p):
        t0, t1 = chunk_terms(2 * p), chunk_terms(2 * p + 1)
        return dict(
            u=jnp.concatenate([t0["u"], t1["u"]], axis=-1),
            wq=jnp.concatenate([jnp.concatenate([t0["w"], t1["w"]], axis=-1),
                                jnp.concatenate([t0["q"], t1["q"]], axis=-1)], axis=1),
            a=jnp.concatenate([t0["a"], t1["a"]], axis=-1),
            k=jnp.concatenate([t0["k"], t1["k"]], axis=-1),
            gl=jnp.concatenate([jnp.broadcast_to(t0["gl"], (n, 1, D)),
                                jnp.broadcast_to(t1["gl"], (n, 1, D))], axis=-1))

    def block_diag(x):
        first = lax.broadcasted_iota(jnp.int32, x.shape, 1) < D
        zero = jnp.zeros_like(x)
        return jnp.concatenate([jnp.where(first, x, zero), jnp.where(first, zero, x)], axis=0)

    n_pairs = GDN_PAIR // 2
    terms = [pair_terms(p) for p in range(n_pairs)]
    S = [state_ref[:, p * 2 * D:(p + 1) * 2 * D] for p in range(n_pairs)]
    for c in range(n):
        for p in range(n_pairs):
            tm = terms[p]
            r1 = jnp.dot(tm["wq"][c], block_diag(S[p].astype(BF16)), preferred_element_type=F32)
            v_b = (tm["u"][c] - r1[0:CHUNK]).astype(BF16)
            o_c = r1[CHUNK:2 * CHUNK] + jnp.dot(tm["a"][c], block_diag(v_b),
                                                preferred_element_type=F32)
            kv = lax.dot_general(tm["k"][c], v_b, _TN, preferred_element_type=F32)
            S[p] = S[p] * tm["gl"][c] + jnp.concatenate([kv[0:D, 0:D], kv[D:2 * D, D:2 * D]],
                                                         axis=1)
            obuf[c * CHUNK:(c + 1) * CHUNK, p * 2 * D:(p + 1) * 2 * D] = o_c
    for p in range(n_pairs):
        state_ref[:, p * 2 * D:(p + 1) * 2 * D] = S[p]

    for gg in range(GDN_PAIR):
        cols = slice(gg * D, (gg + 1) * D)
        o = obuf[:, cols]
        ms = jnp.mean(o * o, axis=-1, keepdims=True)
        z = z_ref[:, cols].astype(F32)
        o = o * lax.rsqrt(ms + RMS_EPS) * gain_ref[...] * (z * jax.nn.sigmoid(z))
        o_ref[:, cols] = o.astype(o_ref.dtype)


def _gdn(proj, proj_z, conv_w, gates_c, gates_r, out_gain, batch, seq_len, rows=1024):
    T = proj.shape[0]
    rows = min(rows, seq_len)
    nt = seq_len // rows
    n = rows // CHUNK
    pairs = GDN_HEADS // GDN_PAIR
    W = GDN_PAIR * HEAD_DIM

    def col(base):
        return pl.BlockSpec((rows, W), lambda b, hp, t: (b * nt + t, base // GDN_PAIR + hp))

    def wcol(base):
        return pl.BlockSpec((GDN_CONV, W), lambda b, hp, t: (0, base // GDN_PAIR + hp))

    return pl.pallas_call(
        functools.partial(_gdn_kernel, rows=rows),
        out_shape=jax.ShapeDtypeStruct((T, GDN_WIDTH), BF16),
        grid=(batch, pairs, nt),
        in_specs=[col(_GQ), col(_GK), col(_GV), col(_GZ),
                  wcol(0), wcol(GDN_HEADS), wcol(2 * GDN_HEADS),
                  pl.BlockSpec((rows, GATE_LANES), lambda b, hp, t: (b * nt + t, 0)),
                  pl.BlockSpec((n, GATE_LANES, CHUNK), lambda b, hp, t: (b * nt + t, 0, 0)),
                  pl.BlockSpec((1, HEAD_DIM), lambda b, hp, t: (0, 0))],
        out_specs=pl.BlockSpec((rows, W), lambda b, hp, t: (b * nt + t, hp)),
        scratch_shapes=[pltpu.VMEM((_HALO + rows, W), F32),
                        pltpu.VMEM((_HALO + rows, W), F32),
                        pltpu.VMEM((_HALO + rows, W), F32),
                        pltpu.VMEM((HEAD_DIM, W), F32),
                        pltpu.VMEM((rows, W), F32)],
        compiler_params=_params(("parallel", "parallel", "arbitrary")),
        name="gdn",
    )(proj, proj, proj, proj_z, conv_w, conv_w, conv_w, gates_c, gates_r,
      out_gain.reshape(1, HEAD_DIM))


_GATE_F_ROWS, _GATE_AB_ROWS = 16, 64


def _gate_rows_kernel(f_ref, ab_ref, *o_refs, depth):
    kt = f_ref.shape[1] // depth
    row = lax.broadcasted_iota(jnp.int32, (GATE_LANES, HEAD_DIM), 0)
    for l in range(depth):
        f = pltpu.einshape("nkl->knl", f_ref[:, pl.ds(l, kt, stride=depth), :])
        ab = pltpu.einshape("nkl->knl", ab_ref[:, pl.ds(l, kt, stride=depth), :])
        for k in range(kt):
            fk = jnp.concatenate([f[k], jnp.zeros((GATE_LANES - _GATE_F_ROWS, HEAD_DIM), F32)], axis=0)
            abk = jnp.concatenate([ab[k], jnp.zeros((GATE_LANES - _GATE_AB_ROWS, HEAD_DIM), F32)],
                                  axis=0)
            w = jnp.where(row < FOX_HEADS, fk,
                          jnp.where(row < FOX_HEADS + 2 * GDN_HEADS, abk, 0.0))
            o_refs[l][:, k * HEAD_DIM:(k + 1) * HEAD_DIM] = w.astype(BF16)


def _small_gate_weights_t(w_view, depth):
    p, rows, lanes = w_view.shape
    d = rows // depth * lanes
    f_col, ab_col = _A_WIDTH, _B_COL + _B_WIDTH
    assert f_col % _GATE_F_ROWS == 0 and ab_col % _GATE_AB_ROWS == FOX_HEADS
    return pl.pallas_call(
        functools.partial(_gate_rows_kernel, depth=depth),
        out_shape=[jax.ShapeDtypeStruct((GATE_LANES, d), BF16)] * depth,
        grid=(1,),
        in_specs=[pl.BlockSpec((_GATE_F_ROWS, rows, lanes), lambda i: (f_col // _GATE_F_ROWS, 0, 0)),
                  pl.BlockSpec((_GATE_AB_ROWS, rows, lanes), lambda i: (ab_col // _GATE_AB_ROWS, 0, 0))],
        out_specs=[pl.BlockSpec((GATE_LANES, d), lambda i: (0, 0))] * depth,
        compiler_params=_params(("arbitrary",)),
        name="gate_rows",
    )(w_view, w_view)


def _gate_columns(fox_forget_bias, gdn_a_log, gdn_dt_bias):
    pad = GATE_LANES - FOX_HEADS - GDN_HEADS
    bias = jnp.concatenate([fox_forget_bias, gdn_dt_bias, jnp.zeros((pad,), F32)])
    alog = jnp.concatenate([jnp.zeros((FOX_HEADS,), F32), gdn_a_log, jnp.zeros((pad,), F32)])
    return bias.reshape(GATE_LANES, 1), alog.reshape(GATE_LANES, 1)


def _layer(x, u, batch, seq_len, layer, w_in, fox_forget_bias, fox_out_norm, sc_conv_w, gdn_conv_w,
           gdn_a_log, gdn_dt_bias, gdn_out_norm, w_out, mix_post_norm, ffn_pre_norm,
           w_gate, w_up, w_down, ffn_post_norm, next_pre_norm):
    T, D = x.shape
    wa_t, wb_t, wc_t, w_small_t = w_in
    bias_col, alog_col = _gate_columns(fox_forget_bias, gdn_a_log, gdn_dt_bias)

    q_scale = jnp.concatenate([jnp.full((FOX_WIDTH,), FOX_Q_SCALE, F32),
                               jnp.ones((_A_WIDTH - FOX_WIDTH,), F32)]).reshape(1, _A_WIDTH)
    proj_a, w_out_b = _matmul(u, wa_t, BF16, tm=1024, tn=768, name="in_proj_a", w_is_t=True,
                              col_scale=q_scale, carried=_CarriedCast(w_out, layer))
    proj_b, wg = _matmul(u, wb_t, BF16, tm=1024, tn=768, name="in_proj_b", w_is_t=True,
                         carried=_CarriedCast(w_gate, layer))
    proj_c, = _matmul(u, wc_t, BF16, tm=1024, tn=768, name="in_proj_c", w_is_t=True)
    gates_t = _gates(u, w_small_t, bias_col, alog_col, seq_len)
    gates_c = gates_t.T
    gates_r = gates_t.reshape(GATE_LANES, T // CHUNK, CHUNK).transpose(1, 0, 2)

    fox_out = _fox_attention(proj_a, gates_c, fox_out_norm, batch, seq_len)
    sc_out = _short_conv(proj_b, sc_conv_w, seq_len)
    gdn_out = _gdn(proj_b, proj_c, gdn_conv_w, gates_c, gates_r, gdn_out_norm, batch, seq_len)

    y, wu = _out_proj(fox_out, sc_out, gdn_out,
                      w_out_b[:FOX_WIDTH], w_out_b[FOX_WIDTH:FOX_WIDTH + SC_WIDTH],
                      w_out_b[FOX_WIDTH + SC_WIDTH:], _CarriedCast(w_up, layer))
    h, v = _norm_residual(y, x, mix_post_norm, ffn_pre_norm)

    act, wd = _ffn_up(v, wg, wu, _CarriedCast(w_down, layer))
    y2, = _matmul(act, wd, BF16, tm=512, tn=512, name="ffn_down")
    return _norm_residual(y2, h, ffn_post_norm, next_pre_norm)


def kernel(x, mix_pre_norm, w_in, fox_forget_bias, fox_out_norm, sc_conv_w, gdn_conv_w, gdn_a_log,
           gdn_dt_bias, gdn_out_norm, w_out, mix_post_norm, ffn_pre_norm, w_gate, w_up, w_down,
           ffn_post_norm):
    B, S, D = x.shape
    depth = w_in.shape[0]
    h = x.reshape(B * S, D)
    u = _norm_cast(h, mix_pre_norm[0])
    w_view = _column_major_view(w_in)
    windows = [_window_t(w_view, depth, col, width)
               for col, width in ((_A_COL, _A_WIDTH), (_B_COL, _B_WIDTH), (_C_COL, _C_WIDTH))]
    windows.append(_small_gate_weights_t(w_view, depth))
    for l in range(depth):
        nxt = mix_pre_norm[l + 1] if l + 1 < depth else None
        w_in_l = tuple(win[l] for win in windows)
        h, u = _layer(h, u, B, S, l, w_in_l, fox_forget_bias[l], fox_out_norm[l], sc_conv_w[l],
                      gdn_conv_w[l], gdn_a_log[l], gdn_dt_bias[l], gdn_out_norm[l], w_out,
                      mix_post_norm[l], ffn_pre_norm[l], w_gate, w_up, w_down,
                      ffn_post_norm[l], nxt)
    return h.reshape(B, S, D)
```

```python
import functools
import math
from typing import NamedTuple

import jax
import jax.numpy as jnp
from jax import lax
from jax.experimental import pallas as pl
from jax.experimental.pallas import tpu as pltpu

F32 = jnp.float32
BF16 = jnp.bfloat16

HEAD_DIM = 128
FOX_HEADS = 12
FOX_WIDTH = FOX_HEADS * HEAD_DIM
SC_WIDTH = 8 * HEAD_DIM
SC_KERNEL = 3
GDN_HEADS = 12
GDN_WIDTH = GDN_HEADS * HEAD_DIM
GDN_CONV = 4
CHUNK = 64
RMS_EPS = 1e-6
GATE_LANES = 128

VMEM_LIMIT_BYTES = 56 * 1024 * 1024

_FQ, _FK, _FV = 0, 12, 24
_SB, _SC, _SH = 0, 8, 16
_GQ, _GK, _GV = 24, 36, 48
_GZ = 0
_A_COL, _A_WIDTH = 0, 3 * FOX_WIDTH
_B_COL, _B_WIDTH = 3 * FOX_WIDTH + FOX_HEADS, 3 * SC_WIDTH + 3 * GDN_WIDTH
_C_COL, _C_WIDTH = _B_COL + _B_WIDTH + 2 * GDN_HEADS, GDN_WIDTH

_NT = (((1,), (1,)), ((), ()))
_TN = (((0,), (0,)), ((), ()))
_HI = lax.Precision.HIGHEST


def _params(sem):
    return pltpu.CompilerParams(dimension_semantics=sem, vmem_limit_bytes=VMEM_LIMIT_BYTES)


def _norm_cast_kernel(x_ref, g_ref, o_ref):
    x = x_ref[...]
    ms = jnp.mean(x * x, axis=-1, keepdims=True)
    o_ref[...] = (x * lax.rsqrt(ms + RMS_EPS) * g_ref[...]).astype(o_ref.dtype)


def _norm_cast(x, gain, tm=256):
    T, D = x.shape
    tm = min(tm, T)
    return pl.pallas_call(
        _norm_cast_kernel,
        out_shape=jax.ShapeDtypeStruct((T, D), BF16),
        grid=(T // tm,),
        in_specs=[pl.BlockSpec((tm, D), lambda i: (i, 0)),
                  pl.BlockSpec((1, D), lambda i: (0, 0))],
        out_specs=pl.BlockSpec((tm, D), lambda i: (i, 0)),
        compiler_params=_params(("parallel",)),
        name="norm_cast",
    )(x, gain.reshape(1, D))


def _norm_residual_kernel(y_ref, x_ref, g_ref, gn_ref, h_ref, *maybe_u_ref):
    y = y_ref[...].astype(F32)
    ms = jnp.mean(y * y, axis=-1, keepdims=True)
    h = x_ref[...] + y * lax.rsqrt(ms + RMS_EPS) * g_ref[...]
    h_ref[...] = h
    if maybe_u_ref:
        ms2 = jnp.mean(h * h, axis=-1, keepdims=True)
        maybe_u_ref[0][...] = (h * lax.rsqrt(ms2 + RMS_EPS) * gn_ref[...]).astype(BF16)


def _norm_residual(y, x, gain, next_gain, tm=256):
    T, D = x.shape
    tm = min(tm, T)
    emit_next = next_gain is not None
    gn = (next_gain if emit_next else gain).reshape(1, D)
    row = pl.BlockSpec((tm, D), lambda i: (i, 0))
    vec = pl.BlockSpec((1, D), lambda i: (0, 0))
    out_shape = [jax.ShapeDtypeStruct((T, D), F32)]
    out_specs = [row]
    if emit_next:
        out_shape.append(jax.ShapeDtypeStruct((T, D), BF16))
        out_specs.append(row)
    res = pl.pallas_call(
        _norm_residual_kernel,
        out_shape=out_shape,
        grid=(T // tm,),
        in_specs=[row, row, vec, vec],
        out_specs=out_specs,
        compiler_params=_params(("parallel",)),
        name="norm_residual",
    )(y, x, gain.reshape(1, D), gn)
    return (res[0], res[1]) if emit_next else (res[0], None)


def _window_t_kernel(a_ref, b_ref, *o_refs, off, depth):
    tn = o_refs[0].shape[0]
    kt = a_ref.shape[1] // depth
    for l in range(depth):
        x = a_ref[:, pl.ds(l, kt, stride=depth), :]
        if off:
            x = jnp.concatenate([x, b_ref[:, pl.ds(l, kt, stride=depth), :]], axis=0)[off:off + tn]
        xt = pltpu.einshape("nkl->knl", x)
        for k in range(kt):
            o_refs[l][:, k * HEAD_DIM:(k + 1) * HEAD_DIM] = xt[k].astype(BF16)


def _column_major_view(w_in):
    depth, d, p = w_in.shape
    kt = d // HEAD_DIM
    v = jnp.transpose(w_in, (2, 0, 1)).reshape(p, depth, kt, HEAD_DIM)
    return jnp.transpose(v, (0, 2, 1, 3)).reshape(p, kt * depth, HEAD_DIM)


def _window_t(w_view, depth, src_col, width, tn=256, tail=64):
    p, rows, lanes = w_view.shape
    d = rows // depth * lanes
    base, off = src_col // tn, src_col % tn
    assert width % tn == 0 and tn % tail == 0 and off <= tail
    return pl.pallas_call(
        functools.partial(_window_t_kernel, off=off, depth=depth),
        out_shape=[jax.ShapeDtypeStruct((width, d), BF16)] * depth,
        grid=(width // tn,),
        in_specs=[pl.BlockSpec((tn, rows, lanes), lambda j: (base + j, 0, 0)),
                  pl.BlockSpec((tail, rows, lanes), lambda j: ((base + j + 1) * (tn // tail), 0, 0))],
        out_specs=[pl.BlockSpec((tn, d), lambda j: (j, 0))] * depth,
        compiler_params=_params(("parallel",)),
        name="window_t",
    )(w_view, w_view)


class _CarriedCast(NamedTuple):
    w_stack: jax.Array
    layer: int


_BF16_SUBLANES = 16


def _carried_tile_rows(rows, steps):
    for t in range(_BF16_SUBLANES, rows, _BF16_SUBLANES):
        if rows % t == 0 and rows // t <= steps:
            return t
    return rows


def _call(kernel, carried, grid, in_specs, out_specs, out_shape, args, name):
    n_in, n_out = len(in_specs), len(out_specs)
    body = kernel
    if carried is not None:
        _, rows, cols = carried.w_stack.shape
        tile_rows = _carried_tile_rows(rows, math.prod(grid))
        n_tiles = rows // tile_rows

        def tile(*g):
            step = 0
            for size, idx in zip(grid, g):
                step = step * size + idx
            return jnp.minimum(step, n_tiles - 1)

        in_specs = in_specs + [pl.BlockSpec((1, tile_rows, cols),
                                            lambda *g: (carried.layer, tile(*g), 0))]
        out_specs = out_specs + [pl.BlockSpec((tile_rows, cols), lambda *g: (tile(*g), 0))]
        out_shape = out_shape + [jax.ShapeDtypeStruct((rows, cols), BF16)]
        args = args + [carried.w_stack]

        def body(*refs):
            src, dst = refs[n_in], refs[n_in + 1 + n_out]
            dst[...] = src[0].astype(dst.dtype)
            kernel(*refs[:n_in], *refs[n_in + 1:n_in + 1 + n_out], *refs[n_in + 2 + n_out:])

    return pl.pallas_call(
        body, out_shape=out_shape, grid=grid, in_specs=in_specs, out_specs=out_specs,
        compiler_params=_params(("arbitrary",) * len(grid)), name=name)(*args)


def _matmul_kernel(x_ref, w_ref, *rest, scaled, w_is_t):
    if w_is_t:
        acc = lax.dot_general(x_ref[...], w_ref[...], _NT, preferred_element_type=F32)
    else:
        acc = jnp.dot(x_ref[...], w_ref[...], preferred_element_type=F32)
    if scaled:
        s_ref, o_ref = rest
        acc = acc * s_ref[...]
    else:
        o_ref, = rest
    o_ref[...] = acc.astype(o_ref.dtype)


def _matmul(x, w, out_dtype, tm, tn, name, col_scale=None, carried=None, w_is_t=False):
    M, K = x.shape
    N = w.shape[0] if w_is_t else w.shape[1]
    tm, tn = min(tm, M), min(tn, N)
    scaled = col_scale is not None
    in_specs = [pl.BlockSpec((tm, K), lambda i, j: (i, 0)),
                pl.BlockSpec((tn, K), lambda i, j: (j, 0)) if w_is_t
                else pl.BlockSpec((K, tn), lambda i, j: (0, j))]
    args = [x, w]
    if scaled:
        in_specs.append(pl.BlockSpec((1, tn), lambda i, j: (0, j)))
        args.append(col_scale)
    return _call(functools.partial(_matmul_kernel, scaled=scaled, w_is_t=w_is_t), carried,
                 (M // tm, N // tn), in_specs, [pl.BlockSpec((tm, tn), lambda i, j: (i, j))],
                 [jax.ShapeDtypeStruct((M, N), out_dtype)], args, name)


def _out_proj_kernel(a1_ref, a2_ref, a3_ref, w1_ref, w2_ref, w3_ref, o_ref):
    acc = jnp.dot(a1_ref[...], w1_ref[...], preferred_element_type=F32)
    acc += jnp.dot(a2_ref[...], w2_ref[...], preferred_element_type=F32)
    acc += jnp.dot(a3_ref[...], w3_ref[...], preferred_element_type=F32)
    o_ref[...] = acc.astype(o_ref.dtype)


def _out_proj(a1, a2, a3, w1, w2, w3, carried, tm=1024, tn=1024):
    T = a1.shape[0]
    N = w1.shape[1]
    tm, tn = min(tm, T), min(tn, N)

    def lhs(a):
        return pl.BlockSpec((tm, a.shape[1]), lambda i, j: (i, 0))

    def rhs(w):
        return pl.BlockSpec((w.shape[0], tn), lambda i, j: (0, j))

    return _call(_out_proj_kernel, carried, (T // tm, N // tn),
                 [lhs(a1), lhs(a2), lhs(a3), rhs(w1), rhs(w2), rhs(w3)],
                 [pl.BlockSpec((tm, tn), lambda i, j: (i, j))],
                 [jax.ShapeDtypeStruct((T, N), BF16)], [a1, a2, a3, w1, w2, w3], "out_proj")


def _ffn_up_kernel(x_ref, wg_ref, wu_ref, o_ref):
    x = x_ref[...]
    g = jnp.dot(x, wg_ref[...], preferred_element_type=F32)
    u = jnp.dot(x, wu_ref[...], preferred_element_type=F32)
    o_ref[...] = (g * jax.nn.sigmoid(g) * u).astype(o_ref.dtype)


def _ffn_up(x, wg, wu, carried, tm=1024, tn=512):
    T, K = x.shape
    N = wg.shape[1]
    tm = min(tm, T)
    return _call(_ffn_up_kernel, carried, (T // tm, pl.cdiv(N, tn)),
                 [pl.BlockSpec((tm, K), lambda i, j: (i, 0)),
                  pl.BlockSpec((K, tn), lambda i, j: (0, j)),
                  pl.BlockSpec((K, tn), lambda i, j: (0, j))],
                 [pl.BlockSpec((tm, tn), lambda i, j: (i, j))],
                 [jax.ShapeDtypeStruct((T, N), BF16)], [x, wg, wu], "ffn_up")


def _gates_kernel(w_ref, u_ref, bias_ref, alog_ref, gt_ref, carry_ref, *, steps_per_seq, ts):
    t = pl.program_id(0)

    @pl.when(t % steps_per_seq == 0)
    def _():
        carry_ref[...] = jnp.zeros_like(carry_ref)

    z = lax.dot_general(w_ref[...], u_ref[...], _NT, preferred_element_type=F32) + bias_ref[...]
    row = lax.broadcasted_iota(jnp.int32, z.shape, 0)
    tail = jnp.log1p(jnp.exp(-jnp.abs(z)))
    log_sig = jnp.minimum(z, 0.0) - tail
    softplus = jnp.maximum(z, 0.0) + tail
    sig = 1.0 / (1.0 + jnp.exp(-z))
    decay = -jnp.exp(alog_ref[...]) * softplus
    val = jnp.where(row < FOX_HEADS, log_sig, jnp.where(row < FOX_HEADS + GDN_HEADS, decay, sig))

    src = lax.broadcasted_iota(jnp.int32, (ts, ts), 0)
    dst = lax.broadcasted_iota(jnp.int32, (ts, ts), 1)
    upper = src <= dst
    same_chunk = (src // CHUNK) == (dst // CHUNK)
    cum_all = jnp.dot(val, upper.astype(F32), precision=_HI, preferred_element_type=F32)
    cum_chunk = jnp.dot(val, (upper & same_chunk).astype(F32), precision=_HI,
                        preferred_element_type=F32)
    cum_all = cum_all + carry_ref[...]
    carry_ref[...] = cum_all[:, ts - 1:ts]
    gt_ref[...] = jnp.where(row < FOX_HEADS, cum_all,
                            jnp.where(row < FOX_HEADS + GDN_HEADS, cum_chunk, val))


def _gates(u, w_small_t, bias_col, alog_col, seq_len, ts=512):
    T, D = u.shape
    ts = min(ts, seq_len)
    return pl.pallas_call(
        functools.partial(_gates_kernel, steps_per_seq=seq_len // ts, ts=ts),
        out_shape=jax.ShapeDtypeStruct((GATE_LANES, T), F32),
        grid=(T // ts,),
        in_specs=[pl.BlockSpec((GATE_LANES, D), lambda t: (0, 0)),
                  pl.BlockSpec((ts, D), lambda t: (t, 0)),
                  pl.BlockSpec((GATE_LANES, 1), lambda t: (0, 0)),
                  pl.BlockSpec((GATE_LANES, 1), lambda t: (0, 0))],
        out_specs=pl.BlockSpec((GATE_LANES, ts), lambda t: (0, t)),
        scratch_shapes=[pltpu.VMEM((GATE_LANES, 1), F32)],
        compiler_params=_params(("arbitrary",)),
        name="gates",
    )(w_small_t, u, bias_col, alog_col)


FOX_Q_SCALE = (HEAD_DIM ** -0.5) * math.log2(math.e)
FOX_PAIR = 2
_FOX_BUILD_ROWS = 512


def _fox_kernel(q_ref, qn_ref, k_ref, v_ref, gc_ref, g_ref, o_ref, kaug, vaug, m_ref, acc_ref,
                s0_ref, sa_ref, sb_ref, *, tq, tk, seq_len, pairs):
    D = HEAD_DIM
    pair = pl.program_id(0) % pairs
    qi = pl.program_id(1)

    @pl.when(qi == 0)
    def _build():
        ri = lax.broadcasted_iota(jnp.int32, (3 * D, D), 0)
        ci = lax.broadcasted_iota(jnp.int32, (3 * D, D), 1)
        ones = jnp.ones((_FOX_BUILD_ROWS, D), BF16)

        def chunk(i, carry):
            r0 = pl.multiple_of(i * _FOX_BUILD_ROWS, _FOX_BUILD_ROWS)
            rows = pl.ds(r0, _FOX_BUILD_ROWS)
            g = gc_ref[rows, :] * (-math.log2(math.e))
            hi = g.astype(BF16)
            r1 = g - hi.astype(F32)
            mid = r1.astype(BF16)
            lo = (r1 - mid.astype(F32)).astype(BF16)
            pieces = jnp.concatenate([hi, mid, lo], axis=1)
            for gg in range(FOX_PAIR):
                h = pair * FOX_PAIR + gg
                sel = (((ri == h) & (ci == 0)) | ((ri == D + h) & (ci == 1))
                       | ((ri == 2 * D + h) & (ci == 2)))
                aug = jnp.dot(pieces, sel.astype(BF16), preferred_element_type=F32)
                kaug[gg, rows, 0:D] = k_ref[rows, gg * D:(gg + 1) * D]
                kaug[gg, rows, D:2 * D] = aug.astype(BF16)
                vaug[gg, rows, 0:D] = v_ref[rows, gg * D:(gg + 1) * D]
                vaug[gg, rows, D:2 * D] = ones
            return carry

        lax.fori_loop(0, seq_len // _FOX_BUILD_ROWS, chunk, 0)

    lane = lax.broadcasted_iota(jnp.int32, (tq, D), 1)
    ones3 = jnp.where(lane < 3, 1.0, 0.0).astype(BF16)
    m_ref[...] = jnp.full_like(m_ref, -jnp.inf)
    acc_ref[...] = jnp.zeros_like(acc_ref)

    def scores(kj, slot_ref, queries=q_ref):
        k0 = pl.multiple_of(kj * tk, tk)
        for gg in range(FOX_PAIR):
            q_aug = jnp.concatenate([queries[:, gg * D:(gg + 1) * D], ones3], axis=1)
            slot_ref[gg] = lax.dot_general(q_aug, kaug[gg, pl.ds(k0, tk), :], _NT,
                                           preferred_element_type=F32)

    def accumulate(kj, slot_ref, masked=False):
        k0 = pl.multiple_of(kj * tk, tk)
        for gg in range(FOX_PAIR):
            s = slot_ref[gg]
            if masked:
                r = qi * tq + lax.broadcasted_iota(jnp.int32, s.shape, 0)
                c = kj * tk + lax.broadcasted_iota(jnp.int32, s.shape, 1)
                s = jnp.where(c <= r, s, -jnp.inf)
            m_prev = m_ref[gg]
            m_new = jnp.maximum(m_prev, jnp.max(s, axis=-1, keepdims=True))
            alpha = jnp.exp2(m_prev - m_new)
            p = jnp.concatenate(
                [jnp.exp2(s[:, j * D:(j + 1) * D] - m_new) for j in range(tk // D)],
                axis=1).astype(BF16)
            pv = jnp.dot(p, vaug[gg, pl.ds(k0, tk), :], preferred_element_type=F32)
            acc_ref[gg] = jnp.concatenate([alpha, alpha], axis=1) * acc_ref[gg] + pv
            m_ref[gg] = m_new

    n_full = (qi * tq) // tk
    first_prefetching_step = tk // tq

    @pl.when(qi <= first_prefetching_step)
    def _():
        scores(0, s0_ref)

    @pl.when(n_full == 0)
    def _():
        accumulate(0, s0_ref, masked=True)

    @pl.when(n_full >= 1)
    def _():
        scores(1, sa_ref)
        accumulate(0, s0_ref)

    def body(i, carry):
        scores(2 * i + 2, sb_ref)
        accumulate(2 * i + 1, sa_ref)
        scores(2 * i + 3, sa_ref)
        accumulate(2 * i + 2, sb_ref)
        return carry

    lax.fori_loop(0, (n_full - 1) // 2, body, 0)

    @pl.when((n_full >= 1) & (n_full % 2 == 1))
    def _():
        scores(0, s0_ref, qn_ref)
        accumulate(n_full, sa_ref, masked=True)

    @pl.when((n_full >= 1) & (n_full % 2 == 0))
    def _():
        scores(n_full, sb_ref)
        accumulate(n_full - 1, sa_ref)
        scores(0, s0_ref, qn_ref)
        accumulate(n_full, sb_ref, masked=True)

    for gg in range(FOX_PAIR):
        acc = acc_ref[gg]
        out = acc[:, 0:D] / acc[:, D:2 * D]
        ms = jnp.mean(out * out, axis=-1, keepdims=True)
        o_ref[:, gg * D:(gg + 1) * D] = (out * lax.rsqrt(ms + RMS_EPS) * g_ref[...]).astype(o_ref.dtype)


def _fox_attention(proj, gates_c, out_gain, batch, seq_len, tq=512, tk=1024):
    T = proj.shape[0]
    tq = min(tq, seq_len)
    tk = min(tk, seq_len)
    assert tk % tq == 0 and seq_len % tk == 0
    nq = seq_len // tq
    pairs = FOX_HEADS // FOX_PAIR
    W = FOX_PAIR * HEAD_DIM

    once = pl.Buffered(1)

    def rows_spec(base):
        return pl.BlockSpec((seq_len, W), lambda bp, qi: (bp // pairs, base // FOX_PAIR + bp % pairs),
                            pipeline_mode=once)

    return pl.pallas_call(
        functools.partial(_fox_kernel, tq=tq, tk=tk, seq_len=seq_len, pairs=pairs),
        out_shape=jax.ShapeDtypeStruct((T, FOX_WIDTH), BF16),
        grid=(batch * pairs, nq),
        in_specs=[pl.BlockSpec((tq, W), lambda bp, qi: ((bp // pairs) * nq + qi,
                                                         _FQ // FOX_PAIR + bp % pairs)),
                  pl.BlockSpec((tq, W), lambda bp, qi: ((bp // pairs) * nq + jnp.minimum(qi + 1, nq - 1),
                                                         _FQ // FOX_PAIR + bp % pairs)),
                  rows_spec(_FK), rows_spec(_FV),
                  pl.BlockSpec((seq_len, GATE_LANES), lambda bp, qi: (bp // pairs, 0),
                               pipeline_mode=once),
                  pl.BlockSpec((1, HEAD_DIM), lambda bp, qi: (0, 0))],
        out_specs=pl.BlockSpec((tq, W), lambda bp, qi: ((bp // pairs) * nq + qi, bp % pairs)),
        scratch_shapes=[pltpu.VMEM((FOX_PAIR, seq_len, 2 * HEAD_DIM), BF16),
                        pltpu.VMEM((FOX_PAIR, seq_len, 2 * HEAD_DIM), BF16),
                        pltpu.VMEM((FOX_PAIR, tq, HEAD_DIM), F32),
                        pltpu.VMEM((FOX_PAIR, tq, 2 * HEAD_DIM), F32),
                        pltpu.VMEM((FOX_PAIR, tq, tk), F32),
                        pltpu.VMEM((FOX_PAIR, tq, tk), F32),
                        pltpu.VMEM((FOX_PAIR, tq, tk), F32)],
        compiler_params=_params(("parallel", "arbitrary")),
        name="fox_attention",
    )(proj, proj, proj, proj, gates_c, out_gain.reshape(1, HEAD_DIM))


_HALO = 8


def _short_conv_kernel(b_ref, c_ref, h_ref, w_ref, o_ref, buf_ref, *, steps_per_seq, ts):
    t = pl.program_id(1)

    @pl.when(t % steps_per_seq == 0)
    def _():
        buf_ref[0:_HALO, :] = jnp.zeros((_HALO, buf_ref.shape[1]), F32)

    buf_ref[_HALO:_HALO + ts, :] = c_ref[...].astype(F32) * h_ref[...].astype(F32)
    w = w_ref[...]
    y = w[0:1, :] * buf_ref[pl.ds(_HALO - 2, ts), :]
    y += w[1:2, :] * buf_ref[pl.ds(_HALO - 1, ts), :]
    y += w[2:3, :] * buf_ref[pl.ds(_HALO, ts), :]
    o_ref[...] = (b_ref[...].astype(F32) * y).astype(o_ref.dtype)
    buf_ref[0:_HALO, :] = buf_ref[ts:ts + _HALO, :]


def _short_conv(proj, conv_w, seq_len, ts=1024, tc=1024):
    T = proj.shape[0]
    ts = min(ts, seq_len)
    per = tc // HEAD_DIM

    def col(base):
        return pl.BlockSpec((ts, tc), lambda c, t: (t, base // per + c))

    return pl.pallas_call(
        functools.partial(_short_conv_kernel, steps_per_seq=seq_len // ts, ts=ts),
        out_shape=jax.ShapeDtypeStruct((T, SC_WIDTH), BF16),
        grid=(SC_WIDTH // tc, T // ts),
        in_specs=[col(_SB), col(_SC), col(_SH),
                  pl.BlockSpec((SC_KERNEL, tc), lambda c, t: (0, c))],
        out_specs=pl.BlockSpec((ts, tc), lambda c, t: (t, c)),
        scratch_shapes=[pltpu.VMEM((_HALO + ts, tc), F32)],
        compiler_params=_params(("parallel", "arbitrary")),
        name="short_conv",
    )(proj, proj, proj, conv_w)


def _bmm(a, b, dims):
    return lax.dot_general(a.astype(BF16), b.astype(BF16), dims, preferred_element_type=F32)


_B_NN = (((2,), (1,)), ((0,), (0,)))
_B_NT = (((2,), (2,)), ((0,), (0,)))


GDN_PAIR = 4


def _gdn_kernel(q_ref, k_ref, v_ref, z_ref, wq_ref, wk_ref, wv_ref, gc_ref, gr_ref, gain_ref,
                o_ref, qbuf, kbuf, vbuf, state_ref, obuf, *, rows):
    hp = pl.program_id(1)
    t = pl.program_id(2)
    n = rows // CHUNK
    D = HEAD_DIM

    @pl.when(t == 0)
    def _():
        zeros = jnp.zeros((_HALO, GDN_PAIR * D), F32)
        qbuf[0:_HALO, :] = zeros
        kbuf[0:_HALO, :] = zeros
        vbuf[0:_HALO, :] = zeros
        state_ref[...] = jnp.zeros_like(state_ref)

    def conv_silu(x_ref, w_ref, buf):
        buf[_HALO:_HALO + rows, :] = x_ref[...].astype(F32)
        w = w_ref[...]
        y = w[0:1, :] * buf[pl.ds(_HALO - 3, rows), :]
        y += w[1:2, :] * buf[pl.ds(_HALO - 2, rows), :]
        y += w[2:3, :] * buf[pl.ds(_HALO - 1, rows), :]
        y += w[3:4, :] * buf[pl.ds(_HALO, rows), :]
        buf[0:_HALO, :] = buf[rows:rows + _HALO, :]
        return y * jax.nn.sigmoid(y)

    q_all = conv_silu(q_ref, wq_ref, qbuf)
    k_all = conv_silu(k_ref, wk_ref, kbuf)
    v_all = conv_silu(v_ref, wv_ref, vbuf)

    gates = gc_ref[...]
    lane = lax.broadcasted_iota(jnp.int32, gates.shape, 1)
    ri = lax.broadcasted_iota(jnp.int32, (CHUNK, CHUNK), 0)
    ci = lax.broadcasted_iota(jnp.int32, (CHUNK, CHUNK), 1)
    tri_incl = (ci <= ri)[None]
    tri_strict = (ci < ri)[None]
    same16 = ((ri // 16) == (ci // 16))[None]
    same32 = ((ri // 32) == (ci // 32))[None]
    eye = (ri == ci).astype(F32)[None]

    def chunk_terms(gg):
        h = hp * GDN_PAIR + gg
        cols = slice(gg * D, (gg + 1) * D)
        q, k, v = q_all[:, cols], k_all[:, cols], v_all[:, cols]
        q = q * (lax.rsqrt(jnp.sum(q * q, axis=-1, keepdims=True) + RMS_EPS) * (D ** -0.5))
        k = k * lax.rsqrt(jnp.sum(k * k, axis=-1, keepdims=True) + RMS_EPS)
        g_cum = jnp.sum(jnp.where(lane == FOX_HEADS + h, gates, 0.0), axis=-1, keepdims=True)
        beta = jnp.sum(jnp.where(lane == FOX_HEADS + GDN_HEADS + h, gates, 0.0), axis=-1,
                       keepdims=True)
        q3 = q.reshape(n, CHUNK, D)
        k3 = k.reshape(n, CHUNK, D)
        v3 = v.reshape(n, CHUNK, D)
        g3 = g_cum.reshape(n, CHUNK, 1)
        b3 = beta.reshape(n, CHUNK, 1)
        g_row = gr_ref[:, pl.ds(FOX_HEADS + h, 1), :]
        g_last = g3[:, CHUNK - 1:CHUNK, :]

        decay = jnp.exp(jnp.where(tri_incl, g3 - g_row, -jnp.inf))
        kk = _bmm(k3, k3, _B_NT)
        L = jnp.where(tri_strict, b3 * kk * decay, 0.0)

        P = jnp.where(same16, L, 0.0)
        X = eye - P
        P2 = _bmm(P, P, _B_NN)
        X = _bmm(X, eye + P2, _B_NN)
        P4 = _bmm(P2, P2, _B_NN)
        X = _bmm(X, eye + P4, _B_NN)
        P8 = _bmm(P4, P4, _B_NN)
        X = _bmm(X, eye + P8, _B_NN)
        O32 = jnp.where(same32 & jnp.logical_not(same16), L, 0.0)
        X = X - _bmm(_bmm(X, O32, _B_NN), X, _B_NN)
        O64 = jnp.where(same32, 0.0, L)
        X = X - _bmm(_bmm(X, O64, _B_NN), X, _B_NN)

        e3 = jnp.exp(g3)
        rhs = jnp.concatenate([v3 * b3, k3 * (b3 * e3)], axis=-1)
        sol = _bmm(X, rhs, _B_NN)
        attn = jnp.where(tri_incl, _bmm(q3, k3, _B_NT) * decay, 0.0)
        return dict(u=sol[:, :, :D], w=sol[:, :, D:].astype(BF16), q=(q3 * e3).astype(BF16),
                    a=attn.astype(BF16), k=(k3 * jnp.exp(g_last - g3)).astype(BF16),
                    gl=jnp.exp(g_last))

    def pair_terms(p):
        t0, t1 = chunk_terms(2 * p), chunk_terms(2 * p + 1)
        return dict(
            u=jnp.concatenate([t0["u"], t1["u"]], axis=-1),
            wq=jnp.concatenate([jnp.concatenate([t0["w"], t1["w"]], axis=-1),
                                jnp.concatenate([t0["q"], t1["q"]], axis=-1)], axis=1),
            a=jnp.concatenate([t0["a"], t1["a"]], axis=-1),
            k=jnp.concatenate([t0["k"], t1["k"]], axis=-1),
            gl=jnp.concatenate([jnp.broadcast_to(t0["gl"], (n, 1, D)),
                                jnp.broadcast_to(t1["gl"], (n, 1, D))], axis=-1))

    def block_diag(x):
        first = lax.broadcasted_iota(jnp.int32, x.shape, 1) < D
        zero = jnp.zeros_like(x)
        return jnp.concatenate([jnp.where(first, x, zero), jnp.where(first, zero, x)], axis=0)

    n_pairs = GDN_PAIR // 2
    terms = [pair_terms(p) for p in range(n_pairs)]
    S = [state_ref[:, p * 2 * D:(p + 1) * 2 * D] for p in range(n_pairs)]
    for c in range(n):
        for p in range(n_pairs):
            tm = terms[p]
            r1 = jnp.dot(tm["wq"][c], block_diag(S[p].astype(BF16)), preferred_element_type=F32)
            v_b = (tm["u"][c] - r1[0:CHUNK]).astype(BF16)
            o_c = r1[CHUNK:2 * CHUNK] + jnp.dot(tm["a"][c], block_diag(v_b),
                                                preferred_element_type=F32)
            kv = lax.dot_general(tm["k"][c], v_b, _TN, preferred_element_type=F32)
            S[p] = S[p] * tm["gl"][c] + jnp.concatenate([kv[0:D, 0:D], kv[D:2 * D, D:2 * D]],
                                                         axis=1)
            obuf[c * CHUNK:(c + 1) * CHUNK, p * 2 * D:(p + 1) * 2 * D] = o_c
    for p in range(n_pairs):
        state_ref[:, p * 2 * D:(p + 1) * 2 * D] = S[p]

    for gg in range(GDN_PAIR):
        cols = slice(gg * D, (gg + 1) * D)
        o = obuf[:, cols]
        ms = jnp.mean(o * o, axis=-1, keepdims=True)
        z = z_ref[:, cols].astype(F32)
        o = o * lax.rsqrt(ms + RMS_EPS) * gain_ref[...] * (z * jax.nn.sigmoid(z))
        o_ref[:, cols] = o.astype(o_ref.dtype)


def _gdn(proj, proj_z, conv_w, gates_c, gates_r, out_gain, batch, seq_len, rows=1024):
    T = proj.shape[0]
    rows = min(rows, seq_len)
    nt = seq_len // rows
    n = rows // CHUNK
    pairs = GDN_HEADS // GDN_PAIR
    W = GDN_PAIR * HEAD_DIM

    def col(base):
        return pl.BlockSpec((rows, W), lambda b, hp, t: (b * nt + t, base // GDN_PAIR + hp))

    def wcol(base):
        return pl.BlockSpec((GDN_CONV, W), lambda b, hp, t: (0, base // GDN_PAIR + hp))

    return pl.pallas_call(
        functools.partial(_gdn_kernel, rows=rows),
        out_shape=jax.ShapeDtypeStruct((T, GDN_WIDTH), BF16),
        grid=(batch, pairs, nt),
        in_specs=[col(_GQ), col(_GK), col(_GV), col(_GZ),
                  wcol(0), wcol(GDN_HEADS), wcol(2 * GDN_HEADS),
                  pl.BlockSpec((rows, GATE_LANES), lambda b, hp, t: (b * nt + t, 0)),
                  pl.BlockSpec((n, GATE_LANES, CHUNK), lambda b, hp, t: (b * nt + t, 0, 0)),
                  pl.BlockSpec((1, HEAD_DIM), lambda b, hp, t: (0, 0))],
        out_specs=pl.BlockSpec((rows, W), lambda b, hp, t: (b * nt + t, hp)),
        scratch_shapes=[pltpu.VMEM((_HALO + rows, W), F32),
                        pltpu.VMEM((_HALO + rows, W), F32),
                        pltpu.VMEM((_HALO + rows, W), F32),
                        pltpu.VMEM((HEAD_DIM, W), F32),
                        pltpu.VMEM((rows, W), F32)],
        compiler_params=_params(("parallel", "parallel", "arbitrary")),
        name="gdn",
    )(proj, proj, proj, proj_z, conv_w, conv_w, conv_w, gates_c, gates_r,
      out_gain.reshape(1, HEAD_DIM))


_GATE_F_ROWS, _GATE_AB_ROWS = 16, 64


def _gate_rows_kernel(f_ref, ab_ref, *o_refs, depth):
    kt = f_ref.shape[1] // depth
    row = lax.broadcasted_iota(jnp.int32, (GATE_LANES, HEAD_DIM), 0)
    for l in range(depth):
        f = pltpu.einshape("nkl->knl", f_ref[:, pl.ds(l, kt, stride=depth), :])
        ab = pltpu.einshape("nkl->knl", ab_ref[:, pl.ds(l, kt, stride=depth), :])
        for k in range(kt):
            fk = jnp.concatenate([f[k], jnp.zeros((GATE_LANES - _GATE_F_ROWS, HEAD_DIM), F32)], axis=0)
            abk = jnp.concatenate([ab[k], jnp.zeros((GATE_LANES - _GATE_AB_ROWS, HEAD_DIM), F32)],
                                  axis=0)
            w = jnp.where(row < FOX_HEADS, fk,
                          jnp.where(row < FOX_HEADS + 2 * GDN_HEADS, abk, 0.0))
            o_refs[l][:, k * HEAD_DIM:(k + 1) * HEAD_DIM] = w.astype(BF16)


def _small_gate_weights_t(w_view, depth):
    p, rows, lanes = w_view.shape
    d = rows // depth * lanes
    f_col, ab_col = _A_WIDTH, _B_COL + _B_WIDTH
    assert f_col % _GATE_F_ROWS == 0 and ab_col % _GATE_AB_ROWS == FOX_HEADS
    return pl.pallas_call(
        functools.partial(_gate_rows_kernel, depth=depth),
        out_shape=[jax.ShapeDtypeStruct((GATE_LANES, d), BF16)] * depth,
        grid=(1,),
        in_specs=[pl.BlockSpec((_GATE_F_ROWS, rows, lanes), lambda i: (f_col // _GATE_F_ROWS, 0, 0)),
                  pl.BlockSpec((_GATE_AB_ROWS, rows, lanes), lambda i: (ab_col // _GATE_AB_ROWS, 0, 0))],
        out_specs=[pl.BlockSpec((GATE_LANES, d), lambda i: (0, 0))] * depth,
        compiler_params=_params(("arbitrary",)),
        name="gate_rows",
    )(w_view, w_view)


def _gate_columns(fox_forget_bias, gdn_a_log, gdn_dt_bias):
    pad = GATE_LANES - FOX_HEADS - GDN_HEADS
    bias = jnp.concatenate([fox_forget_bias, gdn_dt_bias, jnp.zeros((pad,), F32)])
    alog = jnp.concatenate([jnp.zeros((FOX_HEADS,), F32), gdn_a_log, jnp.zeros((pad,), F32)])
    return bias.reshape(GATE_LANES, 1), alog.reshape(GATE_LANES, 1)


def _layer(x, u, batch, seq_len, layer, w_in, fox_forget_bias, fox_out_norm, sc_conv_w, gdn_conv_w,
           gdn_a_log, gdn_dt_bias, gdn_out_norm, w_out, mix_post_norm, ffn_pre_norm,
           w_gate, w_up, w_down, ffn_post_norm, next_pre_norm):
    T, D = x.shape
    wa_t, wb_t, wc_t, w_small_t = w_in
    bias_col, alog_col = _gate_columns(fox_forget_bias, gdn_a_log, gdn_dt_bias)

    q_scale = jnp.concatenate([jnp.full((FOX_WIDTH,), FOX_Q_SCALE, F32),
                               jnp.ones((_A_WIDTH - FOX_WIDTH,), F32)]).reshape(1, _A_WIDTH)
    proj_a, w_out_b = _matmul(u, wa_t, BF16, tm=1024, tn=768, name="in_proj_a", w_is_t=True,
                              col_scale=q_scale, carried=_CarriedCast(w_out, layer))
    proj_b, wg = _matmul(u, wb_t, BF16, tm=1024, tn=768, name="in_proj_b", w_is_t=True,
                         carried=_CarriedCast(w_gate, layer))
    proj_c, = _matmul(u, wc_t, BF16, tm=1024, tn=768, name="in_proj_c", w_is_t=True)
    gates_t = _gates(u, w_small_t, bias_col, alog_col, seq_len)
    gates_c = gates_t.T
    gates_r = gates_t.reshape(GATE_LANES, T // CHUNK, CHUNK).transpose(1, 0, 2)

    fox_out = _fox_attention(proj_a, gates_c, fox_out_norm, batch, seq_len)
    sc_out = _short_conv(proj_b, sc_conv_w, seq_len)
    gdn_out = _gdn(proj_b, proj_c, gdn_conv_w, gates_c, gates_r, gdn_out_norm, batch, seq_len)

    y, wu = _out_proj(fox_out, sc_out, gdn_out,
                      w_out_b[:FOX_WIDTH], w_out_b[FOX_WIDTH:FOX_WIDTH + SC_WIDTH],
                      w_out_b[FOX_WIDTH + SC_WIDTH:], _CarriedCast(w_up, layer))
    h, v = _norm_residual(y, x, mix_post_norm, ffn_pre_norm)

    act, wd = _ffn_up(v, wg, wu, _CarriedCast(w_down, layer), tm=2048, tn=256)
    y2, = _matmul(act, wd, BF16, tm=512, tn=512, name="ffn_down")
    return _norm_residual(y2, h, ffn_post_norm, next_pre_norm)


def kernel(x, mix_pre_norm, w_in, fox_forget_bias, fox_out_norm, sc_conv_w, gdn_conv_w, gdn_a_log,
           gdn_dt_bias, gdn_out_norm, w_out, mix_post_norm, ffn_pre_norm, w_gate, w_up, w_down,
           ffn_post_norm):
    B, S, D = x.shape
    depth = w_in.shape[0]
    h = x.reshape(B * S, D)
    u = _norm_cast(h, mix_pre_norm[0])
    w_view = _column_major_view(w_in)
    windows = [_window_t(w_view, depth, col, width)
               for col, width in ((_A_COL, _A_WIDTH), (_B_COL, _B_WIDTH), (_C_COL, _C_WIDTH))]
    windows.append(_small_gate_weights_t(w_view, depth))
    for l in range(depth):
        nxt = mix_pre_norm[l + 1] if l + 1 < depth else None
        w_in_l = tuple(win[l] for win in windows)
        h, u = _layer(h, u, B, S, l, w_in_l, fox_forget_bias[l], fox_out_norm[l], sc_conv_w[l],
                      gdn_conv_w[l], gdn_a_log[l], gdn_dt_bias[l], gdn_out_norm[l], w_out,
                      mix_post_norm[l], ffn_pre_norm[l], w_gate, w_up, w_down,
                      ffn_post_norm[l], nxt)
    return h.reshape(B, S, D)
```

```python
import functools
import math
from typing import NamedTuple

import jax
import jax.numpy as jnp
from jax import lax
from jax.experimental import pallas as pl
from jax.experimental.pallas import tpu as pltpu

F32 = jnp.float32
BF16 = jnp.bfloat16

HEAD_DIM = 128
FOX_HEADS = 12
FOX_WIDTH = FOX_HEADS * HEAD_DIM
SC_WIDTH = 8 * HEAD_DIM
SC_KERNEL = 3
GDN_HEADS = 12
GDN_WIDTH = GDN_HEADS * HEAD_DIM
GDN_CONV = 4
CHUNK = 64
RMS_EPS = 1e-6
GATE_LANES = 128

VMEM_LIMIT_BYTES = 56 * 1024 * 1024

_FQ, _FK, _FV = 0, 12, 24
_SB, _SC, _SH = 0, 8, 16
_GQ, _GK, _GV = 24, 36, 48
_GZ = 0
_A_COL, _A_WIDTH = 0, 3 * FOX_WIDTH
_B_COL, _B_WIDTH = 3 * FOX_WIDTH + FOX_HEADS, 3 * SC_WIDTH + 3 * GDN_WIDTH
_C_COL, _C_WIDTH = _B_COL + _B_WIDTH + 2 * GDN_HEADS, GDN_WIDTH

_NT = (((1,), (1,)), ((), ()))
_TN = (((0,), (0,)), ((), ()))


def _params(sem):
    return pltpu.CompilerParams(dimension_semantics=sem, vmem_limit_bytes=VMEM_LIMIT_BYTES)


def _norm_cast_kernel(x_ref, g_ref, o_ref):
    x = x_ref[...]
    ms = jnp.mean(x * x, axis=-1, keepdims=True)
    o_ref[...] = (x * lax.rsqrt(ms + RMS_EPS) * g_ref[...]).astype(o_ref.dtype)


def _norm_cast(x, gain, tm=256):
    T, D = x.shape
    tm = min(tm, T)
    return pl.pallas_call(
        _norm_cast_kernel,
        out_shape=jax.ShapeDtypeStruct((T, D), BF16),
        grid=(T // tm,),
        in_specs=[pl.BlockSpec((tm, D), lambda i: (i, 0)),
                  pl.BlockSpec((1, D), lambda i: (0, 0))],
        out_specs=pl.BlockSpec((tm, D), lambda i: (i, 0)),
        compiler_params=_params(("parallel",)),
        name="norm_cast",
    )(x, gain.reshape(1, D))


def _norm_residual_kernel(y_ref, x_ref, g_ref, gn_ref, h_ref, *maybe_u_ref):
    y = y_ref[...].astype(F32)
    ms = jnp.mean(y * y, axis=-1, keepdims=True)
    h = x_ref[...] + y * lax.rsqrt(ms + RMS_EPS) * g_ref[...]
    h_ref[...] = h
    if maybe_u_ref:
        ms2 = jnp.mean(h * h, axis=-1, keepdims=True)
        maybe_u_ref[0][...] = (h * lax.rsqrt(ms2 + RMS_EPS) * gn_ref[...]).astype(BF16)


def _norm_residual(y, x, gain, next_gain, tm=256):
    T, D = x.shape
    tm = min(tm, T)
    emit_next = next_gain is not None
    gn = (next_gain if emit_next else gain).reshape(1, D)
    row = pl.BlockSpec((tm, D), lambda i: (i, 0))
    vec = pl.BlockSpec((1, D), lambda i: (0, 0))
    out_shape = [jax.ShapeDtypeStruct((T, D), F32)]
    out_specs = [row]
    if emit_next:
        out_shape.append(jax.ShapeDtypeStruct((T, D), BF16))
        out_specs.append(row)
    res = pl.pallas_call(
        _norm_residual_kernel,
        out_shape=out_shape,
        grid=(T // tm,),
        in_specs=[row, row, vec, vec],
        out_specs=out_specs,
        compiler_params=_params(("parallel",)),
        name="norm_residual",
    )(y, x, gain.reshape(1, D), gn)
    return (res[0], res[1]) if emit_next else (res[0], None)


def _window_t_kernel(a_ref, b_ref, *o_refs, off, depth):
    tn = o_refs[0].shape[0]
    kt = a_ref.shape[1] // depth
    for l in range(depth):
        x = a_ref[:, pl.ds(l, kt, stride=depth), :]
        if off:
            x = jnp.concatenate([x, b_ref[:, pl.ds(l, kt, stride=depth), :]], axis=0)[off:off + tn]
        xt = pltpu.einshape("nkl->knl", x)
        for k in range(kt):
            o_refs[l][:, k * HEAD_DIM:(k + 1) * HEAD_DIM] = xt[k].astype(BF16)


def _column_major_view(w_in):
    depth, d, p = w_in.shape
    kt = d // HEAD_DIM
    v = jnp.transpose(w_in, (2, 0, 1)).reshape(p, depth, kt, HEAD_DIM)
    return jnp.transpose(v, (0, 2, 1, 3)).reshape(p, kt * depth, HEAD_DIM)


def _window_t(w_view, depth, src_col, width, tn=256, tail=64):
    p, rows, lanes = w_view.shape
    d = rows // depth * lanes
    base, off = src_col // tn, src_col % tn
    assert width % tn == 0 and tn % tail == 0 and off <= tail
    return pl.pallas_call(
        functools.partial(_window_t_kernel, off=off, depth=depth),
        out_shape=[jax.ShapeDtypeStruct((width, d), BF16)] * depth,
        grid=(width // tn,),
        in_specs=[pl.BlockSpec((tn, rows, lanes), lambda j: (base + j, 0, 0)),
                  pl.BlockSpec((tail, rows, lanes), lambda j: ((base + j + 1) * (tn // tail), 0, 0))],
        out_specs=[pl.BlockSpec((tn, d), lambda j: (j, 0))] * depth,
        compiler_params=_params(("parallel",)),
        name="window_t",
    )(w_view, w_view)


class _CarriedCast(NamedTuple):
    w_stack: jax.Array
    layer: int


_BF16_SUBLANES = 16


def _carried_tile_rows(rows, steps):
    for t in range(_BF16_SUBLANES, rows, _BF16_SUBLANES):
        if rows % t == 0 and rows // t <= steps:
            return t
    return rows


def _call(kernel, carried, grid, in_specs, out_specs, out_shape, args, name):
    n_in, n_out = len(in_specs), len(out_specs)
    body = kernel
    if carried is not None:
        _, rows, cols = carried.w_stack.shape
        tile_rows = _carried_tile_rows(rows, math.prod(grid))
        n_tiles = rows // tile_rows

        def tile(*g):
            step = 0
            for size, idx in zip(grid, g):
                step = step * size + idx
            return jnp.minimum(step, n_tiles - 1)

        in_specs = in_specs + [pl.BlockSpec((1, tile_rows, cols),
                                            lambda *g: (carried.layer, tile(*g), 0))]
        out_specs = out_specs + [pl.BlockSpec((tile_rows, cols), lambda *g: (tile(*g), 0))]
        out_shape = out_shape + [jax.ShapeDtypeStruct((rows, cols), BF16)]
        args = args + [carried.w_stack]

        def body(*refs):
            src, dst = refs[n_in], refs[n_in + 1 + n_out]
            dst[...] = src[0].astype(dst.dtype)
            kernel(*refs[:n_in], *refs[n_in + 1:n_in + 1 + n_out], *refs[n_in + 2 + n_out:])

    return pl.pallas_call(
        body, out_shape=out_shape, grid=grid, in_specs=in_specs, out_specs=out_specs,
        compiler_params=_params(("arbitrary",) * len(grid)), name=name)(*args)


def _matmul_kernel(x_ref, w_ref, *rest, scaled, w_is_t):
    if w_is_t:
        acc = lax.dot_general(x_ref[...], w_ref[...], _NT, preferred_element_type=F32)
    else:
        acc = jnp.dot(x_ref[...], w_ref[...], preferred_element_type=F32)
    if scaled:
        s_ref, o_ref = rest
        acc = acc * s_ref[...]
    else:
        o_ref, = rest
    o_ref[...] = acc.astype(o_ref.dtype)


def _matmul(x, w, out_dtype, tm, tn, name, col_scale=None, carried=None, w_is_t=False):
    M, K = x.shape
    N = w.shape[0] if w_is_t else w.shape[1]
    tm, tn = min(tm, M), min(tn, N)
    scaled = col_scale is not None
    in_specs = [pl.BlockSpec((tm, K), lambda i, j: (i, 0)),
                pl.BlockSpec((tn, K), lambda i, j: (j, 0)) if w_is_t
                else pl.BlockSpec((K, tn), lambda i, j: (0, j))]
    args = [x, w]
    if scaled:
        in_specs.append(pl.BlockSpec((1, tn), lambda i, j: (0, j)))
        args.append(col_scale)
    return _call(functools.partial(_matmul_kernel, scaled=scaled, w_is_t=w_is_t), carried,
                 (M // tm, N // tn), in_specs, [pl.BlockSpec((tm, tn), lambda i, j: (i, j))],
                 [jax.ShapeDtypeStruct((M, N), out_dtype)], args, name)


def _out_proj_kernel(a1_ref, a2_ref, a3_ref, w1_ref, w2_ref, w3_ref, o_ref):
    acc = jnp.dot(a1_ref[...], w1_ref[...], preferred_element_type=F32)
    acc += jnp.dot(a2_ref[...], w2_ref[...], preferred_element_type=F32)
    acc += jnp.dot(a3_ref[...], w3_ref[...], preferred_element_type=F32)
    o_ref[...] = acc.astype(o_ref.dtype)


def _out_proj(a1, a2, a3, w1, w2, w3, carried, tm=1024, tn=1024):
    T = a1.shape[0]
    N = w1.shape[1]
    tm, tn = min(tm, T), min(tn, N)

    def lhs(a):
        return pl.BlockSpec((tm, a.shape[1]), lambda i, j: (i, 0))

    def rhs(w):
        return pl.BlockSpec((w.shape[0], tn), lambda i, j: (0, j))

    return _call(_out_proj_kernel, carried, (T // tm, N // tn),
                 [lhs(a1), lhs(a2), lhs(a3), rhs(w1), rhs(w2), rhs(w3)],
                 [pl.BlockSpec((tm, tn), lambda i, j: (i, j))],
                 [jax.ShapeDtypeStruct((T, N), BF16)], [a1, a2, a3, w1, w2, w3], "out_proj")


def _ffn_up_kernel(x_ref, wg_ref, wu_ref, o_ref):
    x = x_ref[...]
    g = jnp.dot(x, wg_ref[...], preferred_element_type=F32)
    u = jnp.dot(x, wu_ref[...], preferred_element_type=F32)
    o_ref[...] = (g * jax.nn.sigmoid(g) * u).astype(o_ref.dtype)


def _ffn_up(x, wg, wu, carried, tm=1024, tn=512):
    T, K = x.shape
    N = wg.shape[1]
    tm = min(tm, T)
    return _call(_ffn_up_kernel, carried, (T // tm, pl.cdiv(N, tn)),
                 [pl.BlockSpec((tm, K), lambda i, j: (i, 0)),
                  pl.BlockSpec((K, tn), lambda i, j: (0, j)),
                  pl.BlockSpec((K, tn), lambda i, j: (0, j))],
                 [pl.BlockSpec((tm, tn), lambda i, j: (i, j))],
                 [jax.ShapeDtypeStruct((T, N), BF16)], [x, wg, wu], "ffn_up")


def _gates_kernel(w_ref, u_ref, bias_ref, alog_ref, gt_ref, carry_ref, *, steps_per_seq, ts):
    t = pl.program_id(0)

    @pl.when(t % steps_per_seq == 0)
    def _():
        carry_ref[...] = jnp.zeros_like(carry_ref)

    z = lax.dot_general(w_ref[...], u_ref[...], _NT, preferred_element_type=F32) + bias_ref[...]
    row = lax.broadcasted_iota(jnp.int32, z.shape, 0)
    tail = jnp.log1p(jnp.exp(-jnp.abs(z)))
    log_sig = jnp.minimum(z, 0.0) - tail
    softplus = jnp.maximum(z, 0.0) + tail
    sig = 1.0 / (1.0 + jnp.exp(-z))
    decay = -jnp.exp(alog_ref[...]) * softplus
    val = jnp.where(row < FOX_HEADS, log_sig, jnp.where(row < FOX_HEADS + GDN_HEADS, decay, sig))

    src = lax.broadcasted_iota(jnp.int32, (ts, ts), 0)
    dst = lax.broadcasted_iota(jnp.int32, (ts, ts), 1)
    upper = src <= dst
    same_chunk = (src // CHUNK) == (dst // CHUNK)
    hi = val.astype(BF16)
    rest = val - hi.astype(F32)
    mid = rest.astype(BF16)
    lo = (rest - mid.astype(F32)).astype(BF16)
    pieces = jnp.concatenate([hi, mid, lo], axis=1)

    def prefix_sum(mask):
        ones = mask.astype(BF16)
        return jnp.dot(pieces, jnp.concatenate([ones, ones, ones], axis=0),
                       preferred_element_type=F32)

    cum_all = prefix_sum(upper)
    cum_chunk = prefix_sum(upper & same_chunk)
    cum_all = cum_all + carry_ref[...]
    carry_ref[...] = cum_all[:, ts - 1:ts]
    gt_ref[...] = jnp.where(row < FOX_HEADS, cum_all,
                            jnp.where(row < FOX_HEADS + GDN_HEADS, cum_chunk, val))


def _gates(u, w_small_t, bias_col, alog_col, seq_len, ts=512):
    T, D = u.shape
    ts = min(ts, seq_len)
    return pl.pallas_call(
        functools.partial(_gates_kernel, steps_per_seq=seq_len // ts, ts=ts),
        out_shape=jax.ShapeDtypeStruct((GATE_LANES, T), F32),
        grid=(T // ts,),
        in_specs=[pl.BlockSpec((GATE_LANES, D), lambda t: (0, 0)),
                  pl.BlockSpec((ts, D), lambda t: (t, 0)),
                  pl.BlockSpec((GATE_LANES, 1), lambda t: (0, 0)),
                  pl.BlockSpec((GATE_LANES, 1), lambda t: (0, 0))],
        out_specs=pl.BlockSpec((GATE_LANES, ts), lambda t: (0, t)),
        scratch_shapes=[pltpu.VMEM((GATE_LANES, 1), F32)],
        compiler_params=_params(("arbitrary",)),
        name="gates",
    )(w_small_t, u, bias_col, alog_col)


FOX_Q_SCALE = (HEAD_DIM ** -0.5) * math.log2(math.e)
FOX_PAIR = 2
_FOX_BUILD_ROWS = 512


def _fox_kernel(q_ref, qn_ref, k_ref, v_ref, gc_ref, g_ref, o_ref, kaug, vaug, m_ref, acc_ref,
                s0_ref, sa_ref, sb_ref, *, tq, tk, seq_len, pairs):
    D = HEAD_DIM
    pair = pl.program_id(0) % pairs
    qi = pl.program_id(1)

    @pl.when(qi == 0)
    def _build():
        ri = lax.broadcasted_iota(jnp.int32, (3 * D, D), 0)
        ci = lax.broadcasted_iota(jnp.int32, (3 * D, D), 1)
        ones = jnp.ones((_FOX_BUILD_ROWS, D), BF16)

        def chunk(i, carry):
            r0 = pl.multiple_of(i * _FOX_BUILD_ROWS, _FOX_BUILD_ROWS)
            rows = pl.ds(r0, _FOX_BUILD_ROWS)
            g = gc_ref[rows, :] * (-math.log2(math.e))
            hi = g.astype(BF16)
            r1 = g - hi.astype(F32)
            mid = r1.astype(BF16)
            lo = (r1 - mid.astype(F32)).astype(BF16)
            pieces = jnp.concatenate([hi, mid, lo], axis=1)
            for gg in range(FOX_PAIR):
                h = pair * FOX_PAIR + gg
                sel = (((ri == h) & (ci == 0)) | ((ri == D + h) & (ci == 1))
                       | ((ri == 2 * D + h) & (ci == 2)))
                aug = jnp.dot(pieces, sel.astype(BF16), preferred_element_type=F32)
                kaug[gg, rows, 0:D] = k_ref[rows, gg * D:(gg + 1) * D]
                kaug[gg, rows, D:2 * D] = aug.astype(BF16)
                vaug[gg, rows, 0:D] = v_ref[rows, gg * D:(gg + 1) * D]
                vaug[gg, rows, D:2 * D] = ones
            return carry

        lax.fori_loop(0, seq_len // _FOX_BUILD_ROWS, chunk, 0)

    lane = lax.broadcasted_iota(jnp.int32, (tq, D), 1)
    ones3 = jnp.where(lane < 3, 1.0, 0.0).astype(BF16)
    m_ref[...] = jnp.full_like(m_ref, -jnp.inf)
    acc_ref[...] = jnp.zeros_like(acc_ref)

    def scores(kj, slot_ref, queries=q_ref, width=tk):
        k0 = pl.multiple_of(kj * tk, tk)
        for gg in range(FOX_PAIR):
            q_aug = jnp.concatenate([queries[:, gg * D:(gg + 1) * D], ones3], axis=1)
            slot_ref[gg, :, 0:width] = lax.dot_general(q_aug, kaug[gg, pl.ds(k0, width), :], _NT,
                                                       preferred_element_type=F32)

    def accumulate(kj, slot_ref, masked=False, width=tk):
        k0 = pl.multiple_of(kj * tk, tk)
        for gg in range(FOX_PAIR):
            s = slot_ref[gg, :, 0:width]
            if masked:
                r = qi * tq + lax.broadcasted_iota(jnp.int32, s.shape, 0)
                c = kj * tk + lax.broadcasted_iota(jnp.int32, s.shape, 1)
                s = jnp.where(c <= r, s, -jnp.inf)
            m_prev = m_ref[gg]
            m_new = jnp.maximum(m_prev, jnp.max(s, axis=-1, keepdims=True))
            alpha = jnp.exp2(m_prev - m_new)
            p = jnp.concatenate(
                [jnp.exp2(s[:, j * D:(j + 1) * D] - m_new) for j in range(width // D)],
                axis=1).astype(BF16)
            pv = jnp.dot(p, vaug[gg, pl.ds(k0, width), :], preferred_element_type=F32)
            acc_ref[gg] = jnp.concatenate([alpha, alpha], axis=1) * acc_ref[gg] + pv
            m_ref[gg] = m_new

    n_full = (qi * tq) // tk
    first_prefetching_step = tk // tq

    @pl.when(qi <= first_prefetching_step)
    def _():
        scores(0, s0_ref)

    ratio = tk // tq

    def diagonal_variants(cond, emit):
        for r in range(ratio):
            pl.when(cond & (qi % ratio == r))(functools.partial(emit, (r + 1) * tq))

    diagonal_variants(n_full == 0, lambda width: accumulate(0, s0_ref, masked=True, width=width))

    @pl.when(n_full >= 1)
    def _():
        scores(1, sa_ref)
        accumulate(0, s0_ref)

    def body(i, carry):
        scores(2 * i + 2, sb_ref)
        accumulate(2 * i + 1, sa_ref)
        scores(2 * i + 3, sa_ref)
        accumulate(2 * i + 2, sb_ref)
        return carry

    lax.fori_loop(0, (n_full - 1) // 2, body, 0)

    def odd_tail(width):
        scores(0, s0_ref, qn_ref)
        accumulate(n_full, sa_ref, masked=True, width=width)

    def even_tail(width):
        scores(n_full, sb_ref, width=width)
        accumulate(n_full - 1, sa_ref)
        scores(0, s0_ref, qn_ref)
        accumulate(n_full, sb_ref, masked=True, width=width)

    diagonal_variants((n_full >= 1) & (n_full % 2 == 1), odd_tail)
    diagonal_variants((n_full >= 1) & (n_full % 2 == 0), even_tail)

    for gg in range(FOX_PAIR):
        acc = acc_ref[gg]
        out = acc[:, 0:D] / acc[:, D:2 * D]
        ms = jnp.mean(out * out, axis=-1, keepdims=True)
        o_ref[:, gg * D:(gg + 1) * D] = (out * lax.rsqrt(ms + RMS_EPS) * g_ref[...]).astype(o_ref.dtype)


def _fox_attention(proj, gates_c, out_gain, batch, seq_len, tq=512, tk=1024):
    T = proj.shape[0]
    tq = min(tq, seq_len)
    tk = min(tk, seq_len)
    assert tk % tq == 0 and seq_len % tk == 0
    nq = seq_len // tq
    pairs = FOX_HEADS // FOX_PAIR
    W = FOX_PAIR * HEAD_DIM

    once = pl.Buffered(1)

    def rows_spec(base):
        return pl.BlockSpec((seq_len, W), lambda bp, qi: (bp // pairs, base // FOX_PAIR + bp % pairs),
                            pipeline_mode=once)

    return pl.pallas_call(
        functools.partial(_fox_kernel, tq=tq, tk=tk, seq_len=seq_len, pairs=pairs),
        out_shape=jax.ShapeDtypeStruct((T, FOX_WIDTH), BF16),
        grid=(batch * pairs, nq),
        in_specs=[pl.BlockSpec((tq, W), lambda bp, qi: ((bp // pairs) * nq + qi,
                                                         _FQ // FOX_PAIR + bp % pairs)),
                  pl.BlockSpec((tq, W), lambda bp, qi: ((bp // pairs) * nq + jnp.minimum(qi + 1, nq - 1),
                                                         _FQ // FOX_PAIR + bp % pairs)),
                  rows_spec(_FK), rows_spec(_FV),
                  pl.BlockSpec((seq_len, GATE_LANES), lambda bp, qi: (bp // pairs, 0),
                               pipeline_mode=once),
                  pl.BlockSpec((1, HEAD_DIM), lambda bp, qi: (0, 0))],
        out_specs=pl.BlockSpec((tq, W), lambda bp, qi: ((bp // pairs) * nq + qi, bp % pairs)),
        scratch_shapes=[pltpu.VMEM((FOX_PAIR, seq_len, 2 * HEAD_DIM), BF16),
                        pltpu.VMEM((FOX_PAIR, seq_len, 2 * HEAD_DIM), BF16),
                        pltpu.VMEM((FOX_PAIR, tq, HEAD_DIM), F32),
                        pltpu.VMEM((FOX_PAIR, tq, 2 * HEAD_DIM), F32),
                        pltpu.VMEM((FOX_PAIR, tq, tk), F32),
                        pltpu.VMEM((FOX_PAIR, tq, tk), F32),
                        pltpu.VMEM((FOX_PAIR, tq, tk), F32)],
        compiler_params=_params(("parallel", "arbitrary")),
        name="fox_attention",
    )(proj, proj, proj, proj, gates_c, out_gain.reshape(1, HEAD_DIM))


_HALO = 8


def _short_conv_kernel(b_ref, c_ref, h_ref, w_ref, o_ref, buf_ref, *, steps_per_seq, ts):
    t = pl.program_id(1)

    @pl.when(t % steps_per_seq == 0)
    def _():
        buf_ref[0:_HALO, :] = jnp.zeros((_HALO, buf_ref.shape[1]), F32)

    buf_ref[_HALO:_HALO + ts, :] = c_ref[...].astype(F32) * h_ref[...].astype(F32)
    w = w_ref[...]
    y = w[0:1, :] * buf_ref[pl.ds(_HALO - 2, ts), :]
    y += w[1:2, :] * buf_ref[pl.ds(_HALO - 1, ts), :]
    y += w[2:3, :] * buf_ref[pl.ds(_HALO, ts), :]
    o_ref[...] = (b_ref[...].astype(F32) * y).astype(o_ref.dtype)
    buf_ref[0:_HALO, :] = buf_ref[ts:ts + _HALO, :]


def _short_conv(proj, conv_w, seq_len, ts=1024, tc=1024):
    T = proj.shape[0]
    ts = min(ts, seq_len)
    per = tc // HEAD_DIM

    def col(base):
        return pl.BlockSpec((ts, tc), lambda c, t: (t, base // per + c))

    return pl.pallas_call(
        functools.partial(_short_conv_kernel, steps_per_seq=seq_len // ts, ts=ts),
        out_shape=jax.ShapeDtypeStruct((T, SC_WIDTH), BF16),
        grid=(SC_WIDTH // tc, T // ts),
        in_specs=[col(_SB), col(_SC), col(_SH),
                  pl.BlockSpec((SC_KERNEL, tc), lambda c, t: (0, c))],
        out_specs=pl.BlockSpec((ts, tc), lambda c, t: (t, c)),
        scratch_shapes=[pltpu.VMEM((_HALO + ts, tc), F32)],
        compiler_params=_params(("parallel", "arbitrary")),
        name="short_conv",
    )(proj, proj, proj, conv_w)


def _bmm(a, b, dims):
    return lax.dot_general(a.astype(BF16), b.astype(BF16), dims, preferred_element_type=F32)


_B_NN = (((2,), (1,)), ((0,), (0,)))
_B_NT = (((2,), (2,)), ((0,), (0,)))


GDN_PAIR = 4


def _gdn_kernel(q_ref, k_ref, v_ref, z_ref, wq_ref, wk_ref, wv_ref, gc_ref, gr_ref, gain_ref,
                o_ref, qbuf, kbuf, vbuf, state_ref, obuf, *, rows):
    hp = pl.program_id(1)
    t = pl.program_id(2)
    n = rows // CHUNK
    D = HEAD_DIM

    @pl.when(t == 0)
    def _():
        zeros = jnp.zeros((_HALO, GDN_PAIR * D), F32)
        qbuf[0:_HALO, :] = zeros
        kbuf[0:_HALO, :] = zeros
        vbuf[0:_HALO, :] = zeros
        state_ref[...] = jnp.zeros_like(state_ref)

    def conv_silu(x_ref, w_ref, buf):
        buf[_HALO:_HALO + rows, :] = x_ref[...].astype(F32)
        w = w_ref[...]
        y = w[0:1, :] * buf[pl.ds(_HALO - 3, rows), :]
        y += w[1:2, :] * buf[pl.ds(_HALO - 2, rows), :]
        y += w[2:3, :] * buf[pl.ds(_HALO - 1, rows), :]
        y += w[3:4, :] * buf[pl.ds(_HALO, rows), :]
        buf[0:_HALO, :] = buf[rows:rows + _HALO, :]
        return y * jax.nn.sigmoid(y)

    q_all = conv_silu(q_ref, wq_ref, qbuf)
    k_all = conv_silu(k_ref, wk_ref, kbuf)
    v_all = conv_silu(v_ref, wv_ref, vbuf)

    gates = gc_ref[...]
    lane = lax.broadcasted_iota(jnp.int32, gates.shape, 1)
    ri = lax.broadcasted_iota(jnp.int32, (CHUNK, CHUNK), 0)
    ci = lax.broadcasted_iota(jnp.int32, (CHUNK, CHUNK), 1)
    tri_incl = (ci <= ri)[None]
    tri_strict = (ci < ri)[None]
    same16 = ((ri // 16) == (ci // 16))[None]
    same32 = ((ri // 32) == (ci // 32))[None]
    eye = (ri == ci).astype(F32)[None]

    def chunk_terms(gg):
        h = hp * GDN_PAIR + gg
        cols = slice(gg * D, (gg + 1) * D)
        q, k, v = q_all[:, cols], k_all[:, cols], v_all[:, cols]
        q = q * (lax.rsqrt(jnp.sum(q * q, axis=-1, keepdims=True) + RMS_EPS) * (D ** -0.5))
        k = k * lax.rsqrt(jnp.sum(k * k, axis=-1, keepdims=True) + RMS_EPS)
        g_cum = jnp.sum(jnp.where(lane == FOX_HEADS + h, gates, 0.0), axis=-1, keepdims=True)
        beta = jnp.sum(jnp.where(lane == FOX_HEADS + GDN_HEADS + h, gates, 0.0), axis=-1,
                       keepdims=True)
        q3 = q.reshape(n, CHUNK, D)
        k3 = k.reshape(n, CHUNK, D)
        v3 = v.reshape(n, CHUNK, D)
        g3 = g_cum.reshape(n, CHUNK, 1)
        b3 = beta.reshape(n, CHUNK, 1)
        g_row = gr_ref[:, pl.ds(FOX_HEADS + h, 1), :]
        g_last = g3[:, CHUNK - 1:CHUNK, :]

        decay = jnp.exp(jnp.where(tri_incl, g3 - g_row, -jnp.inf))
        kk = _bmm(k3, k3, _B_NT)
        L = jnp.where(tri_strict, b3 * kk * decay, 0.0)

        P = jnp.where(same16, L, 0.0)
        X = eye - P
        P2 = _bmm(P, P, _B_NN)
        X = _bmm(X, eye + P2, _B_NN)
        P4 = _bmm(P2, P2, _B_NN)
        X = _bmm(X, eye + P4, _B_NN)
        P8 = _bmm(P4, P4, _B_NN)
        X = _bmm(X, eye + P8, _B_NN)
        O32 = jnp.where(same32 & jnp.logical_not(same16), L, 0.0)
        X = X - _bmm(_bmm(X, O32, _B_NN), X, _B_NN)
        O64 = jnp.where(same32, 0.0, L)
        X = X - _bmm(_bmm(X, O64, _B_NN), X, _B_NN)

        e3 = jnp.exp(g3)
        rhs = jnp.concatenate([v3 * b3, k3 * (b3 * e3)], axis=-1)
        sol = _bmm(X, rhs, _B_NN)
        attn = jnp.where(tri_incl, _bmm(q3, k3, _B_NT) * decay, 0.0)
        return dict(u=sol[:, :, :D], w=sol[:, :, D:].astype(BF16), q=(q3 * e3).astype(BF16),
                    a=attn.astype(BF16), k=(k3 * jnp.exp(g_last - g3)).astype(BF16),
                    gl=jnp.exp(g_last))

    def pair_terms(p):
        t0, t1 = chunk_terms(2 * p), chunk_terms(2 * p + 1)
        return dict(
            u=jnp.concatenate([t0["u"], t1["u"]], axis=-1),
            wq=jnp.concatenate([jnp.concatenate([t0["w"], t1["w"]], axis=-1),
                                jnp.concatenate([t0["q"], t1["q"]], axis=-1)], axis=1),
            a=jnp.concatenate([t0["a"], t1["a"]], axis=-1),
            k=jnp.concatenate([t0["k"], t1["k"]], axis=-1),
            gl=jnp.concatenate([jnp.broadcast_to(t0["gl"], (n, 1, D)),
                                jnp.broadcast_to(t1["gl"], (n, 1, D))], axis=-1))

    def block_diag(x):
        first = lax.broadcasted_iota(jnp.int32, x.shape, 1) < D
        zero = jnp.zeros_like(x)
        return jnp.concatenate([jnp.where(first, x, zero), jnp.where(first, zero, x)], axis=0)

    n_pairs = GDN_PAIR // 2
    terms = [pair_terms(p) for p in range(n_pairs)]
    S = [state_ref[:, p * 2 * D:(p + 1) * 2 * D] for p in range(n_pairs)]
    for c in range(n):
        for p in range(n_pairs):
            tm = terms[p]
            r1 = jnp.dot(tm["wq"][c], block_diag(S[p].astype(BF16)), preferred_element_type=F32)
            v_b = (tm["u"][c] - r1[0:CHUNK]).astype(BF16)
            o_c = r1[CHUNK:2 * CHUNK] + jnp.dot(tm["a"][c], block_diag(v_b),
                                                preferred_element_type=F32)
            kv = lax.dot_general(tm["k"][c], v_b, _TN, preferred_element_type=F32)
            S[p] = S[p] * tm["gl"][c] + jnp.concatenate([kv[0:D, 0:D], kv[D:2 * D, D:2 * D]],
                                                         axis=1)
            obuf[c * CHUNK:(c + 1) * CHUNK, p * 2 * D:(p + 1) * 2 * D] = o_c
    for p in range(n_pairs):
        state_ref[:, p * 2 * D:(p + 1) * 2 * D] = S[p]

    for gg in range(GDN_PAIR):
        cols = slice(gg * D, (gg + 1) * D)
        o = obuf[:, cols]
        ms = jnp.mean(o * o, axis=-1, keepdims=True)
        z = z_ref[:, cols].astype(F32)
        o = o * lax.rsqrt(ms + RMS_EPS) * gain_ref[...] * (z * jax.nn.sigmoid(z))
        o_ref[:, cols] = o.astype(o_ref.dtype)


def _gdn(proj, proj_z, conv_w, gates_c, gates_r, out_gain, batch, seq_len, rows=1024):
    T = proj.shape[0]
    rows = min(rows, seq_len)
    nt = seq_len // rows
    n = rows // CHUNK
    pairs = GDN_HEADS // GDN_PAIR
    W = GDN_PAIR * HEAD_DIM

    def col(base):
        return pl.BlockSpec((rows, W), lambda b, hp, t: (b * nt + t, base // GDN_PAIR + hp))

    def wcol(base):
        return pl.BlockSpec((GDN_CONV, W), lambda b, hp, t: (0, base // GDN_PAIR + hp))

    return pl.pallas_call(
        functools.partial(_gdn_kernel, rows=rows),
        out_shape=jax.ShapeDtypeStruct((T, GDN_WIDTH), BF16),
        grid=(batch, pairs, nt),
        in_specs=[col(_GQ), col(_GK), col(_GV), col(_GZ),
                  wcol(0), wcol(GDN_HEADS), wcol(2 * GDN_HEADS),
                  pl.BlockSpec((rows, GATE_LANES), lambda b, hp, t: (b * nt + t, 0)),
                  pl.BlockSpec((n, GATE_LANES, CHUNK), lambda b, hp, t: (b * nt + t, 0, 0)),
                  pl.BlockSpec((1, HEAD_DIM), lambda b, hp, t: (0, 0))],
        out_specs=pl.BlockSpec((rows, W), lambda b, hp, t: (b * nt + t, hp)),
        scratch_shapes=[pltpu.VMEM((_HALO + rows, W), F32),
                        pltpu.VMEM((_HALO + rows, W), F32),
                        pltpu.VMEM((_HALO + rows, W), F32),
                        pltpu.VMEM((HEAD_DIM, W), F32),
                        pltpu.VMEM((rows, W), F32)],
        compiler_params=_params(("parallel", "parallel", "arbitrary")),
        name="gdn",
    )(proj, proj, proj, proj_z, conv_w, conv_w, conv_w, gates_c, gates_r,
      out_gain.reshape(1, HEAD_DIM))


_GATE_F_ROWS, _GATE_AB_ROWS = 16, 64


def _gate_rows_kernel(f_ref, ab_ref, *o_refs, depth):
    kt = f_ref.shape[1] // depth
    row = lax.broadcasted_iota(jnp.int32, (GATE_LANES, HEAD_DIM), 0)
    for l in range(depth):
        f = pltpu.einshape("nkl->knl", f_ref[:, pl.ds(l, kt, stride=depth), :])
        ab = pltpu.einshape("nkl->knl", ab_ref[:, pl.ds(l, kt, stride=depth), :])
        for k in range(kt):
            fk = jnp.concatenate([f[k], jnp.zeros((GATE_LANES - _GATE_F_ROWS, HEAD_DIM), F32)], axis=0)
            abk = jnp.concatenate([ab[k], jnp.zeros((GATE_LANES - _GATE_AB_ROWS, HEAD_DIM), F32)],
                                  axis=0)
            w = jnp.where(row < FOX_HEADS, fk,
                          jnp.where(row < FOX_HEADS + 2 * GDN_HEADS, abk, 0.0))
            o_refs[l][:, k * HEAD_DIM:(k + 1) * HEAD_DIM] = w.astype(BF16)


def _small_gate_weights_t(w_view, depth):
    p, rows, lanes = w_view.shape
    d = rows // depth * lanes
    f_col, ab_col = _A_WIDTH, _B_COL + _B_WIDTH
    assert f_col % _GATE_F_ROWS == 0 and ab_col % _GATE_AB_ROWS == FOX_HEADS
    return pl.pallas_call(
        functools.partial(_gate_rows_kernel, depth=depth),
        out_shape=[jax.ShapeDtypeStruct((GATE_LANES, d), BF16)] * depth,
        grid=(1,),
        in_specs=[pl.BlockSpec((_GATE_F_ROWS, rows, lanes), lambda i: (f_col // _GATE_F_ROWS, 0, 0)),
                  pl.BlockSpec((_GATE_AB_ROWS, rows, lanes), lambda i: (ab_col // _GATE_AB_ROWS, 0, 0))],
        out_specs=[pl.BlockSpec((GATE_LANES, d), lambda i: (0, 0))] * depth,
        compiler_params=_params(("arbitrary",)),
        name="gate_rows",
    )(w_view, w_view)


def _gate_columns(fox_forget_bias, gdn_a_log, gdn_dt_bias):
    pad = GATE_LANES - FOX_HEADS - GDN_HEADS
    bias = jnp.concatenate([fox_forget_bias, gdn_dt_bias, jnp.zeros((pad,), F32)])
    alog = jnp.concatenate([jnp.zeros((FOX_HEADS,), F32), gdn_a_log, jnp.zeros((pad,), F32)])
    return bias.reshape(GATE_LANES, 1), alog.reshape(GATE_LANES, 1)


def _layer(x, u, batch, seq_len, layer, w_in, fox_forget_bias, fox_out_norm, sc_conv_w, gdn_conv_w,
           gdn_a_log, gdn_dt_bias, gdn_out_norm, w_out, mix_post_norm, ffn_pre_norm,
           w_gate, w_up, w_down, ffn_post_norm, next_pre_norm):
    T, D = x.shape
    wa_t, wb_t, wc_t, w_small_t = w_in
    bias_col, alog_col = _gate_columns(fox_forget_bias, gdn_a_log, gdn_dt_bias)

    q_scale = jnp.concatenate([jnp.full((FOX_WIDTH,), FOX_Q_SCALE, F32),
                               jnp.ones((_A_WIDTH - FOX_WIDTH,), F32)]).reshape(1, _A_WIDTH)
    proj_a, w_out_b = _matmul(u, wa_t, BF16, tm=1024, tn=768, name="in_proj_a", w_is_t=True,
                              col_scale=q_scale, carried=_CarriedCast(w_out, layer))
    proj_b, wg = _matmul(u, wb_t, BF16, tm=1024, tn=768, name="in_proj_b", w_is_t=True,
                         carried=_CarriedCast(w_gate, layer))
    proj_c, = _matmul(u, wc_t, BF16, tm=1024, tn=768, name="in_proj_c", w_is_t=True)
    gates_t = _gates(u, w_small_t, bias_col, alog_col, seq_len)
    gates_c = gates_t.T
    gates_r = gates_t.reshape(GATE_LANES, T // CHUNK, CHUNK).transpose(1, 0, 2)

    fox_out = _fox_attention(proj_a, gates_c, fox_out_norm, batch, seq_len)
    sc_out = _short_conv(proj_b, sc_conv_w, seq_len)
    gdn_out = _gdn(proj_b, proj_c, gdn_conv_w, gates_c, gates_r, gdn_out_norm, batch, seq_len)

    y, wu = _out_proj(fox_out, sc_out, gdn_out,
                      w_out_b[:FOX_WIDTH], w_out_b[FOX_WIDTH:FOX_WIDTH + SC_WIDTH],
                      w_out_b[FOX_WIDTH + SC_WIDTH:], _CarriedCast(w_up, layer))
    h, v = _norm_residual(y, x, mix_post_norm, ffn_pre_norm)

    act, wd = _ffn_up(v, wg, wu, _CarriedCast(w_down, layer), tm=2048, tn=256)
    y2, = _matmul(act, wd, BF16, tm=512, tn=512, name="ffn_down")
    return _norm_residual(y2, h, ffn_post_norm, next_pre_norm)


def kernel(x, mix_pre_norm, w_in, fox_forget_bias, fox_out_norm, sc_conv_w, gdn_conv_w, gdn_a_log,
           gdn_dt_bias, gdn_out_norm, w_out, mix_post_norm, ffn_pre_norm, w_gate, w_up, w_down,
           ffn_post_norm):
    B, S, D = x.shape
    depth = w_in.shape[0]
    h = x.reshape(B * S, D)
    u = _norm_cast(h, mix_pre_norm[0])
    w_view = _column_major_view(w_in)
    windows = [_window_t(w_view, depth, col, width)
               for col, width in ((_A_COL, _A_WIDTH), (_B_COL, _B_WIDTH), (_C_COL, _C_WIDTH))]
    windows.append(_small_gate_weights_t(w_view, depth))
    for l in range(depth):
        nxt = mix_pre_norm[l + 1] if l + 1 < depth else None
        w_in_l = tuple(win[l] for win in windows)
        h, u = _layer(h, u, B, S, l, w_in_l, fox_forget_bias[l], fox_out_norm[l], sc_conv_w[l],
                      gdn_conv_w[l], gdn_a_log[l], gdn_dt_bias[l], gdn_out_norm[l], w_out,
                      mix_post_norm[l], ffn_pre_norm[l], w_gate, w_up, w_down,
                      ffn_post_norm[l], nxt)
    return h.reshape(B, S, D)
```

```python
import functools
import math
from typing import NamedTuple

import jax
import jax.numpy as jnp
from jax import lax
from jax.experimental import pallas as pl
from jax.experimental.pallas import tpu as pltpu

F32 = jnp.float32
BF16 = jnp.bfloat16

HEAD_DIM = 128
FOX_HEADS = 12
FOX_WIDTH = FOX_HEADS * HEAD_DIM
SC_WIDTH = 8 * HEAD_DIM
SC_KERNEL = 3
GDN_HEADS = 12
GDN_WIDTH = GDN_HEADS * HEAD_DIM
GDN_CONV = 4
CHUNK = 64
RMS_EPS = 1e-6
GATE_LANES = 128

VMEM_LIMIT_BYTES = 56 * 1024 * 1024

_FQ, _FK, _FV = 0, 12, 24
_SB, _SC, _SH = 0, 8, 16
_GQ, _GK, _GV = 24, 36, 48
_GZ = 0
_A_COL, _A_WIDTH = 0, 3 * FOX_WIDTH
_B_COL, _B_WIDTH = 3 * FOX_WIDTH + FOX_HEADS, 3 * SC_WIDTH + 3 * GDN_WIDTH
_C_COL, _C_WIDTH = _B_COL + _B_WIDTH + 2 * GDN_HEADS, GDN_WIDTH

_NT = (((1,), (1,)), ((), ()))
_TN = (((0,), (0,)), ((), ()))


def _params(sem):
    return pltpu.CompilerParams(dimension_semantics=sem, vmem_limit_bytes=VMEM_LIMIT_BYTES)


def _norm_cast_kernel(x_ref, g_ref, o_ref):
    x = x_ref[...]
    ms = jnp.mean(x * x, axis=-1, keepdims=True)
    o_ref[...] = (x * lax.rsqrt(ms + RMS_EPS) * g_ref[...]).astype(o_ref.dtype)


def _norm_cast(x, gain, tm=256):
    T, D = x.shape
    tm = min(tm, T)
    return pl.pallas_call(
        _norm_cast_kernel,
        out_shape=jax.ShapeDtypeStruct((T, D), BF16),
        grid=(T // tm,),
        in_specs=[pl.BlockSpec((tm, D), lambda i: (i, 0)),
                  pl.BlockSpec((1, D), lambda i: (0, 0))],
        out_specs=pl.BlockSpec((tm, D), lambda i: (i, 0)),
        compiler_params=_params(("parallel",)),
        name="norm_cast",
    )(x, gain.reshape(1, D))


def _norm_residual_kernel(y_ref, x_ref, g_ref, gn_ref, h_ref, *maybe_u_ref):
    y = y_ref[...].astype(F32)
    ms = jnp.mean(y * y, axis=-1, keepdims=True)
    h = x_ref[...] + y * lax.rsqrt(ms + RMS_EPS) * g_ref[...]
    h_ref[...] = h
    if maybe_u_ref:
        ms2 = jnp.mean(h * h, axis=-1, keepdims=True)
        maybe_u_ref[0][...] = (h * lax.rsqrt(ms2 + RMS_EPS) * gn_ref[...]).astype(BF16)


def _norm_residual(y, x, gain, next_gain, tm=256):
    T, D = x.shape
    tm = min(tm, T)
    emit_next = next_gain is not None
    gn = (next_gain if emit_next else gain).reshape(1, D)
    row = pl.BlockSpec((tm, D), lambda i: (i, 0))
    vec = pl.BlockSpec((1, D), lambda i: (0, 0))
    out_shape = [jax.ShapeDtypeStruct((T, D), F32)]
    out_specs = [row]
    if emit_next:
        out_shape.append(jax.ShapeDtypeStruct((T, D), BF16))
        out_specs.append(row)
    res = pl.pallas_call(
        _norm_residual_kernel,
        out_shape=out_shape,
        grid=(T // tm,),
        in_specs=[row, row, vec, vec],
        out_specs=out_specs,
        compiler_params=_params(("parallel",)),
        name="norm_residual",
    )(y, x, gain.reshape(1, D), gn)
    return (res[0], res[1]) if emit_next else (res[0], None)


def _window_t_kernel(a_ref, b_ref, *o_refs, off, depth):
    tn = o_refs[0].shape[0]
    kt = a_ref.shape[1] // depth
    for l in range(depth):
        x = a_ref[:, pl.ds(l, kt, stride=depth), :]
        if off:
            x = jnp.concatenate([x, b_ref[:, pl.ds(l, kt, stride=depth), :]], axis=0)[off:off + tn]
        xt = pltpu.einshape("nkl->knl", x)
        for k in range(kt):
            o_refs[l][:, k * HEAD_DIM:(k + 1) * HEAD_DIM] = xt[k].astype(BF16)


def _column_major_view(w_in):
    depth, d, p = w_in.shape
    kt = d // HEAD_DIM
    v = jnp.transpose(w_in, (2, 0, 1)).reshape(p, depth, kt, HEAD_DIM)
    return jnp.transpose(v, (0, 2, 1, 3)).reshape(p, kt * depth, HEAD_DIM)


def _window_t(w_view, depth, src_col, width, tn=256, tail=64):
    p, rows, lanes = w_view.shape
    d = rows // depth * lanes
    base, off = src_col // tn, src_col % tn
    assert width % tn == 0 and tn % tail == 0 and off <= tail
    return pl.pallas_call(
        functools.partial(_window_t_kernel, off=off, depth=depth),
        out_shape=[jax.ShapeDtypeStruct((width, d), BF16)] * depth,
        grid=(width // tn,),
        in_specs=[pl.BlockSpec((tn, rows, lanes), lambda j: (base + j, 0, 0)),
                  pl.BlockSpec((tail, rows, lanes), lambda j: ((base + j + 1) * (tn // tail), 0, 0))],
        out_specs=[pl.BlockSpec((tn, d), lambda j: (j, 0))] * depth,
        compiler_params=_params(("parallel",)),
        name="window_t",
    )(w_view, w_view)


class _CarriedCast(NamedTuple):
    w_stack: jax.Array
    layer: int


_BF16_SUBLANES = 16


def _carried_tile_rows(rows, steps):
    for t in range(_BF16_SUBLANES, rows, _BF16_SUBLANES):
        if rows % t == 0 and rows // t <= steps:
            return t
    return rows


def _call(kernel, carried, grid, in_specs, out_specs, out_shape, args, name):
    n_in, n_out = len(in_specs), len(out_specs)
    body = kernel
    if carried is not None:
        _, rows, cols = carried.w_stack.shape
        tile_rows = _carried_tile_rows(rows, math.prod(grid))
        n_tiles = rows // tile_rows

        def tile(*g):
            step = 0
            for size, idx in zip(grid, g):
                step = step * size + idx
            return jnp.minimum(step, n_tiles - 1)

        in_specs = in_specs + [pl.BlockSpec((1, tile_rows, cols),
                                            lambda *g: (carried.layer, tile(*g), 0))]
        out_specs = out_specs + [pl.BlockSpec((tile_rows, cols), lambda *g: (tile(*g), 0))]
        out_shape = out_shape + [jax.ShapeDtypeStruct((rows, cols), BF16)]
        args = args + [carried.w_stack]

        def body(*refs):
            src, dst = refs[n_in], refs[n_in + 1 + n_out]
            dst[...] = src[0].astype(dst.dtype)
            kernel(*refs[:n_in], *refs[n_in + 1:n_in + 1 + n_out], *refs[n_in + 2 + n_out:])

    return pl.pallas_call(
        body, out_shape=out_shape, grid=grid, in_specs=in_specs, out_specs=out_specs,
        compiler_params=_params(("arbitrary",) * len(grid)), name=name)(*args)


def _matmul_kernel(x_ref, w_ref, *rest, scaled, w_is_t):
    if w_is_t:
        acc = lax.dot_general(x_ref[...], w_ref[...], _NT, preferred_element_type=F32)
    else:
        acc = jnp.dot(x_ref[...], w_ref[...], preferred_element_type=F32)
    if scaled:
        s_ref, o_ref = rest
        acc = acc * s_ref[...]
    else:
        o_ref, = rest
    o_ref[...] = acc.astype(o_ref.dtype)


def _matmul(x, w, out_dtype, tm, tn, name, col_scale=None, carried=None, w_is_t=False):
    M, K = x.shape
    N = w.shape[0] if w_is_t else w.shape[1]
    tm, tn = min(tm, M), min(tn, N)
    scaled = col_scale is not None
    in_specs = [pl.BlockSpec((tm, K), lambda i, j: (i, 0)),
                pl.BlockSpec((tn, K), lambda i, j: (j, 0)) if w_is_t
                else pl.BlockSpec((K, tn), lambda i, j: (0, j))]
    args = [x, w]
    if scaled:
        in_specs.append(pl.BlockSpec((1, tn), lambda i, j: (0, j)))
        args.append(col_scale)
    return _call(functools.partial(_matmul_kernel, scaled=scaled, w_is_t=w_is_t), carried,
                 (M // tm, N // tn), in_specs, [pl.BlockSpec((tm, tn), lambda i, j: (i, j))],
                 [jax.ShapeDtypeStruct((M, N), out_dtype)], args, name)


def _out_proj_kernel(a1_ref, a2_ref, a3_ref, w1_ref, w2_ref, w3_ref, o_ref):
    acc = jnp.dot(a1_ref[...], w1_ref[...], preferred_element_type=F32)
    acc += jnp.dot(a2_ref[...], w2_ref[...], preferred_element_type=F32)
    acc += jnp.dot(a3_ref[...], w3_ref[...], preferred_element_type=F32)
    o_ref[...] = acc.astype(o_ref.dtype)


def _out_proj(a1, a2, a3, w1, w2, w3, carried, tm=1024, tn=1024):
    T = a1.shape[0]
    N = w1.shape[1]
    tm, tn = min(tm, T), min(tn, N)

    def lhs(a):
        return pl.BlockSpec((tm, a.shape[1]), lambda i, j: (i, 0))

    def rhs(w):
        return pl.BlockSpec((w.shape[0], tn), lambda i, j: (0, j))

    return _call(_out_proj_kernel, carried, (T // tm, N // tn),
                 [lhs(a1), lhs(a2), lhs(a3), rhs(w1), rhs(w2), rhs(w3)],
                 [pl.BlockSpec((tm, tn), lambda i, j: (i, j))],
                 [jax.ShapeDtypeStruct((T, N), BF16)], [a1, a2, a3, w1, w2, w3], "out_proj")


def _ffn_up_kernel(x_ref, wg_ref, wu_ref, o_ref):
    x = x_ref[...]
    g = jnp.dot(x, wg_ref[...], preferred_element_type=F32)
    u = jnp.dot(x, wu_ref[...], preferred_element_type=F32)
    o_ref[...] = (g * jax.nn.sigmoid(g) * u).astype(o_ref.dtype)


def _ffn_up(x, wg, wu, carried, tm=1024, tn=512):
    T, K = x.shape
    N = wg.shape[1]
    tm = min(tm, T)
    return _call(_ffn_up_kernel, carried, (T // tm, pl.cdiv(N, tn)),
                 [pl.BlockSpec((tm, K), lambda i, j: (i, 0)),
                  pl.BlockSpec((K, tn), lambda i, j: (0, j)),
                  pl.BlockSpec((K, tn), lambda i, j: (0, j))],
                 [pl.BlockSpec((tm, tn), lambda i, j: (i, j))],
                 [jax.ShapeDtypeStruct((T, N), BF16)], [x, wg, wu], "ffn_up")


def _gates_kernel(w_ref, u_ref, bias_ref, alog_ref, gt_ref, carry_ref, *, steps_per_seq, ts):
    t = pl.program_id(0)

    @pl.when(t % steps_per_seq == 0)
    def _():
        carry_ref[...] = jnp.zeros_like(carry_ref)

    z = lax.dot_general(w_ref[...], u_ref[...], _NT, preferred_element_type=F32) + bias_ref[...]
    row = lax.broadcasted_iota(jnp.int32, z.shape, 0)
    tail = jnp.log1p(jnp.exp(-jnp.abs(z)))
    log_sig = jnp.minimum(z, 0.0) - tail
    softplus = jnp.maximum(z, 0.0) + tail
    sig = 1.0 / (1.0 + jnp.exp(-z))
    decay = -jnp.exp(alog_ref[...]) * softplus
    val = jnp.where(row < FOX_HEADS, log_sig, jnp.where(row < FOX_HEADS + GDN_HEADS, decay, sig))

    src = lax.broadcasted_iota(jnp.int32, (ts, ts), 0)
    dst = lax.broadcasted_iota(jnp.int32, (ts, ts), 1)
    upper = src <= dst
    same_chunk = (src // CHUNK) == (dst // CHUNK)
    hi = val.astype(BF16)
    rest = val - hi.astype(F32)
    mid = rest.astype(BF16)
    lo = (rest - mid.astype(F32)).astype(BF16)
    pieces = jnp.concatenate([hi, mid, lo], axis=1)

    def prefix_sum(mask):
        ones = mask.astype(BF16)
        return jnp.dot(pieces, jnp.concatenate([ones, ones, ones], axis=0),
                       preferred_element_type=F32)

    cum_all = prefix_sum(upper)
    cum_chunk = prefix_sum(upper & same_chunk)
    cum_all = cum_all + carry_ref[...]
    carry_ref[...] = cum_all[:, ts - 1:ts]
    gt_ref[...] = jnp.where(row < FOX_HEADS, cum_all,
                            jnp.where(row < FOX_HEADS + GDN_HEADS, cum_chunk, val))


def _gates(u, w_small_t, bias_col, alog_col, seq_len, ts=512):
    T, D = u.shape
    ts = min(ts, seq_len)
    return pl.pallas_call(
        functools.partial(_gates_kernel, steps_per_seq=seq_len // ts, ts=ts),
        out_shape=jax.ShapeDtypeStruct((GATE_LANES, T), F32),
        grid=(T // ts,),
        in_specs=[pl.BlockSpec((GATE_LANES, D), lambda t: (0, 0)),
                  pl.BlockSpec((ts, D), lambda t: (t, 0)),
                  pl.BlockSpec((GATE_LANES, 1), lambda t: (0, 0)),
                  pl.BlockSpec((GATE_LANES, 1), lambda t: (0, 0))],
        out_specs=pl.BlockSpec((GATE_LANES, ts), lambda t: (0, t)),
        scratch_shapes=[pltpu.VMEM((GATE_LANES, 1), F32)],
        compiler_params=_params(("arbitrary",)),
        name="gates",
    )(w_small_t, u, bias_col, alog_col)


FOX_Q_SCALE = (HEAD_DIM ** -0.5) * math.log2(math.e)
FOX_PAIR = 2
_FOX_BUILD_ROWS = 512


def _fox_kernel(q_ref, qn_ref, k_ref, v_ref, gc_ref, g_ref, o_ref, kaug, vaug, m_ref, acc_ref,
                s0_ref, sa_ref, sb_ref, *, tq, tk, seq_len, pairs):
    D = HEAD_DIM
    pair = pl.program_id(0) % pairs
    qi = pl.program_id(1)

    @pl.when(qi == 0)
    def _build():
        ri = lax.broadcasted_iota(jnp.int32, (3 * D, D), 0)
        ci = lax.broadcasted_iota(jnp.int32, (3 * D, D), 1)
        ones = jnp.ones((_FOX_BUILD_ROWS, D), BF16)

        def chunk(i, carry):
            r0 = pl.multiple_of(i * _FOX_BUILD_ROWS, _FOX_BUILD_ROWS)
            rows = pl.ds(r0, _FOX_BUILD_ROWS)
            g = gc_ref[rows, :] * (-math.log2(math.e))
            hi = g.astype(BF16)
            r1 = g - hi.astype(F32)
            mid = r1.astype(BF16)
            lo = (r1 - mid.astype(F32)).astype(BF16)
            pieces = jnp.concatenate([hi, mid, lo], axis=1)
            for gg in range(FOX_PAIR):
                h = pair * FOX_PAIR + gg
                sel = (((ri == h) & (ci == 0)) | ((ri == D + h) & (ci == 1))
                       | ((ri == 2 * D + h) & (ci == 2)))
                aug = jnp.dot(pieces, sel.astype(BF16), preferred_element_type=F32)
                kaug[gg, rows, 0:D] = k_ref[rows, gg * D:(gg + 1) * D]
                kaug[gg, rows, D:2 * D] = aug.astype(BF16)
                vaug[gg, rows, 0:D] = v_ref[rows, gg * D:(gg + 1) * D]
                vaug[gg, rows, D:2 * D] = ones
            return carry

        lax.fori_loop(0, seq_len // _FOX_BUILD_ROWS, chunk, 0)

    lane = lax.broadcasted_iota(jnp.int32, (tq, D), 1)
    ones3 = jnp.where(lane < 3, 1.0, 0.0).astype(BF16)
    m_ref[...] = jnp.full_like(m_ref, -jnp.inf)
    acc_ref[...] = jnp.zeros_like(acc_ref)

    def scores(kj, slot_ref, queries=q_ref, width=tk):
        k0 = pl.multiple_of(kj * tk, tk)
        for gg in range(FOX_PAIR):
            q_aug = jnp.concatenate([queries[:, gg * D:(gg + 1) * D], ones3], axis=1)
            slot_ref[gg, :, 0:width] = lax.dot_general(q_aug, kaug[gg, pl.ds(k0, width), :], _NT,
                                                       preferred_element_type=F32)

    def accumulate(kj, slot_ref, masked=False, width=tk):
        k0 = pl.multiple_of(kj * tk, tk)
        for gg in range(FOX_PAIR):
            s = slot_ref[gg, :, 0:width]
            if masked:
                r = qi * tq + lax.broadcasted_iota(jnp.int32, s.shape, 0)
                c = kj * tk + lax.broadcasted_iota(jnp.int32, s.shape, 1)
                s = jnp.where(c <= r, s, -jnp.inf)
            m_prev = m_ref[gg]
            m_new = jnp.maximum(m_prev, jnp.max(s, axis=-1, keepdims=True))
            alpha = jnp.exp2(m_prev - m_new)
            p = jnp.concatenate(
                [jnp.exp2(s[:, j * D:(j + 1) * D] - m_new) for j in range(width // D)],
                axis=1).astype(BF16)
            pv = jnp.dot(p, vaug[gg, pl.ds(k0, width), :], preferred_element_type=F32)
            acc_ref[gg] = jnp.concatenate([alpha, alpha], axis=1) * acc_ref[gg] + pv
            m_ref[gg] = m_new

    n_full = (qi * tq) // tk
    first_prefetching_step = tk // tq

    @pl.when(qi <= first_prefetching_step)
    def _():
        scores(0, s0_ref)

    ratio = tk // tq

    def diagonal_variants(cond, emit):
        for r in range(ratio):
            pl.when(cond & (qi % ratio == r))(functools.partial(emit, (r + 1) * tq))

    diagonal_variants(n_full == 0, lambda width: accumulate(0, s0_ref, masked=True, width=width))

    @pl.when(n_full >= 1)
    def _():
        scores(1, sa_ref)
        accumulate(0, s0_ref)

    def body(i, carry):
        scores(2 * i + 2, sb_ref)
        accumulate(2 * i + 1, sa_ref)
        scores(2 * i + 3, sa_ref)
        accumulate(2 * i + 2, sb_ref)
        return carry

    lax.fori_loop(0, (n_full - 1) // 2, body, 0)

    def odd_tail(width):
        scores(0, s0_ref, qn_ref)
        accumulate(n_full, sa_ref, masked=True, width=width)

    def even_tail(width):
        scores(n_full, sb_ref, width=width)
        accumulate(n_full - 1, sa_ref)
        scores(0, s0_ref, qn_ref)
        accumulate(n_full, sb_ref, masked=True, width=width)

    diagonal_variants((n_full >= 1) & (n_full % 2 == 1), odd_tail)
    diagonal_variants((n_full >= 1) & (n_full % 2 == 0), even_tail)

    for gg in range(FOX_PAIR):
        acc = acc_ref[gg]
        out = acc[:, 0:D] / acc[:, D:2 * D]
        ms = jnp.mean(out * out, axis=-1, keepdims=True)
        o_ref[:, gg * D:(gg + 1) * D] = (out * lax.rsqrt(ms + RMS_EPS) * g_ref[...]).astype(o_ref.dtype)


def _fox_attention(proj, gates_c, out_gain, batch, seq_len, tq=512, tk=1024):
    T = proj.shape[0]
    tq = min(tq, seq_len)
    tk = min(tk, seq_len)
    assert tk % tq == 0 and seq_len % tk == 0
    nq = seq_len // tq
    pairs = FOX_HEADS // FOX_PAIR
    W = FOX_PAIR * HEAD_DIM

    once = pl.Buffered(1)

    def rows_spec(base):
        return pl.BlockSpec((seq_len, W), lambda bp, qi: (bp // pairs, base // FOX_PAIR + bp % pairs),
                            pipeline_mode=once)

    return pl.pallas_call(
        functools.partial(_fox_kernel, tq=tq, tk=tk, seq_len=seq_len, pairs=pairs),
        out_shape=jax.ShapeDtypeStruct((T, FOX_WIDTH), BF16),
        grid=(batch * pairs, nq),
        in_specs=[pl.BlockSpec((tq, W), lambda bp, qi: ((bp // pairs) * nq + qi,
                                                         _FQ // FOX_PAIR + bp % pairs)),
                  pl.BlockSpec((tq, W), lambda bp, qi: ((bp // pairs) * nq + jnp.minimum(qi + 1, nq - 1),
                                                         _FQ // FOX_PAIR + bp % pairs)),
                  rows_spec(_FK), rows_spec(_FV),
                  pl.BlockSpec((seq_len, GATE_LANES), lambda bp, qi: (bp // pairs, 0),
                               pipeline_mode=once),
                  pl.BlockSpec((1, HEAD_DIM), lambda bp, qi: (0, 0))],
        out_specs=pl.BlockSpec((tq, W), lambda bp, qi: ((bp // pairs) * nq + qi, bp % pairs)),
        scratch_shapes=[pltpu.VMEM((FOX_PAIR, seq_len, 2 * HEAD_DIM), BF16),
                        pltpu.VMEM((FOX_PAIR, seq_len, 2 * HEAD_DIM), BF16),
                        pltpu.VMEM((FOX_PAIR, tq, HEAD_DIM), F32),
                        pltpu.VMEM((FOX_PAIR, tq, 2 * HEAD_DIM), F32),
                        pltpu.VMEM((FOX_PAIR, tq, tk), F32),
                        pltpu.VMEM((FOX_PAIR, tq, tk), F32),
                        pltpu.VMEM((FOX_PAIR, tq, tk), F32)],
        compiler_params=_params(("parallel", "arbitrary")),
        name="fox_attention",
    )(proj, proj, proj, proj, gates_c, out_gain.reshape(1, HEAD_DIM))


_HALO = 8


def _short_conv_kernel(b_ref, c_ref, h_ref, w_ref, o_ref, buf_ref, *, steps_per_seq, ts):
    t = pl.program_id(1)

    @pl.when(t % steps_per_seq == 0)
    def _():
        buf_ref[0:_HALO, :] = jnp.zeros((_HALO, buf_ref.shape[1]), F32)

    buf_ref[_HALO:_HALO + ts, :] = c_ref[...].astype(F32) * h_ref[...].astype(F32)
    w = w_ref[...]
    xa = buf_ref[...]
    y = w[2:3, :] * xa[_HALO:_HALO + ts, :]
    y += w[1:2, :] * pltpu.roll(xa, 1, axis=0)[_HALO:_HALO + ts, :]
    y += w[0:1, :] * pltpu.roll(xa, 2, axis=0)[_HALO:_HALO + ts, :]
    o_ref[...] = (b_ref[...].astype(F32) * y).astype(o_ref.dtype)
    buf_ref[0:_HALO, :] = buf_ref[ts:ts + _HALO, :]


def _short_conv(proj, conv_w, seq_len, ts=1024, tc=1024):
    T = proj.shape[0]
    ts = min(ts, seq_len)
    per = tc // HEAD_DIM

    def col(base):
        return pl.BlockSpec((ts, tc), lambda c, t: (t, base // per + c))

    return pl.pallas_call(
        functools.partial(_short_conv_kernel, steps_per_seq=seq_len // ts, ts=ts),
        out_shape=jax.ShapeDtypeStruct((T, SC_WIDTH), BF16),
        grid=(SC_WIDTH // tc, T // ts),
        in_specs=[col(_SB), col(_SC), col(_SH),
                  pl.BlockSpec((SC_KERNEL, tc), lambda c, t: (0, c))],
        out_specs=pl.BlockSpec((ts, tc), lambda c, t: (t, c)),
        scratch_shapes=[pltpu.VMEM((_HALO + ts, tc), F32)],
        compiler_params=_params(("parallel", "arbitrary")),
        name="short_conv",
    )(proj, proj, proj, conv_w)


def _bmm(a, b, dims):
    return lax.dot_general(a.astype(BF16), b.astype(BF16), dims, preferred_element_type=F32)


_B_NN = (((2,), (1,)), ((0,), (0,)))
_B_NT = (((2,), (2,)), ((0,), (0,)))


GDN_PAIR = 4


def _gdn_kernel(q_ref, k_ref, v_ref, z_ref, wq_ref, wk_ref, wv_ref, gc_ref, gr_ref, gain_ref,
                o_ref, qbuf, kbuf, vbuf, state_ref, obuf, *, rows):
    hp = pl.program_id(1)
    t = pl.program_id(2)
    n = rows // CHUNK
    D = HEAD_DIM

    @pl.when(t == 0)
    def _():
        zeros = jnp.zeros((_HALO, GDN_PAIR * D), F32)
        qbuf[0:_HALO, :] = zeros
        kbuf[0:_HALO, :] = zeros
        vbuf[0:_HALO, :] = zeros
        state_ref[...] = jnp.zeros_like(state_ref)

    def conv_silu(x_ref, w_ref, buf):
        buf[_HALO:_HALO + rows, :] = x_ref[...].astype(F32)
        w = w_ref[...]
        xa = buf[...]

        def delayed(j):
            return pltpu.roll(xa, j, axis=0)[_HALO:_HALO + rows, :]

        y = w[3:4, :] * xa[_HALO:_HALO + rows, :]
        y += w[2:3, :] * delayed(1)
        y += w[1:2, :] * delayed(2)
        y += w[0:1, :] * delayed(3)
        buf[0:_HALO, :] = buf[rows:rows + _HALO, :]
        return y * jax.nn.sigmoid(y)

    q_all = conv_silu(q_ref, wq_ref, qbuf)
    k_all = conv_silu(k_ref, wk_ref, kbuf)
    v_all = conv_silu(v_ref, wv_ref, vbuf)

    gates = gc_ref[...]
    lane = lax.broadcasted_iota(jnp.int32, gates.shape, 1)
    ri = lax.broadcasted_iota(jnp.int32, (CHUNK, CHUNK), 0)
    ci = lax.broadcasted_iota(jnp.int32, (CHUNK, CHUNK), 1)
    tri_incl = (ci <= ri)[None]
    tri_strict = (ci < ri)[None]
    same16 = ((ri // 16) == (ci // 16))[None]
    same32 = ((ri // 32) == (ci // 32))[None]
    eye = (ri == ci).astype(F32)[None]

    def chunk_terms(gg):
        h = hp * GDN_PAIR + gg
        cols = slice(gg * D, (gg + 1) * D)
        q, k, v = q_all[:, cols], k_all[:, cols], v_all[:, cols]
        q = q * (lax.rsqrt(jnp.sum(q * q, axis=-1, keepdims=True) + RMS_EPS) * (D ** -0.5))
        k = k * lax.rsqrt(jnp.sum(k * k, axis=-1, keepdims=True) + RMS_EPS)
        g_cum = jnp.sum(jnp.where(lane == FOX_HEADS + h, gates, 0.0), axis=-1, keepdims=True)
        beta = jnp.sum(jnp.where(lane == FOX_HEADS + GDN_HEADS + h, gates, 0.0), axis=-1,
                       keepdims=True)
        q3 = q.reshape(n, CHUNK, D)
        k3 = k.reshape(n, CHUNK, D)
        v3 = v.reshape(n, CHUNK, D)
        g3 = g_cum.reshape(n, CHUNK, 1)
        b3 = beta.reshape(n, CHUNK, 1)
        g_row = gr_ref[:, pl.ds(FOX_HEADS + h, 1), :]
        g_last = g3[:, CHUNK - 1:CHUNK, :]

        decay = jnp.exp(jnp.where(tri_incl, g3 - g_row, -jnp.inf))
        kk = _bmm(k3, k3, _B_NT)
        L = jnp.where(tri_strict, b3 * kk * decay, 0.0)

        P = jnp.where(same16, L, 0.0)
        X = eye - P
        P2 = _bmm(P, P, _B_NN)
        X = _bmm(X, eye + P2, _B_NN)
        P4 = _bmm(P2, P2, _B_NN)
        X = _bmm(X, eye + P4, _B_NN)
        P8 = _bmm(P4, P4, _B_NN)
        X = _bmm(X, eye + P8, _B_NN)
        O32 = jnp.where(same32 & jnp.logical_not(same16), L, 0.0)
        X = X - _bmm(_bmm(X, O32, _B_NN), X, _B_NN)
        O64 = jnp.where(same32, 0.0, L)
        X = X - _bmm(_bmm(X, O64, _B_NN), X, _B_NN)

        e3 = jnp.exp(g3)
        rhs = jnp.concatenate([v3 * b3, k3 * (b3 * e3)], axis=-1)
        sol = _bmm(X, rhs, _B_NN)
        attn = jnp.where(tri_incl, _bmm(q3, k3, _B_NT) * decay, 0.0)
        return dict(u=sol[:, :, :D], w=sol[:, :, D:].astype(BF16), q=(q3 * e3).astype(BF16),
                    a=attn.astype(BF16), k=(k3 * jnp.exp(g_last - g3)).astype(BF16),
                    gl=jnp.exp(g_last))

    def pair_terms(p):
        t0, t1 = chunk_terms(2 * p), chunk_terms(2 * p + 1)
        return dict(
            u=jnp.concatenate([t0["u"], t1["u"]], axis=-1),
            wq=jnp.concatenate([jnp.concatenate([t0["w"], t1["w"]], axis=-1),
                                jnp.concatenate([t0["q"], t1["q"]], axis=-1)], axis=1),
            a=jnp.concatenate([t0["a"], t1["a"]], axis=-1),
            k=jnp.concatenate([t0["k"], t1["k"]], axis=-1),
            gl=jnp.concatenate([jnp.broadcast_to(t0["gl"], (n, 1, D)),
                                jnp.broadcast_to(t1["gl"], (n, 1, D))], axis=-1))

    def block_diag(x):
        first = lax.broadcasted_iota(jnp.int32, x.shape, 1) < D
        zero = jnp.zeros_like(x)
        return jnp.concatenate([jnp.where(first, x, zero), jnp.where(first, zero, x)], axis=0)

    n_pairs = GDN_PAIR // 2
    terms = [pair_terms(p) for p in range(n_pairs)]
    S = [state_ref[:, p * 2 * D:(p + 1) * 2 * D] for p in range(n_pairs)]
    for c in range(n):
        for p in range(n_pairs):
            tm = terms[p]
            r1 = jnp.dot(tm["wq"][c], block_diag(S[p].astype(BF16)), preferred_element_type=F32)
            v_b = (tm["u"][c] - r1[0:CHUNK]).astype(BF16)
            o_c = r1[CHUNK:2 * CHUNK] + jnp.dot(tm["a"][c], block_diag(v_b),
                                                preferred_element_type=F32)
            kv = lax.dot_general(tm["k"][c], v_b, _TN, preferred_element_type=F32)
            S[p] = S[p] * tm["gl"][c] + jnp.concatenate([kv[0:D, 0:D], kv[D:2 * D, D:2 * D]],
                                                         axis=1)
            obuf[c * CHUNK:(c + 1) * CHUNK, p * 2 * D:(p + 1) * 2 * D] = o_c
    for p in range(n_pairs):
        state_ref[:, p * 2 * D:(p + 1) * 2 * D] = S[p]

    for gg in range(GDN_PAIR):
        cols = slice(gg * D, (gg + 1) * D)
        o = obuf[:, cols]
        ms = jnp.mean(o * o, axis=-1, keepdims=True)
        z = z_ref[:, cols].astype(F32)
        o = o * lax.rsqrt(ms + RMS_EPS) * gain_ref[...] * (z * jax.nn.sigmoid(z))
        o_ref[:, cols] = o.astype(o_ref.dtype)


def _gdn(proj, proj_z, conv_w, gates_c, gates_r, out_gain, batch, seq_len, rows=1024):
    T = proj.shape[0]
    rows = min(rows, seq_len)
    nt = seq_len // rows
    n = rows // CHUNK
    pairs = GDN_HEADS // GDN_PAIR
    W = GDN_PAIR * HEAD_DIM

    def col(base):
        return pl.BlockSpec((rows, W), lambda b, hp, t: (b * nt + t, base // GDN_PAIR + hp))

    def wcol(base):
        return pl.BlockSpec((GDN_CONV, W), lambda b, hp, t: (0, base // GDN_PAIR + hp))

    return pl.pallas_call(
        functools.partial(_gdn_kernel, rows=rows),
        out_shape=jax.ShapeDtypeStruct((T, GDN_WIDTH), BF16),
        grid=(batch, pairs, nt),
        in_specs=[col(_GQ), col(_GK), col(_GV), col(_GZ),
                  wcol(0), wcol(GDN_HEADS), wcol(2 * GDN_HEADS),
                  pl.BlockSpec((rows, GATE_LANES), lambda b, hp, t: (b * nt + t, 0)),
                  pl.BlockSpec((n, GATE_LANES, CHUNK), lambda b, hp, t: (b * nt + t, 0, 0)),
                  pl.BlockSpec((1, HEAD_DIM), lambda b, hp, t: (0, 0))],
        out_specs=pl.BlockSpec((rows, W), lambda b, hp, t: (b * nt + t, hp)),
        scratch_shapes=[pltpu.VMEM((_HALO + rows, W), F32),
                        pltpu.VMEM((_HALO + rows, W), F32),
                        pltpu.VMEM((_HALO + rows, W), F32),
                        pltpu.VMEM((HEAD_DIM, W), F32),
                        pltpu.VMEM((rows, W), F32)],
        compiler_params=_params(("parallel", "parallel", "arbitrary")),
        name="gdn",
    )(proj, proj, proj, proj_z, conv_w, conv_w, conv_w, gates_c, gates_r,
      out_gain.reshape(1, HEAD_DIM))


_GATE_F_ROWS, _GATE_AB_ROWS = 16, 64


def _gate_rows_kernel(f_ref, ab_ref, *o_refs, depth):
    kt = f_ref.shape[1] // depth
    row = lax.broadcasted_iota(jnp.int32, (GATE_LANES, HEAD_DIM), 0)
    for l in range(depth):
        f = pltpu.einshape("nkl->knl", f_ref[:, pl.ds(l, kt, stride=depth), :])
        ab = pltpu.einshape("nkl->knl", ab_ref[:, pl.ds(l, kt, stride=depth), :])
        for k in range(kt):
            fk = jnp.concatenate([f[k], jnp.zeros((GATE_LANES - _GATE_F_ROWS, HEAD_DIM), F32)], axis=0)
            abk = jnp.concatenate([ab[k], jnp.zeros((GATE_LANES - _GATE_AB_ROWS, HEAD_DIM), F32)],
                                  axis=0)
            w = jnp.where(row < FOX_HEADS, fk,
                          jnp.where(row < FOX_HEADS + 2 * GDN_HEADS, abk, 0.0))
            o_refs[l][:, k * HEAD_DIM:(k + 1) * HEAD_DIM] = w.astype(BF16)


def _small_gate_weights_t(w_view, depth):
    p, rows, lanes = w_view.shape
    d = rows // depth * lanes
    f_col, ab_col = _A_WIDTH, _B_COL + _B_WIDTH
    assert f_col % _GATE_F_ROWS == 0 and ab_col % _GATE_AB_ROWS == FOX_HEADS
    return pl.pallas_call(
        functools.partial(_gate_rows_kernel, depth=depth),
        out_shape=[jax.ShapeDtypeStruct((GATE_LANES, d), BF16)] * depth,
        grid=(1,),
        in_specs=[pl.BlockSpec((_GATE_F_ROWS, rows, lanes), lambda i: (f_col // _GATE_F_ROWS, 0, 0)),
                  pl.BlockSpec((_GATE_AB_ROWS, rows, lanes), lambda i: (ab_col // _GATE_AB_ROWS, 0, 0))],
        out_specs=[pl.BlockSpec((GATE_LANES, d), lambda i: (0, 0))] * depth,
        compiler_params=_params(("arbitrary",)),
        name="gate_rows",
    )(w_view, w_view)


def _gate_columns(fox_forget_bias, gdn_a_log, gdn_dt_bias):
    pad = GATE_LANES - FOX_HEADS - GDN_HEADS
    bias = jnp.concatenate([fox_forget_bias, gdn_dt_bias, jnp.zeros((pad,), F32)])
    alog = jnp.concatenate([jnp.zeros((FOX_HEADS,), F32), gdn_a_log, jnp.zeros((pad,), F32)])
    return bias.reshape(GATE_LANES, 1), alog.reshape(GATE_LANES, 1)


def _layer(x, u, batch, seq_len, layer, w_in, fox_forget_bias, fox_out_norm, sc_conv_w, gdn_conv_w,
           gdn_a_log, gdn_dt_bias, gdn_out_norm, w_out, mix_post_norm, ffn_pre_norm,
           w_gate, w_up, w_down, ffn_post_norm, next_pre_norm):
    T, D = x.shape
    wa_t, wb_t, wc_t, w_small_t = w_in
    bias_col, alog_col = _gate_columns(fox_forget_bias, gdn_a_log, gdn_dt_bias)

    q_scale = jnp.concatenate([jnp.full((FOX_WIDTH,), FOX_Q_SCALE, F32),
                               jnp.ones((_A_WIDTH - FOX_WIDTH,), F32)]).reshape(1, _A_WIDTH)
    proj_a, w_out_b = _matmul(u, wa_t, BF16, tm=1024, tn=768, name="in_proj_a", w_is_t=True,
                              col_scale=q_scale, carried=_CarriedCast(w_out, layer))
    proj_b, wg = _matmul(u, wb_t, BF16, tm=1024, tn=768, name="in_proj_b", w_is_t=True,
                         carried=_CarriedCast(w_gate, layer))
    proj_c, = _matmul(u, wc_t, BF16, tm=1024, tn=768, name="in_proj_c", w_is_t=True)
    gates_t = _gates(u, w_small_t, bias_col, alog_col, seq_len)
    gates_c = gates_t.T
    gates_r = gates_t.reshape(GATE_LANES, T // CHUNK, CHUNK).transpose(1, 0, 2)

    fox_out = _fox_attention(proj_a, gates_c, fox_out_norm, batch, seq_len)
    sc_out = _short_conv(proj_b, sc_conv_w, seq_len)
    gdn_out = _gdn(proj_b, proj_c, gdn_conv_w, gates_c, gates_r, gdn_out_norm, batch, seq_len)

    y, wu = _out_proj(fox_out, sc_out, gdn_out,
                      w_out_b[:FOX_WIDTH], w_out_b[FOX_WIDTH:FOX_WIDTH + SC_WIDTH],
                      w_out_b[FOX_WIDTH + SC_WIDTH:], _CarriedCast(w_up, layer))
    h, v = _norm_residual(y, x, mix_post_norm, ffn_pre_norm)

    act, wd = _ffn_up(v, wg, wu, _CarriedCast(w_down, layer), tm=2048, tn=256)
    y2, = _matmul(act, wd, BF16, tm=512, tn=512, name="ffn_down")
    return _norm_residual(y2, h, ffn_post_norm, next_pre_norm)


def kernel(x, mix_pre_norm, w_in, fox_forget_bias, fox_out_norm, sc_conv_w, gdn_conv_w, gdn_a_log,
           gdn_dt_bias, gdn_out_norm, w_out, mix_post_norm, ffn_pre_norm, w_gate, w_up, w_down,
           ffn_post_norm):
    B, S, D = x.shape
    depth = w_in.shape[0]
    h = x.reshape(B * S, D)
    u = _norm_cast(h, mix_pre_norm[0])
    w_view = _column_major_view(w_in)
    windows = [_window_t(w_view, depth, col, width)
               for col, width in ((_A_COL, _A_WIDTH), (_B_COL, _B_WIDTH), (_C_COL, _C_WIDTH))]
    windows.append(_small_gate_weights_t(w_view, depth))
    for l in range(depth):
        nxt = mix_pre_norm[l + 1] if l + 1 < depth else None
        w_in_l = tuple(win[l] for win in windows)
        h, u = _layer(h, u, B, S, l, w_in_l, fox_forget_bias[l], fox_out_norm[l], sc_conv_w[l],
                      gdn_conv_w[l], gdn_a_log[l], gdn_dt_bias[l], gdn_out_norm[l], w_out,
                      mix_post_norm[l], ffn_pre_norm[l], w_gate, w_up, w_down,
                      ffn_post_norm[l], nxt)
    return h.reshape(B, S, D)
```

```python
import functools
import math
from typing import NamedTuple

import jax
import jax.numpy as jnp
from jax import lax
from jax.experimental import pallas as pl
from jax.experimental.pallas import tpu as pltpu

F32 = jnp.float32
BF16 = jnp.bfloat16

HEAD_DIM = 128
FOX_HEADS = 12
FOX_WIDTH = FOX_HEADS * HEAD_DIM
SC_WIDTH = 8 * HEAD_DIM
SC_KERNEL = 3
GDN_HEADS = 12
GDN_WIDTH = GDN_HEADS * HEAD_DIM
GDN_CONV = 4
CHUNK = 64
RMS_EPS = 1e-6
GATE_LANES = 128

VMEM_LIMIT_BYTES = 56 * 1024 * 1024

_FQ, _FK, _FV = 0, 12, 24
_SB, _SC, _SH = 0, 8, 16
_GQ, _GK, _GV = 24, 36, 48
_GZ = 0
_A_COL, _A_WIDTH = 0, 3 * FOX_WIDTH
_B_COL, _B_WIDTH = 3 * FOX_WIDTH + FOX_HEADS, 3 * SC_WIDTH + 3 * GDN_WIDTH
_C_COL, _C_WIDTH = _B_COL + _B_WIDTH + 2 * GDN_HEADS, GDN_WIDTH

_NT = (((1,), (1,)), ((), ()))
_TN = (((0,), (0,)), ((), ()))


def _params(sem):
    return pltpu.CompilerParams(dimension_semantics=sem, vmem_limit_bytes=VMEM_LIMIT_BYTES)


def _norm_cast_kernel(x_ref, g_ref, o_ref):
    x = x_ref[...]
    ms = jnp.mean(x * x, axis=-1, keepdims=True)
    o_ref[...] = (x * lax.rsqrt(ms + RMS_EPS) * g_ref[...]).astype(o_ref.dtype)


def _norm_cast(x, gain, tm=256):
    T, D = x.shape
    tm = min(tm, T)
    return pl.pallas_call(
        _norm_cast_kernel,
        out_shape=jax.ShapeDtypeStruct((T, D), BF16),
        grid=(T // tm,),
        in_specs=[pl.BlockSpec((tm, D), lambda i: (i, 0)),
                  pl.BlockSpec((1, D), lambda i: (0, 0))],
        out_specs=pl.BlockSpec((tm, D), lambda i: (i, 0)),
        compiler_params=_params(("parallel",)),
        name="norm_cast",
    )(x, gain.reshape(1, D))


def _norm_residual_kernel(y_ref, x_ref, g_ref, gn_ref, h_ref, *maybe_u_ref):
    y = y_ref[...].astype(F32)
    ms = jnp.mean(y * y, axis=-1, keepdims=True)
    h = x_ref[...] + y * lax.rsqrt(ms + RMS_EPS) * g_ref[...]
    h_ref[...] = h
    if maybe_u_ref:
        ms2 = jnp.mean(h * h, axis=-1, keepdims=True)
        maybe_u_ref[0][...] = (h * lax.rsqrt(ms2 + RMS_EPS) * gn_ref[...]).astype(BF16)


def _norm_residual(y, x, gain, next_gain, tm=256):
    T, D = x.shape
    tm = min(tm, T)
    emit_next = next_gain is not None
    gn = (next_gain if emit_next else gain).reshape(1, D)
    row = pl.BlockSpec((tm, D), lambda i: (i, 0))
    vec = pl.BlockSpec((1, D), lambda i: (0, 0))
    out_shape = [jax.ShapeDtypeStruct((T, D), F32)]
    out_specs = [row]
    if emit_next:
        out_shape.append(jax.ShapeDtypeStruct((T, D), BF16))
        out_specs.append(row)
    res = pl.pallas_call(
        _norm_residual_kernel,
        out_shape=out_shape,
        grid=(T // tm,),
        in_specs=[row, row, vec, vec],
        out_specs=out_specs,
        compiler_params=_params(("parallel",)),
        name="norm_residual",
    )(y, x, gain.reshape(1, D), gn)
    return (res[0], res[1]) if emit_next else (res[0], None)


def _window_t_kernel(a_ref, b_ref, *o_refs, off, depth):
    tn = o_refs[0].shape[0]
    kt = a_ref.shape[1] // depth
    for l in range(depth):
        x = a_ref[:, pl.ds(l, kt, stride=depth), :]
        if off:
            x = jnp.concatenate([x, b_ref[:, pl.ds(l, kt, stride=depth), :]], axis=0)[off:off + tn]
        xt = pltpu.einshape("nkl->knl", x)
        for k in range(kt):
            o_refs[l][:, k * HEAD_DIM:(k + 1) * HEAD_DIM] = xt[k].astype(BF16)


def _column_major_view(w_in):
    depth, d, p = w_in.shape
    kt = d // HEAD_DIM
    v = jnp.transpose(w_in, (2, 0, 1)).reshape(p, depth, kt, HEAD_DIM)
    return jnp.transpose(v, (0, 2, 1, 3)).reshape(p, kt * depth, HEAD_DIM)


def _window_t(w_view, depth, src_col, width, tn=256, tail=64):
    p, rows, lanes = w_view.shape
    d = rows // depth * lanes
    base, off = src_col // tn, src_col % tn
    assert width % tn == 0 and tn % tail == 0 and off <= tail
    return pl.pallas_call(
        functools.partial(_window_t_kernel, off=off, depth=depth),
        out_shape=[jax.ShapeDtypeStruct((width, d), BF16)] * depth,
        grid=(width // tn,),
        in_specs=[pl.BlockSpec((tn, rows, lanes), lambda j: (base + j, 0, 0)),
                  pl.BlockSpec((tail, rows, lanes), lambda j: ((base + j + 1) * (tn // tail), 0, 0))],
        out_specs=[pl.BlockSpec((tn, d), lambda j: (j, 0))] * depth,
        compiler_params=_params(("parallel",)),
        name="window_t",
    )(w_view, w_view)


class _CarriedCast(NamedTuple):
    w_stack: jax.Array
    layer: int


_BF16_SUBLANES = 16


def _carried_tile_rows(rows, steps):
    for t in range(_BF16_SUBLANES, rows, _BF16_SUBLANES):
        if rows % t == 0 and rows // t <= steps:
            return t
    return rows


def _call(kernel, carried, grid, in_specs, out_specs, out_shape, args, name):
    n_in, n_out = len(in_specs), len(out_specs)
    body = kernel
    if carried is not None:
        _, rows, cols = carried.w_stack.shape
        tile_rows = _carried_tile_rows(rows, math.prod(grid))
        n_tiles = rows // tile_rows

        def tile(*g):
            step = 0
            for size, idx in zip(grid, g):
                step = step * size + idx
            return jnp.minimum(step, n_tiles - 1)

        in_specs = in_specs + [pl.BlockSpec((1, tile_rows, cols),
                                            lambda *g: (carried.layer, tile(*g), 0))]
        out_specs = out_specs + [pl.BlockSpec((tile_rows, cols), lambda *g: (tile(*g), 0))]
        out_shape = out_shape + [jax.ShapeDtypeStruct((rows, cols), BF16)]
        args = args + [carried.w_stack]

        def body(*refs):
            src, dst = refs[n_in], refs[n_in + 1 + n_out]
            dst[...] = src[0].astype(dst.dtype)
            kernel(*refs[:n_in], *refs[n_in + 1:n_in + 1 + n_out], *refs[n_in + 2 + n_out:])

    return pl.pallas_call(
        body, out_shape=out_shape, grid=grid, in_specs=in_specs, out_specs=out_specs,
        compiler_params=_params(("arbitrary",) * len(grid)), name=name)(*args)


def _matmul_kernel(x_ref, w_ref, *rest, scaled, w_is_t):
    if w_is_t:
        acc = lax.dot_general(x_ref[...], w_ref[...], _NT, preferred_element_type=F32)
    else:
        acc = jnp.dot(x_ref[...], w_ref[...], preferred_element_type=F32)
    if scaled:
        s_ref, o_ref = rest
        acc = acc * s_ref[...]
    else:
        o_ref, = rest
    o_ref[...] = acc.astype(o_ref.dtype)


def _matmul(x, w, out_dtype, tm, tn, name, col_scale=None, carried=None, w_is_t=False):
    M, K = x.shape
    N = w.shape[0] if w_is_t else w.shape[1]
    tm, tn = min(tm, M), min(tn, N)
    scaled = col_scale is not None
    in_specs = [pl.BlockSpec((tm, K), lambda i, j: (i, 0)),
                pl.BlockSpec((tn, K), lambda i, j: (j, 0)) if w_is_t
                else pl.BlockSpec((K, tn), lambda i, j: (0, j))]
    args = [x, w]
    if scaled:
        in_specs.append(pl.BlockSpec((1, tn), lambda i, j: (0, j)))
        args.append(col_scale)
    return _call(functools.partial(_matmul_kernel, scaled=scaled, w_is_t=w_is_t), carried,
                 (M // tm, N // tn), in_specs, [pl.BlockSpec((tm, tn), lambda i, j: (i, j))],
                 [jax.ShapeDtypeStruct((M, N), out_dtype)], args, name)


def _out_proj_kernel(a1_ref, a2_ref, a3_ref, w1_ref, w2_ref, w3_ref, o_ref):
    acc = jnp.dot(a1_ref[...], w1_ref[...], preferred_element_type=F32)
    acc += jnp.dot(a2_ref[...], w2_ref[...], preferred_element_type=F32)
    acc += jnp.dot(a3_ref[...], w3_ref[...], preferred_element_type=F32)
    o_ref[...] = acc.astype(o_ref.dtype)


def _out_proj(a1, a2, a3, w1, w2, w3, carried, tm=1024, tn=1024):
    T = a1.shape[0]
    N = w1.shape[1]
    tm, tn = min(tm, T), min(tn, N)

    def lhs(a):
        return pl.BlockSpec((tm, a.shape[1]), lambda i, j: (i, 0))

    def rhs(w):
        return pl.BlockSpec((w.shape[0], tn), lambda i, j: (0, j))

    return _call(_out_proj_kernel, carried, (T // tm, N // tn),
                 [lhs(a1), lhs(a2), lhs(a3), rhs(w1), rhs(w2), rhs(w3)],
                 [pl.BlockSpec((tm, tn), lambda i, j: (i, j))],
                 [jax.ShapeDtypeStruct((T, N), BF16)], [a1, a2, a3, w1, w2, w3], "out_proj")


def _ffn_up_kernel(x_ref, wg_ref, wu_ref, o_ref):
    x = x_ref[...]
    g = jnp.dot(x, wg_ref[...], preferred_element_type=F32)
    u = jnp.dot(x, wu_ref[...], preferred_element_type=F32)
    o_ref[...] = (g * jax.nn.sigmoid(g) * u).astype(o_ref.dtype)


def _ffn_up(x, wg, wu, carried, tm=1024, tn=512):
    T, K = x.shape
    N = wg.shape[1]
    tm = min(tm, T)
    return _call(_ffn_up_kernel, carried, (T // tm, pl.cdiv(N, tn)),
                 [pl.BlockSpec((tm, K), lambda i, j: (i, 0)),
                  pl.BlockSpec((K, tn), lambda i, j: (0, j)),
                  pl.BlockSpec((K, tn), lambda i, j: (0, j))],
                 [pl.BlockSpec((tm, tn), lambda i, j: (i, j))],
                 [jax.ShapeDtypeStruct((T, N), BF16)], [x, wg, wu], "ffn_up")


def _gates_kernel(w_ref, u_ref, bias_ref, alog_ref, gt_ref, carry_ref, *, steps_per_seq, ts):
    t = pl.program_id(0)

    @pl.when(t % steps_per_seq == 0)
    def _():
        carry_ref[...] = jnp.zeros_like(carry_ref)

    z = lax.dot_general(w_ref[...], u_ref[...], _NT, preferred_element_type=F32) + bias_ref[...]
    row = lax.broadcasted_iota(jnp.int32, z.shape, 0)
    tail = jnp.log1p(jnp.exp(-jnp.abs(z)))
    log_sig = jnp.minimum(z, 0.0) - tail
    softplus = jnp.maximum(z, 0.0) + tail
    sig = 1.0 / (1.0 + jnp.exp(-z))
    decay = -jnp.exp(alog_ref[...]) * softplus
    val = jnp.where(row < FOX_HEADS, log_sig, jnp.where(row < FOX_HEADS + GDN_HEADS, decay, sig))

    src = lax.broadcasted_iota(jnp.int32, (ts, ts), 0)
    dst = lax.broadcasted_iota(jnp.int32, (ts, ts), 1)
    upper = src <= dst
    same_chunk = (src // CHUNK) == (dst // CHUNK)
    hi = val.astype(BF16)
    rest = val - hi.astype(F32)
    mid = rest.astype(BF16)
    lo = (rest - mid.astype(F32)).astype(BF16)
    pieces = jnp.concatenate([hi, mid, lo], axis=1)

    def prefix_sum(mask):
        ones = mask.astype(BF16)
        return jnp.dot(pieces, jnp.concatenate([ones, ones, ones], axis=0),
                       preferred_element_type=F32)

    cum_all = prefix_sum(upper)
    cum_chunk = prefix_sum(upper & same_chunk)
    cum_all = cum_all + carry_ref[...]
    carry_ref[...] = cum_all[:, ts - 1:ts]
    gt_ref[...] = jnp.where(row < FOX_HEADS, cum_all,
                            jnp.where(row < FOX_HEADS + GDN_HEADS, cum_chunk, val))


def _gates(u, w_small_t, bias_col, alog_col, seq_len, ts=512):
    T, D = u.shape
    ts = min(ts, seq_len)
    return pl.pallas_call(
        functools.partial(_gates_kernel, steps_per_seq=seq_len // ts, ts=ts),
        out_shape=jax.ShapeDtypeStruct((GATE_LANES, T), F32),
        grid=(T // ts,),
        in_specs=[pl.BlockSpec((GATE_LANES, D), lambda t: (0, 0)),
                  pl.BlockSpec((ts, D), lambda t: (t, 0)),
                  pl.BlockSpec((GATE_LANES, 1), lambda t: (0, 0)),
                  pl.BlockSpec((GATE_LANES, 1), lambda t: (0, 0))],
        out_specs=pl.BlockSpec((GATE_LANES, ts), lambda t: (0, t)),
        scratch_shapes=[pltpu.VMEM((GATE_LANES, 1), F32)],
        compiler_params=_params(("arbitrary",)),
        name="gates",
    )(w_small_t, u, bias_col, alog_col)


FOX_Q_SCALE = (HEAD_DIM ** -0.5) * math.log2(math.e)
FOX_PAIR = 2
_FOX_BUILD_ROWS = 512


def _fox_kernel(q_ref, qn_ref, k_ref, v_ref, gc_ref, g_ref, o_ref, kaug, vaug, m_ref, acc_ref,
                s0_ref, sa_ref, sb_ref, *, tq, tk, seq_len, pairs):
    D = HEAD_DIM
    pair = pl.program_id(0) % pairs
    qi = pl.program_id(1)

    @pl.when(qi == 0)
    def _build():
        ri = lax.broadcasted_iota(jnp.int32, (3 * D, FOX_PAIR * D), 0)
        ci = lax.broadcasted_iota(jnp.int32, (3 * D, FOX_PAIR * D), 1)
        sel = ri < 0
        for gg in range(FOX_PAIR):
            for j in range(3):
                sel |= (ri == j * D + pair * FOX_PAIR + gg) & (ci == gg * D + j)
        sel = sel.astype(BF16)
        ones = jnp.ones((_FOX_BUILD_ROWS, D), BF16)

        def chunk(i, carry):
            r0 = pl.multiple_of(i * _FOX_BUILD_ROWS, _FOX_BUILD_ROWS)
            rows = pl.ds(r0, _FOX_BUILD_ROWS)
            g = gc_ref[rows, :] * (-math.log2(math.e))
            hi = g.astype(BF16)
            r1 = g - hi.astype(F32)
            mid = r1.astype(BF16)
            lo = (r1 - mid.astype(F32)).astype(BF16)
            pieces = jnp.concatenate([hi, mid, lo], axis=1)
            aug = jnp.dot(pieces, sel, preferred_element_type=F32).astype(BF16)
            for gg in range(FOX_PAIR):
                kaug[gg, rows, 0:D] = k_ref[rows, gg * D:(gg + 1) * D]
                kaug[gg, rows, D:2 * D] = aug[:, gg * D:(gg + 1) * D]
                vaug[gg, rows, 0:D] = v_ref[rows, gg * D:(gg + 1) * D]
                vaug[gg, rows, D:2 * D] = ones
            return carry

        lax.fori_loop(0, seq_len // _FOX_BUILD_ROWS, chunk, 0, unroll=8)

    lane = lax.broadcasted_iota(jnp.int32, (tq, D), 1)
    ones3 = jnp.where(lane < 3, 1.0, 0.0).astype(BF16)
    m_ref[...] = jnp.full_like(m_ref, -jnp.inf)
    acc_ref[...] = jnp.zeros_like(acc_ref)

    def scores(kj, slot_ref, queries=q_ref, width=tk):
        k0 = pl.multiple_of(kj * tk, tk)
        for gg in range(FOX_PAIR):
            q_aug = jnp.concatenate([queries[:, gg * D:(gg + 1) * D], ones3], axis=1)
            slot_ref[gg, :, 0:width] = lax.dot_general(q_aug, kaug[gg, pl.ds(k0, width), :], _NT,
                                                       preferred_element_type=F32)

    def accumulate(kj, slot_ref, masked=False, width=tk):
        k0 = pl.multiple_of(kj * tk, tk)
        for gg in range(FOX_PAIR):
            s = slot_ref[gg, :, 0:width]
            if masked:
                r = qi * tq + lax.broadcasted_iota(jnp.int32, s.shape, 0)
                c = kj * tk + lax.broadcasted_iota(jnp.int32, s.shape, 1)
                s = jnp.where(c <= r, s, -jnp.inf)
            m_prev = m_ref[gg]
            m_new = jnp.maximum(m_prev, jnp.max(s, axis=-1, keepdims=True))
            alpha = jnp.exp2(m_prev - m_new)
            p = jnp.concatenate(
                [jnp.exp2(s[:, j * D:(j + 1) * D] - m_new) for j in range(width // D)],
                axis=1).astype(BF16)
            pv = jnp.dot(p, vaug[gg, pl.ds(k0, width), :], preferred_element_type=F32)
            acc_ref[gg] = jnp.concatenate([alpha, alpha], axis=1) * acc_ref[gg] + pv
            m_ref[gg] = m_new

    n_full = (qi * tq) // tk
    first_prefetching_step = tk // tq

    @pl.when(qi <= first_prefetching_step)
    def _():
        scores(0, s0_ref)

    ratio = tk // tq

    def diagonal_variants(cond, emit):
        for r in range(ratio):
            pl.when(cond & (qi % ratio == r))(functools.partial(emit, (r + 1) * tq))

    diagonal_variants(n_full == 0, lambda width: accumulate(0, s0_ref, masked=True, width=width))

    @pl.when(n_full >= 1)
    def _():
        scores(1, sa_ref)
        accumulate(0, s0_ref)

    def body(i, carry):
        scores(2 * i + 2, sb_ref)
        accumulate(2 * i + 1, sa_ref)
        scores(2 * i + 3, sa_ref)
        accumulate(2 * i + 2, sb_ref)
        return carry

    lax.fori_loop(0, (n_full - 1) // 2, body, 0)

    def odd_tail(width):
        scores(0, s0_ref, qn_ref)
        accumulate(n_full, sa_ref, masked=True, width=width)

    def even_tail(width):
        scores(n_full, sb_ref, width=width)
        accumulate(n_full - 1, sa_ref)
        scores(0, s0_ref, qn_ref)
        accumulate(n_full, sb_ref, masked=True, width=width)

    diagonal_variants((n_full >= 1) & (n_full % 2 == 1), odd_tail)
    diagonal_variants((n_full >= 1) & (n_full % 2 == 0), even_tail)

    for gg in range(FOX_PAIR):
        acc = acc_ref[gg]
        out = acc[:, 0:D] / acc[:, D:2 * D]
        ms = jnp.mean(out * out, axis=-1, keepdims=True)
        o_ref[:, gg * D:(gg + 1) * D] = (out * lax.rsqrt(ms + RMS_EPS) * g_ref[...]).astype(o_ref.dtype)


def _fox_attention(proj, gates_c, out_gain, batch, seq_len, tq=512, tk=1024):
    T = proj.shape[0]
    tq = min(tq, seq_len)
    tk = min(tk, seq_len)
    assert tk % tq == 0 and seq_len % tk == 0
    nq = seq_len // tq
    pairs = FOX_HEADS // FOX_PAIR
    W = FOX_PAIR * HEAD_DIM

    once = pl.Buffered(1)

    def rows_spec(base):
        return pl.BlockSpec((seq_len, W), lambda bp, qi: (bp // pairs, base // FOX_PAIR + bp % pairs),
                            pipeline_mode=once)

    return pl.pallas_call(
        functools.partial(_fox_kernel, tq=tq, tk=tk, seq_len=seq_len, pairs=pairs),
        out_shape=jax.ShapeDtypeStruct((T, FOX_WIDTH), BF16),
        grid=(batch * pairs, nq),
        in_specs=[pl.BlockSpec((tq, W), lambda bp, qi: ((bp // pairs) * nq + qi,
                                                         _FQ // FOX_PAIR + bp % pairs)),
                  pl.BlockSpec((tq, W), lambda bp, qi: ((bp // pairs) * nq + jnp.minimum(qi + 1, nq - 1),
                                                         _FQ // FOX_PAIR + bp % pairs)),
                  rows_spec(_FK), rows_spec(_FV),
                  pl.BlockSpec((seq_len, GATE_LANES), lambda bp, qi: (bp // pairs, 0),
                               pipeline_mode=once),
                  pl.BlockSpec((1, HEAD_DIM), lambda bp, qi: (0, 0))],
        out_specs=pl.BlockSpec((tq, W), lambda bp, qi: ((bp // pairs) * nq + qi, bp % pairs)),
        scratch_shapes=[pltpu.VMEM((FOX_PAIR, seq_len, 2 * HEAD_DIM), BF16),
                        pltpu.VMEM((FOX_PAIR, seq_len, 2 * HEAD_DIM), BF16),
                        pltpu.VMEM((FOX_PAIR, tq, HEAD_DIM), F32),
                        pltpu.VMEM((FOX_PAIR, tq, 2 * HEAD_DIM), F32),
                        pltpu.VMEM((FOX_PAIR, tq, tk), F32),
                        pltpu.VMEM((FOX_PAIR, tq, tk), F32),
                        pltpu.VMEM((FOX_PAIR, tq, tk), F32)],
        compiler_params=_params(("parallel", "arbitrary")),
        name="fox_attention",
    )(proj, proj, proj, proj, gates_c, out_gain.reshape(1, HEAD_DIM))


_HALO = 8


def _short_conv_kernel(b_ref, c_ref, h_ref, w_ref, o_ref, buf_ref, *, steps_per_seq, ts):
    t = pl.program_id(1)

    @pl.when(t % steps_per_seq == 0)
    def _():
        buf_ref[0:_HALO, :] = jnp.zeros((_HALO, buf_ref.shape[1]), F32)

    buf_ref[_HALO:_HALO + ts, :] = c_ref[...].astype(F32) * h_ref[...].astype(F32)
    w = w_ref[...]
    xa = buf_ref[...]
    y = w[2:3, :] * xa[_HALO:_HALO + ts, :]
    y += w[1:2, :] * pltpu.roll(xa, 1, axis=0)[_HALO:_HALO + ts, :]
    y += w[0:1, :] * pltpu.roll(xa, 2, axis=0)[_HALO:_HALO + ts, :]
    o_ref[...] = (b_ref[...].astype(F32) * y).astype(o_ref.dtype)
    buf_ref[0:_HALO, :] = buf_ref[ts:ts + _HALO, :]


def _short_conv(proj, conv_w, seq_len, ts=1024, tc=1024):
    T = proj.shape[0]
    ts = min(ts, seq_len)
    per = tc // HEAD_DIM

    def col(base):
        return pl.BlockSpec((ts, tc), lambda c, t: (t, base // per + c))

    return pl.pallas_call(
        functools.partial(_short_conv_kernel, steps_per_seq=seq_len // ts, ts=ts),
        out_shape=jax.ShapeDtypeStruct((T, SC_WIDTH), BF16),
        grid=(SC_WIDTH // tc, T // ts),
        in_specs=[col(_SB), col(_SC), col(_SH),
                  pl.BlockSpec((SC_KERNEL, tc), lambda c, t: (0, c))],
        out_specs=pl.BlockSpec((ts, tc), lambda c, t: (t, c)),
        scratch_shapes=[pltpu.VMEM((_HALO + ts, tc), F32)],
        compiler_params=_params(("parallel", "arbitrary")),
        name="short_conv",
    )(proj, proj, proj, conv_w)


def _bmm(a, b, dims):
    return lax.dot_general(a.astype(BF16), b.astype(BF16), dims, preferred_element_type=F32)


_B_NN = (((2,), (1,)), ((0,), (0,)))
_B_NT = (((2,), (2,)), ((0,), (0,)))


GDN_PAIR = 4


def _gdn_kernel(q_ref, k_ref, v_ref, z_ref, wq_ref, wk_ref, wv_ref, gc_ref, gr_ref, gain_ref,
                o_ref, qbuf, kbuf, vbuf, state_ref, obuf, *, rows):
    hp = pl.program_id(1)
    t = pl.program_id(2)
    n = rows // CHUNK
    D = HEAD_DIM

    @pl.when(t == 0)
    def _():
        zeros = jnp.zeros((_HALO, GDN_PAIR * D), F32)
        qbuf[0:_HALO, :] = zeros
        kbuf[0:_HALO, :] = zeros
        vbuf[0:_HALO, :] = zeros
        state_ref[...] = jnp.zeros_like(state_ref)

    def conv_silu(x_ref, w_ref, buf):
        buf[_HALO:_HALO + rows, :] = x_ref[...].astype(F32)
        w = w_ref[...]
        xa = buf[...]

        def delayed(j):
            return pltpu.roll(xa, j, axis=0)[_HALO:_HALO + rows, :]

        y = w[3:4, :] * xa[_HALO:_HALO + rows, :]
        y += w[2:3, :] * delayed(1)
        y += w[1:2, :] * delayed(2)
        y += w[0:1, :] * delayed(3)
        buf[0:_HALO, :] = buf[rows:rows + _HALO, :]
        return y * jax.nn.sigmoid(y)

    q_all = conv_silu(q_ref, wq_ref, qbuf)
    k_all = conv_silu(k_ref, wk_ref, kbuf)
    v_all = conv_silu(v_ref, wv_ref, vbuf)

    gates = gc_ref[...]
    lane = lax.broadcasted_iota(jnp.int32, gates.shape, 1)
    ri = lax.broadcasted_iota(jnp.int32, (CHUNK, CHUNK), 0)
    ci = lax.broadcasted_iota(jnp.int32, (CHUNK, CHUNK), 1)
    tri_incl = (ci <= ri)[None]
    tri_strict = (ci < ri)[None]
    same16 = ((ri // 16) == (ci // 16))[None]
    same32 = ((ri // 32) == (ci // 32))[None]
    eye = (ri == ci).astype(F32)[None]

    def chunk_terms(gg):
        h = hp * GDN_PAIR + gg
        cols = slice(gg * D, (gg + 1) * D)
        q, k, v = q_all[:, cols], k_all[:, cols], v_all[:, cols]
        q = q * (lax.rsqrt(jnp.sum(q * q, axis=-1, keepdims=True) + RMS_EPS) * (D ** -0.5))
        k = k * lax.rsqrt(jnp.sum(k * k, axis=-1, keepdims=True) + RMS_EPS)
        g_cum = jnp.sum(jnp.where(lane == FOX_HEADS + h, gates, 0.0), axis=-1, keepdims=True)
        beta = jnp.sum(jnp.where(lane == FOX_HEADS + GDN_HEADS + h, gates, 0.0), axis=-1,
                       keepdims=True)
        q3 = q.reshape(n, CHUNK, D)
        k3 = k.reshape(n, CHUNK, D)
        v3 = v.reshape(n, CHUNK, D)
        g3 = g_cum.reshape(n, CHUNK, 1)
        b3 = beta.reshape(n, CHUNK, 1)
        g_row = gr_ref[:, pl.ds(FOX_HEADS + h, 1), :]
        g_last = g3[:, CHUNK - 1:CHUNK, :]

        decay = jnp.exp(jnp.where(tri_incl, g3 - g_row, -jnp.inf))
        kk = _bmm(k3, k3, _B_NT)
        L = jnp.where(tri_strict, b3 * kk * decay, 0.0)

        P = jnp.where(same16, L, 0.0)
        X = eye - P
        P2 = _bmm(P, P, _B_NN)
        X = _bmm(X, eye + P2, _B_NN)
        P4 = _bmm(P2, P2, _B_NN)
        X = _bmm(X, eye + P4, _B_NN)
        P8 = _bmm(P4, P4, _B_NN)
        X = _bmm(X, eye + P8, _B_NN)
        O32 = jnp.where(same32 & jnp.logical_not(same16), L, 0.0)
        X = X - _bmm(_bmm(X, O32, _B_NN), X, _B_NN)
        O64 = jnp.where(same32, 0.0, L)
        X = X - _bmm(_bmm(X, O64, _B_NN), X, _B_NN)

        e3 = jnp.exp(g3)
        rhs = jnp.concatenate([v3 * b3, k3 * (b3 * e3)], axis=-1)
        sol = _bmm(X, rhs, _B_NN)
        attn = jnp.where(tri_incl, _bmm(q3, k3, _B_NT) * decay, 0.0)
        return dict(u=sol[:, :, :D], w=sol[:, :, D:].astype(BF16), q=(q3 * e3).astype(BF16),
                    a=attn.astype(BF16), k=(k3 * jnp.exp(g_last - g3)).astype(BF16),
                    gl=jnp.exp(g_last))

    def pair_terms(p):
        t0, t1 = chunk_terms(2 * p), chunk_terms(2 * p + 1)
        return dict(
            u=jnp.concatenate([t0["u"], t1["u"]], axis=-1),
            wq=jnp.concatenate([jnp.concatenate([t0["w"], t1["w"]], axis=-1),
                                jnp.concatenate([t0["q"], t1["q"]], axis=-1)], axis=1),
            a=jnp.concatenate([t0["a"], t1["a"]], axis=-1),
            k=jnp.concatenate([t0["k"], t1["k"]], axis=-1),
            gl=jnp.concatenate([jnp.broadcast_to(t0["gl"], (n, 1, D)),
                                jnp.broadcast_to(t1["gl"], (n, 1, D))], axis=-1))

    def block_diag(x):
        first = lax.broadcasted_iota(jnp.int32, x.shape, 1) < D
        zero = jnp.zeros_like(x)
        return jnp.concatenate([jnp.where(first, x, zero), jnp.where(first, zero, x)], axis=0)

    n_pairs = GDN_PAIR // 2
    terms = [pair_terms(p) for p in range(n_pairs)]
    S = [state_ref[:, p * 2 * D:(p + 1) * 2 * D] for p in range(n_pairs)]
    for c in range(n):
        for p in range(n_pairs):
            tm = terms[p]
            r1 = jnp.dot(tm["wq"][c], block_diag(S[p].astype(BF16)), preferred_element_type=F32)
            v_b = (tm["u"][c] - r1[0:CHUNK]).astype(BF16)
            o_c = r1[CHUNK:2 * CHUNK] + jnp.dot(tm["a"][c], block_diag(v_b),
                                                preferred_element_type=F32)
            kv = lax.dot_general(tm["k"][c], v_b, _TN, preferred_element_type=F32)
            S[p] = S[p] * tm["gl"][c] + jnp.concatenate([kv[0:D, 0:D], kv[D:2 * D, D:2 * D]],
                                                         axis=1)
            obuf[c * CHUNK:(c + 1) * CHUNK, p * 2 * D:(p + 1) * 2 * D] = o_c
    for p in range(n_pairs):
        state_ref[:, p * 2 * D:(p + 1) * 2 * D] = S[p]

    for gg in range(GDN_PAIR):
        cols = slice(gg * D, (gg + 1) * D)
        o = obuf[:, cols]
        ms = jnp.mean(o * o, axis=-1, keepdims=True)
        z = z_ref[:, cols].astype(F32)
        o = o * lax.rsqrt(ms + RMS_EPS) * gain_ref[...] * (z * jax.nn.sigmoid(z))
        o_ref[:, cols] = o.astype(o_ref.dtype)


def _gdn(proj, proj_z, conv_w, gates_c, gates_r, out_gain, batch, seq_len, rows=1024):
    T = proj.shape[0]
    rows = min(rows, seq_len)
    nt = seq_len // rows
    n = rows // CHUNK
    pairs = GDN_HEADS // GDN_PAIR
    W = GDN_PAIR * HEAD_DIM

    def col(base):
        return pl.BlockSpec((rows, W), lambda b, hp, t: (b * nt + t, base // GDN_PAIR + hp))

    def wcol(base):
        return pl.BlockSpec((GDN_CONV, W), lambda b, hp, t: (0, base // GDN_PAIR + hp))

    return pl.pallas_call(
        functools.partial(_gdn_kernel, rows=rows),
        out_shape=jax.ShapeDtypeStruct((T, GDN_WIDTH), BF16),
        grid=(batch, pairs, nt),
        in_specs=[col(_GQ), col(_GK), col(_GV), col(_GZ),
                  wcol(0), wcol(GDN_HEADS), wcol(2 * GDN_HEADS),
                  pl.BlockSpec((rows, GATE_LANES), lambda b, hp, t: (b * nt + t, 0)),
                  pl.BlockSpec((n, GATE_LANES, CHUNK), lambda b, hp, t: (b * nt + t, 0, 0)),
                  pl.BlockSpec((1, HEAD_DIM), lambda b, hp, t: (0, 0))],
        out_specs=pl.BlockSpec((rows, W), lambda b, hp, t: (b * nt + t, hp)),
        scratch_shapes=[pltpu.VMEM((_HALO + rows, W), F32),
                        pltpu.VMEM((_HALO + rows, W), F32),
                        pltpu.VMEM((_HALO + rows, W), F32),
                        pltpu.VMEM((HEAD_DIM, W), F32),
                        pltpu.VMEM((rows, W), F32)],
        compiler_params=_params(("parallel", "parallel", "arbitrary")),
        name="gdn",
    )(proj, proj, proj, proj_z, conv_w, conv_w, conv_w, gates_c, gates_r,
      out_gain.reshape(1, HEAD_DIM))


_GATE_F_ROWS, _GATE_AB_ROWS = 16, 64


def _gate_rows_kernel(f_ref, ab_ref, *o_refs, depth):
    kt = f_ref.shape[1] // depth
    row = lax.broadcasted_iota(jnp.int32, (GATE_LANES, HEAD_DIM), 0)
    for l in range(depth):
        f = pltpu.einshape("nkl->knl", f_ref[:, pl.ds(l, kt, stride=depth), :])
        ab = pltpu.einshape("nkl->knl", ab_ref[:, pl.ds(l, kt, stride=depth), :])
        for k in range(kt):
            fk = jnp.concatenate([f[k], jnp.zeros((GATE_LANES - _GATE_F_ROWS, HEAD_DIM), F32)], axis=0)
            abk = jnp.concatenate([ab[k], jnp.zeros((GATE_LANES - _GATE_AB_ROWS, HEAD_DIM), F32)],
                                  axis=0)
            w = jnp.where(row < FOX_HEADS, fk,
                          jnp.where(row < FOX_HEADS + 2 * GDN_HEADS, abk, 0.0))
            o_refs[l][:, k * HEAD_DIM:(k + 1) * HEAD_DIM] = w.astype(BF16)


def _small_gate_weights_t(w_view, depth):
    p, rows, lanes = w_view.shape
    d = rows // depth * lanes
    f_col, ab_col = _A_WIDTH, _B_COL + _B_WIDTH
    assert f_col % _GATE_F_ROWS == 0 and ab_col % _GATE_AB_ROWS == FOX_HEADS
    return pl.pallas_call(
        functools.partial(_gate_rows_kernel, depth=depth),
        out_shape=[jax.ShapeDtypeStruct((GATE_LANES, d), BF16)] * depth,
        grid=(1,),
        in_specs=[pl.BlockSpec((_GATE_F_ROWS, rows, lanes), lambda i: (f_col // _GATE_F_ROWS, 0, 0)),
                  pl.BlockSpec((_GATE_AB_ROWS, rows, lanes), lambda i: (ab_col // _GATE_AB_ROWS, 0, 0))],
        out_specs=[pl.BlockSpec((GATE_LANES, d), lambda i: (0, 0))] * depth,
        compiler_params=_params(("arbitrary",)),
        name="gate_rows",
    )(w_view, w_view)


def _gate_columns(fox_forget_bias, gdn_a_log, gdn_dt_bias):
    pad = GATE_LANES - FOX_HEADS - GDN_HEADS
    bias = jnp.concatenate([fox_forget_bias, gdn_dt_bias, jnp.zeros((pad,), F32)])
    alog = jnp.concatenate([jnp.zeros((FOX_HEADS,), F32), gdn_a_log, jnp.zeros((pad,), F32)])
    return bias.reshape(GATE_LANES, 1), alog.reshape(GATE_LANES, 1)


def _layer(x, u, batch, seq_len, layer, w_in, fox_forget_bias, fox_out_norm, sc_conv_w, gdn_conv_w,
           gdn_a_log, gdn_dt_bias, gdn_out_norm, w_out, mix_post_norm, ffn_pre_norm,
           w_gate, w_up, w_down, ffn_post_norm, next_pre_norm):
    T, D = x.shape
    wa_t, wb_t, wc_t, w_small_t = w_in
    bias_col, alog_col = _gate_columns(fox_forget_bias, gdn_a_log, gdn_dt_bias)

    q_scale = jnp.concatenate([jnp.full((FOX_WIDTH,), FOX_Q_SCALE, F32),
                               jnp.ones((_A_WIDTH - FOX_WIDTH,), F32)]).reshape(1, _A_WIDTH)
    proj_a, w_out_b = _matmul(u, wa_t, BF16, tm=1024, tn=768, name="in_proj_a", w_is_t=True,
                              col_scale=q_scale, carried=_CarriedCast(w_out, layer))
    proj_b, wg = _matmul(u, wb_t, BF16, tm=1024, tn=768, name="in_proj_b", w_is_t=True,
                         carried=_CarriedCast(w_gate, layer))
    proj_c, = _matmul(u, wc_t, BF16, tm=1024, tn=768, name="in_proj_c", w_is_t=True)
    gates_t = _gates(u, w_small_t, bias_col, alog_col, seq_len)
    gates_c = gates_t.T
    gates_r = gates_t.reshape(GATE_LANES, T // CHUNK, CHUNK).transpose(1, 0, 2)

    fox_out = _fox_attention(proj_a, gates_c, fox_out_norm, batch, seq_len)
    sc_out = _short_conv(proj_b, sc_conv_w, seq_len)
    gdn_out = _gdn(proj_b, proj_c, gdn_conv_w, gates_c, gates_r, gdn_out_norm, batch, seq_len)

    y, wu = _out_proj(fox_out, sc_out, gdn_out,
                      w_out_b[:FOX_WIDTH], w_out_b[FOX_WIDTH:FOX_WIDTH + SC_WIDTH],
                      w_out_b[FOX_WIDTH + SC_WIDTH:], _CarriedCast(w_up, layer))
    h, v = _norm_residual(y, x, mix_post_norm, ffn_pre_norm)

    act, wd = _ffn_up(v, wg, wu, _CarriedCast(w_down, layer), tm=2048, tn=256)
    y2, = _matmul(act, wd, BF16, tm=512, tn=512, name="ffn_down")
    return _norm_residual(y2, h, ffn_post_norm, next_pre_norm)


def kernel(x, mix_pre_norm, w_in, fox_forget_bias, fox_out_norm, sc_conv_w, gdn_conv_w, gdn_a_log,
           gdn_dt_bias, gdn_out_norm, w_out, mix_post_norm, ffn_pre_norm, w_gate, w_up, w_down,
           ffn_post_norm):
    B, S, D = x.shape
    depth = w_in.shape[0]
    h = x.reshape(B * S, D)
    u = _norm_cast(h, mix_pre_norm[0])
    w_view = _column_major_view(w_in)
    windows = [_window_t(w_view, depth, col, width)
               for col, width in ((_A_COL, _A_WIDTH), (_B_COL, _B_WIDTH), (_C_COL, _C_WIDTH))]
    windows.append(_small_gate_weights_t(w_view, depth))
    for l in range(depth):
        nxt = mix_pre_norm[l + 1] if l + 1 < depth else None
        w_in_l = tuple(win[l] for win in windows)
        h, u = _layer(h, u, B, S, l, w_in_l, fox_forget_bias[l], fox_out_norm[l], sc_conv_w[l],
                      gdn_conv_w[l], gdn_a_log[l], gdn_dt_bias[l], gdn_out_norm[l], w_out,
                      mix_post_norm[l], ffn_pre_norm[l], w_gate, w_up, w_down,
                      ffn_post_norm[l], nxt)
    return h.reshape(B, S, D)
```

```python
import functools
import math
from typing import NamedTuple

import jax
import jax.numpy as jnp
from jax import lax
from jax.experimental import pallas as pl
from jax.experimental.pallas import tpu as pltpu

F32 = jnp.float32
BF16 = jnp.bfloat16

HEAD_DIM = 128
FOX_HEADS = 12
FOX_WIDTH = FOX_HEADS * HEAD_DIM
SC_WIDTH = 8 * HEAD_DIM
SC_KERNEL = 3
GDN_HEADS = 12
GDN_WIDTH = GDN_HEADS * HEAD_DIM
GDN_CONV = 4
CHUNK = 64
RMS_EPS = 1e-6
GATE_LANES = 128

VMEM_LIMIT_BYTES = 56 * 1024 * 1024

_FQ, _FK, _FV = 0, 12, 24
_SB, _SC, _SH = 0, 8, 16
_GQ, _GK, _GV = 24, 36, 48
_GZ = 0
_A_COL, _A_WIDTH = 0, 3 * FOX_WIDTH
_B_COL, _B_WIDTH = 3 * FOX_WIDTH + FOX_HEADS, 3 * SC_WIDTH + 3 * GDN_WIDTH
_C_COL, _C_WIDTH = _B_COL + _B_WIDTH + 2 * GDN_HEADS, GDN_WIDTH

_NT = (((1,), (1,)), ((), ()))
_TN = (((0,), (0,)), ((), ()))


def _params(sem):
    return pltpu.CompilerParams(dimension_semantics=sem, vmem_limit_bytes=VMEM_LIMIT_BYTES)


def _norm_cast_kernel(x_ref, g_ref, o_ref):
    x = x_ref[...]
    ms = jnp.mean(x * x, axis=-1, keepdims=True)
    o_ref[...] = (x * lax.rsqrt(ms + RMS_EPS) * g_ref[...]).astype(o_ref.dtype)


def _norm_cast(x, gain, tm=256):
    T, D = x.shape
    tm = min(tm, T)
    return pl.pallas_call(
        _norm_cast_kernel,
        out_shape=jax.ShapeDtypeStruct((T, D), BF16),
        grid=(T // tm,),
        in_specs=[pl.BlockSpec((tm, D), lambda i: (i, 0)),
                  pl.BlockSpec((1, D), lambda i: (0, 0))],
        out_specs=pl.BlockSpec((tm, D), lambda i: (i, 0)),
        compiler_params=_params(("parallel",)),
        name="norm_cast",
    )(x, gain.reshape(1, D))


def _norm_residual_kernel(y_ref, x_ref, g_ref, gn_ref, h_ref, *maybe_u_ref):
    y = y_ref[...].astype(F32)
    ms = jnp.mean(y * y, axis=-1, keepdims=True)
    h = x_ref[...] + y * lax.rsqrt(ms + RMS_EPS) * g_ref[...]
    h_ref[...] = h
    if maybe_u_ref:
        ms2 = jnp.mean(h * h, axis=-1, keepdims=True)
        maybe_u_ref[0][...] = (h * lax.rsqrt(ms2 + RMS_EPS) * gn_ref[...]).astype(BF16)


def _norm_residual(y, x, gain, next_gain, tm=256):
    T, D = x.shape
    tm = min(tm, T)
    emit_next = next_gain is not None
    gn = (next_gain if emit_next else gain).reshape(1, D)
    row = pl.BlockSpec((tm, D), lambda i: (i, 0))
    vec = pl.BlockSpec((1, D), lambda i: (0, 0))
    out_shape = [jax.ShapeDtypeStruct((T, D), F32)]
    out_specs = [row]
    if emit_next:
        out_shape.append(jax.ShapeDtypeStruct((T, D), BF16))
        out_specs.append(row)
    res = pl.pallas_call(
        _norm_residual_kernel,
        out_shape=out_shape,
        grid=(T // tm,),
        in_specs=[row, row, vec, vec],
        out_specs=out_specs,
        compiler_params=_params(("parallel",)),
        name="norm_residual",
    )(y, x, gain.reshape(1, D), gn)
    return (res[0], res[1]) if emit_next else (res[0], None)


def _window_t_kernel(a_ref, b_ref, *o_refs, off, depth):
    tn = o_refs[0].shape[0]
    kt = a_ref.shape[1] // depth
    for l in range(depth):
        x = a_ref[:, pl.ds(l, kt, stride=depth), :]
        if off:
            x = jnp.concatenate([x, b_ref[:, pl.ds(l, kt, stride=depth), :]], axis=0)[off:off + tn]
        xt = pltpu.einshape("nkl->knl", x)
        for k in range(kt):
            o_refs[l][:, k * HEAD_DIM:(k + 1) * HEAD_DIM] = xt[k].astype(BF16)


def _column_major_view(w_in):
    depth, d, p = w_in.shape
    kt = d // HEAD_DIM
    v = jnp.transpose(w_in, (2, 0, 1)).reshape(p, depth, kt, HEAD_DIM)
    return jnp.transpose(v, (0, 2, 1, 3)).reshape(p, kt * depth, HEAD_DIM)


def _window_t(w_view, depth, src_col, width, tn=256, tail=64):
    p, rows, lanes = w_view.shape
    d = rows // depth * lanes
    base, off = src_col // tn, src_col % tn
    assert width % tn == 0 and tn % tail == 0 and off <= tail
    return pl.pallas_call(
        functools.partial(_window_t_kernel, off=off, depth=depth),
        out_shape=[jax.ShapeDtypeStruct((width, d), BF16)] * depth,
        grid=(width // tn,),
        in_specs=[pl.BlockSpec((tn, rows, lanes), lambda j: (base + j, 0, 0)),
                  pl.BlockSpec((tail, rows, lanes), lambda j: ((base + j + 1) * (tn // tail), 0, 0))],
        out_specs=[pl.BlockSpec((tn, d), lambda j: (j, 0))] * depth,
        compiler_params=_params(("parallel",)),
        name="window_t",
    )(w_view, w_view)


class _CarriedCast(NamedTuple):
    w_stack: jax.Array
    layer: int


_BF16_SUBLANES = 16


def _carried_tile_rows(rows, steps):
    for t in range(_BF16_SUBLANES, rows, _BF16_SUBLANES):
        if rows % t == 0 and rows // t <= steps:
            return t
    return rows


def _call(kernel, carried, grid, in_specs, out_specs, out_shape, args, name):
    n_in, n_out = len(in_specs), len(out_specs)
    body = kernel
    if carried is not None:
        _, rows, cols = carried.w_stack.shape
        tile_rows = _carried_tile_rows(rows, math.prod(grid))
        n_tiles = rows // tile_rows

        def tile(*g):
            step = 0
            for size, idx in zip(grid, g):
                step = step * size + idx
            return jnp.minimum(step, n_tiles - 1)

        in_specs = in_specs + [pl.BlockSpec((1, tile_rows, cols),
                                            lambda *g: (carried.layer, tile(*g), 0))]
        out_specs = out_specs + [pl.BlockSpec((tile_rows, cols), lambda *g: (tile(*g), 0))]
        out_shape = out_shape + [jax.ShapeDtypeStruct((rows, cols), BF16)]
        args = args + [carried.w_stack]

        def body(*refs):
            src, dst = refs[n_in], refs[n_in + 1 + n_out]
            dst[...] = src[0].astype(dst.dtype)
            kernel(*refs[:n_in], *refs[n_in + 1:n_in + 1 + n_out], *refs[n_in + 2 + n_out:])

    return pl.pallas_call(
        body, out_shape=out_shape, grid=grid, in_specs=in_specs, out_specs=out_specs,
        compiler_params=_params(("arbitrary",) * len(grid)), name=name)(*args)


def _matmul_kernel(x_ref, w_ref, *rest, scaled, w_is_t):
    if w_is_t:
        acc = lax.dot_general(x_ref[...], w_ref[...], _NT, preferred_element_type=F32)
    else:
        acc = jnp.dot(x_ref[...], w_ref[...], preferred_element_type=F32)
    if scaled:
        s_ref, o_ref = rest
        acc = acc * s_ref[...]
    else:
        o_ref, = rest
    o_ref[...] = acc.astype(o_ref.dtype)


def _matmul(x, w, out_dtype, tm, tn, name, col_scale=None, carried=None, w_is_t=False):
    M, K = x.shape
    N = w.shape[0] if w_is_t else w.shape[1]
    tm, tn = min(tm, M), min(tn, N)
    scaled = col_scale is not None
    in_specs = [pl.BlockSpec((tm, K), lambda i, j: (i, 0)),
                pl.BlockSpec((tn, K), lambda i, j: (j, 0)) if w_is_t
                else pl.BlockSpec((K, tn), lambda i, j: (0, j))]
    args = [x, w]
    if scaled:
        in_specs.append(pl.BlockSpec((1, tn), lambda i, j: (0, j)))
        args.append(col_scale)
    return _call(functools.partial(_matmul_kernel, scaled=scaled, w_is_t=w_is_t), carried,
                 (M // tm, N // tn), in_specs, [pl.BlockSpec((tm, tn), lambda i, j: (i, j))],
                 [jax.ShapeDtypeStruct((M, N), out_dtype)], args, name)


def _matmul_w_resident(x, w, out_dtype, tm, tn, name):
    M, K = x.shape
    N = w.shape[1]
    return _call(functools.partial(_matmul_kernel, scaled=False, w_is_t=False), None,
                 (N // tn, M // tm),
                 [pl.BlockSpec((tm, K), lambda j, i: (i, 0)),
                  pl.BlockSpec((K, tn), lambda j, i: (0, j), pipeline_mode=pl.Buffered(1))],
                 [pl.BlockSpec((tm, tn), lambda j, i: (i, j))],
                 [jax.ShapeDtypeStruct((M, N), out_dtype)], [x, w], name)


def _out_proj_kernel(a1_ref, a2_ref, a3_ref, w1_ref, w2_ref, w3_ref, o_ref):
    acc = jnp.dot(a1_ref[...], w1_ref[...], preferred_element_type=F32)
    acc += jnp.dot(a2_ref[...], w2_ref[...], preferred_element_type=F32)
    acc += jnp.dot(a3_ref[...], w3_ref[...], preferred_element_type=F32)
    o_ref[...] = acc.astype(o_ref.dtype)


def _out_proj(a1, a2, a3, w1, w2, w3, carried, tm=1024, tn=1024):
    T = a1.shape[0]
    N = w1.shape[1]
    tm, tn = min(tm, T), min(tn, N)

    def lhs(a):
        return pl.BlockSpec((tm, a.shape[1]), lambda i, j: (i, 0))

    def rhs(w):
        return pl.BlockSpec((w.shape[0], tn), lambda i, j: (0, j))

    return _call(_out_proj_kernel, carried, (T // tm, N // tn),
                 [lhs(a1), lhs(a2), lhs(a3), rhs(w1), rhs(w2), rhs(w3)],
                 [pl.BlockSpec((tm, tn), lambda i, j: (i, j))],
                 [jax.ShapeDtypeStruct((T, N), BF16)], [a1, a2, a3, w1, w2, w3], "out_proj")


def _ffn_up_kernel(x_ref, wg_ref, wu_ref, o_ref):
    x = x_ref[...]
    g = jnp.dot(x, wg_ref[...], preferred_element_type=F32)
    u = jnp.dot(x, wu_ref[...], preferred_element_type=F32)
    o_ref[...] = (g * jax.nn.sigmoid(g) * u).astype(o_ref.dtype)


def _ffn_up(x, wg, wu, carried, tm=1024, tn=512):
    T, K = x.shape
    N = wg.shape[1]
    tm = min(tm, T)
    return _call(_ffn_up_kernel, carried, (T // tm, pl.cdiv(N, tn)),
                 [pl.BlockSpec((tm, K), lambda i, j: (i, 0)),
                  pl.BlockSpec((K, tn), lambda i, j: (0, j)),
                  pl.BlockSpec((K, tn), lambda i, j: (0, j))],
                 [pl.BlockSpec((tm, tn), lambda i, j: (i, j))],
                 [jax.ShapeDtypeStruct((T, N), BF16)], [x, wg, wu], "ffn_up")


def _gates_kernel(w_ref, u_ref, bias_ref, alog_ref, gt_ref, carry_ref, *, steps_per_seq, ts):
    t = pl.program_id(0)

    @pl.when(t % steps_per_seq == 0)
    def _():
        carry_ref[...] = jnp.zeros_like(carry_ref)

    z = lax.dot_general(w_ref[...], u_ref[...], _NT, preferred_element_type=F32) + bias_ref[...]
    row = lax.broadcasted_iota(jnp.int32, z.shape, 0)
    tail = jnp.log1p(jnp.exp(-jnp.abs(z)))
    log_sig = jnp.minimum(z, 0.0) - tail
    softplus = jnp.maximum(z, 0.0) + tail
    sig = 1.0 / (1.0 + jnp.exp(-z))
    decay = -jnp.exp(alog_ref[...]) * softplus
    val = jnp.where(row < FOX_HEADS, log_sig, jnp.where(row < FOX_HEADS + GDN_HEADS, decay, sig))

    src = lax.broadcasted_iota(jnp.int32, (ts, ts), 0)
    dst = lax.broadcasted_iota(jnp.int32, (ts, ts), 1)
    upper = src <= dst
    same_chunk = (src // CHUNK) == (dst // CHUNK)
    hi = val.astype(BF16)
    rest = val - hi.astype(F32)
    mid = rest.astype(BF16)
    lo = (rest - mid.astype(F32)).astype(BF16)
    pieces = jnp.concatenate([hi, mid, lo], axis=1)

    def prefix_sum(mask):
        ones = mask.astype(BF16)
        return jnp.dot(pieces, jnp.concatenate([ones, ones, ones], axis=0),
                       preferred_element_type=F32)

    cum_all = prefix_sum(upper)
    cum_chunk = prefix_sum(upper & same_chunk)
    cum_all = cum_all + carry_ref[...]
    carry_ref[...] = cum_all[:, ts - 1:ts]
    gt_ref[...] = jnp.where(row < FOX_HEADS, cum_all,
                            jnp.where(row < FOX_HEADS + GDN_HEADS, cum_chunk, val))


def _gates(u, w_small_t, bias_col, alog_col, seq_len, ts=512):
    T, D = u.shape
    ts = min(ts, seq_len)
    return pl.pallas_call(
        functools.partial(_gates_kernel, steps_per_seq=seq_len // ts, ts=ts),
        out_shape=jax.ShapeDtypeStruct((GATE_LANES, T), F32),
        grid=(T // ts,),
        in_specs=[pl.BlockSpec((GATE_LANES, D), lambda t: (0, 0)),
                  pl.BlockSpec((ts, D), lambda t: (t, 0)),
                  pl.BlockSpec((GATE_LANES, 1), lambda t: (0, 0)),
                  pl.BlockSpec((GATE_LANES, 1), lambda t: (0, 0))],
        out_specs=pl.BlockSpec((GATE_LANES, ts), lambda t: (0, t)),
        scratch_shapes=[pltpu.VMEM((GATE_LANES, 1), F32)],
        compiler_params=_params(("arbitrary",)),
        name="gates",
    )(w_small_t, u, bias_col, alog_col)


FOX_Q_SCALE = (HEAD_DIM ** -0.5) * math.log2(math.e)
FOX_PAIR = 2
_FOX_BUILD_ROWS = 512


def _fox_kernel(q_ref, qn_ref, k_ref, v_ref, gc_ref, g_ref, o_ref, kaug, vaug, m_ref, acc_ref,
                s0_ref, sa_ref, sb_ref, *, tq, tk, seq_len, pairs):
    D = HEAD_DIM
    pair = pl.program_id(0) % pairs
    qi = pl.program_id(1)

    @pl.when(qi == 0)
    def _build():
        ri = lax.broadcasted_iota(jnp.int32, (3 * D, FOX_PAIR * D), 0)
        ci = lax.broadcasted_iota(jnp.int32, (3 * D, FOX_PAIR * D), 1)
        sel = ri < 0
        for gg in range(FOX_PAIR):
            for j in range(3):
                sel |= (ri == j * D + pair * FOX_PAIR + gg) & (ci == gg * D + j)
        sel = sel.astype(BF16)
        ones = jnp.ones((_FOX_BUILD_ROWS, D), BF16)

        def chunk(i, carry):
            r0 = pl.multiple_of(i * _FOX_BUILD_ROWS, _FOX_BUILD_ROWS)
            rows = pl.ds(r0, _FOX_BUILD_ROWS)
            g = gc_ref[rows, :] * (-math.log2(math.e))
            hi = g.astype(BF16)
            r1 = g - hi.astype(F32)
            mid = r1.astype(BF16)
            lo = (r1 - mid.astype(F32)).astype(BF16)
            pieces = jnp.concatenate([hi, mid, lo], axis=1)
            aug = jnp.dot(pieces, sel, preferred_element_type=F32).astype(BF16)
            for gg in range(FOX_PAIR):
                kaug[gg, rows, 0:D] = k_ref[rows, gg * D:(gg + 1) * D]
                kaug[gg, rows, D:2 * D] = aug[:, gg * D:(gg + 1) * D]
                vaug[gg, rows, 0:D] = v_ref[rows, gg * D:(gg + 1) * D]
                vaug[gg, rows, D:2 * D] = ones
            return carry

        lax.fori_loop(0, seq_len // _FOX_BUILD_ROWS, chunk, 0, unroll=8)

    lane = lax.broadcasted_iota(jnp.int32, (tq, D), 1)
    ones3 = jnp.where(lane < 3, 1.0, 0.0).astype(BF16)
    m_ref[...] = jnp.full_like(m_ref, -jnp.inf)
    acc_ref[...] = jnp.zeros_like(acc_ref)

    def scores(kj, slot_ref, queries=q_ref, width=tk):
        k0 = pl.multiple_of(kj * tk, tk)
        for gg in range(FOX_PAIR):
            q_aug = jnp.concatenate([queries[:, gg * D:(gg + 1) * D], ones3], axis=1)
            slot_ref[gg, :, 0:width] = lax.dot_general(q_aug, kaug[gg, pl.ds(k0, width), :], _NT,
                                                       preferred_element_type=F32)

    def accumulate(kj, slot_ref, masked=False, width=tk):
        k0 = pl.multiple_of(kj * tk, tk)
        for gg in range(FOX_PAIR):
            s = slot_ref[gg, :, 0:width]
            if masked:
                r = qi * tq + lax.broadcasted_iota(jnp.int32, s.shape, 0)
                c = kj * tk + lax.broadcasted_iota(jnp.int32, s.shape, 1)
                s = jnp.where(c <= r, s, -jnp.inf)
            m_prev = m_ref[gg]
            m_new = jnp.maximum(m_prev, jnp.max(s, axis=-1, keepdims=True))
            alpha = jnp.exp2(m_prev - m_new)
            p = jnp.concatenate(
                [jnp.exp2(s[:, j * D:(j + 1) * D] - m_new) for j in range(width // D)],
                axis=1).astype(BF16)
            pv = jnp.dot(p, vaug[gg, pl.ds(k0, width), :], preferred_element_type=F32)
            acc_ref[gg] = jnp.concatenate([alpha, alpha], axis=1) * acc_ref[gg] + pv
            m_ref[gg] = m_new

    n_full = (qi * tq) // tk
    first_prefetching_step = tk // tq

    @pl.when(qi <= first_prefetching_step)
    def _():
        scores(0, s0_ref)

    ratio = tk // tq

    def diagonal_variants(cond, emit):
        for r in range(ratio):
            pl.when(cond & (qi % ratio == r))(functools.partial(emit, (r + 1) * tq))

    diagonal_variants(n_full == 0, lambda width: accumulate(0, s0_ref, masked=True, width=width))

    @pl.when(n_full >= 1)
    def _():
        scores(1, sa_ref)
        accumulate(0, s0_ref)

    def body(i, carry):
        scores(2 * i + 2, sb_ref)
        accumulate(2 * i + 1, sa_ref)
        scores(2 * i + 3, sa_ref)
        accumulate(2 * i + 2, sb_ref)
        return carry

    lax.fori_loop(0, (n_full - 1) // 2, body, 0)

    def odd_tail(width):
        scores(0, s0_ref, qn_ref)
        accumulate(n_full, sa_ref, masked=True, width=width)

    def even_tail(width):
        scores(n_full, sb_ref, width=width)
        accumulate(n_full - 1, sa_ref)
        scores(0, s0_ref, qn_ref)
        accumulate(n_full, sb_ref, masked=True, width=width)

    diagonal_variants((n_full >= 1) & (n_full % 2 == 1), odd_tail)
    diagonal_variants((n_full >= 1) & (n_full % 2 == 0), even_tail)

    for gg in range(FOX_PAIR):
        acc = acc_ref[gg]
        out = acc[:, 0:D] / acc[:, D:2 * D]
        ms = jnp.mean(out * out, axis=-1, keepdims=True)
        o_ref[:, gg * D:(gg + 1) * D] = (out * lax.rsqrt(ms + RMS_EPS) * g_ref[...]).astype(o_ref.dtype)


def _fox_attention(proj, gates_c, out_gain, batch, seq_len, tq=512, tk=1024):
    T = proj.shape[0]
    tq = min(tq, seq_len)
    tk = min(tk, seq_len)
    assert tk % tq == 0 and seq_len % tk == 0
    nq = seq_len // tq
    pairs = FOX_HEADS // FOX_PAIR
    W = FOX_PAIR * HEAD_DIM

    once = pl.Buffered(1)

    def rows_spec(base):
        return pl.BlockSpec((seq_len, W), lambda bp, qi: (bp // pairs, base // FOX_PAIR + bp % pairs),
                            pipeline_mode=once)

    return pl.pallas_call(
        functools.partial(_fox_kernel, tq=tq, tk=tk, seq_len=seq_len, pairs=pairs),
        out_shape=jax.ShapeDtypeStruct((T, FOX_WIDTH), BF16),
        grid=(batch * pairs, nq),
        in_specs=[pl.BlockSpec((tq, W), lambda bp, qi: ((bp // pairs) * nq + qi,
                                                         _FQ // FOX_PAIR + bp % pairs)),
                  pl.BlockSpec((tq, W), lambda bp, qi: ((bp // pairs) * nq + jnp.minimum(qi + 1, nq - 1),
                                                         _FQ // FOX_PAIR + bp % pairs)),
                  rows_spec(_FK), rows_spec(_FV),
                  pl.BlockSpec((seq_len, GATE_LANES), lambda bp, qi: (bp // pairs, 0),
                               pipeline_mode=once),
                  pl.BlockSpec((1, HEAD_DIM), lambda bp, qi: (0, 0))],
        out_specs=pl.BlockSpec((tq, W), lambda bp, qi: ((bp // pairs) * nq + qi, bp % pairs)),
        scratch_shapes=[pltpu.VMEM((FOX_PAIR, seq_len, 2 * HEAD_DIM), BF16),
                        pltpu.VMEM((FOX_PAIR, seq_len, 2 * HEAD_DIM), BF16),
                        pltpu.VMEM((FOX_PAIR, tq, HEAD_DIM), F32),
                        pltpu.VMEM((FOX_PAIR, tq, 2 * HEAD_DIM), F32),
                        pltpu.VMEM((FOX_PAIR, tq, tk), F32),
                        pltpu.VMEM((FOX_PAIR, tq, tk), F32),
                        pltpu.VMEM((FOX_PAIR, tq, tk), F32)],
        compiler_params=_params(("parallel", "arbitrary")),
        name="fox_attention",
    )(proj, proj, proj, proj, gates_c, out_gain.reshape(1, HEAD_DIM))


_HALO = 8


def _short_conv_kernel(b_ref, c_ref, h_ref, w_ref, o_ref, buf_ref, *, steps_per_seq, ts):
    t = pl.program_id(1)

    @pl.when(t % steps_per_seq == 0)
    def _():
        buf_ref[0:_HALO, :] = jnp.zeros((_HALO, buf_ref.shape[1]), F32)

    buf_ref[_HALO:_HALO + ts, :] = c_ref[...].astype(F32) * h_ref[...].astype(F32)
    w = w_ref[...]
    xa = buf_ref[...]
    y = w[2:3, :] * xa[_HALO:_HALO + ts, :]
    y += w[1:2, :] * pltpu.roll(xa, 1, axis=0)[_HALO:_HALO + ts, :]
    y += w[0:1, :] * pltpu.roll(xa, 2, axis=0)[_HALO:_HALO + ts, :]
    o_ref[...] = (b_ref[...].astype(F32) * y).astype(o_ref.dtype)
    buf_ref[0:_HALO, :] = buf_ref[ts:ts + _HALO, :]


def _short_conv(proj, conv_w, seq_len, ts=1024, tc=1024):
    T = proj.shape[0]
    ts = min(ts, seq_len)
    per = tc // HEAD_DIM

    def col(base):
        return pl.BlockSpec((ts, tc), lambda c, t: (t, base // per + c))

    return pl.pallas_call(
        functools.partial(_short_conv_kernel, steps_per_seq=seq_len // ts, ts=ts),
        out_shape=jax.ShapeDtypeStruct((T, SC_WIDTH), BF16),
        grid=(SC_WIDTH // tc, T // ts),
        in_specs=[col(_SB), col(_SC), col(_SH),
                  pl.BlockSpec((SC_KERNEL, tc), lambda c, t: (0, c))],
        out_specs=pl.BlockSpec((ts, tc), lambda c, t: (t, c)),
        scratch_shapes=[pltpu.VMEM((_HALO + ts, tc), F32)],
        compiler_params=_params(("parallel", "arbitrary")),
        name="short_conv",
    )(proj, proj, proj, conv_w)


def _bmm(a, b, dims):
    return lax.dot_general(a.astype(BF16), b.astype(BF16), dims, preferred_element_type=F32)


_B_NN = (((2,), (1,)), ((0,), (0,)))
_B_NT = (((2,), (2,)), ((0,), (0,)))


GDN_PAIR = 4


def _gdn_kernel(q_ref, k_ref, v_ref, z_ref, wq_ref, wk_ref, wv_ref, gc_ref, gr_ref, gain_ref,
                o_ref, qbuf, kbuf, vbuf, state_ref, obuf, *, rows):
    hp = pl.program_id(1)
    t = pl.program_id(2)
    n = rows // CHUNK
    D = HEAD_DIM

    @pl.when(t == 0)
    def _():
        zeros = jnp.zeros((_HALO, GDN_PAIR * D), F32)
        qbuf[0:_HALO, :] = zeros
        kbuf[0:_HALO, :] = zeros
        vbuf[0:_HALO, :] = zeros
        state_ref[...] = jnp.zeros_like(state_ref)

    def conv_silu(x_ref, w_ref, buf):
        buf[_HALO:_HALO + rows, :] = x_ref[...].astype(F32)
        w = w_ref[...]
        xa = buf[...]

        def delayed(j):
            return pltpu.roll(xa, j, axis=0)[_HALO:_HALO + rows, :]

        y = w[3:4, :] * xa[_HALO:_HALO + rows, :]
        y += w[2:3, :] * delayed(1)
        y += w[1:2, :] * delayed(2)
        y += w[0:1, :] * delayed(3)
        buf[0:_HALO, :] = buf[rows:rows + _HALO, :]
        return y * jax.nn.sigmoid(y)

    q_all = conv_silu(q_ref, wq_ref, qbuf)
    k_all = conv_silu(k_ref, wk_ref, kbuf)
    v_all = conv_silu(v_ref, wv_ref, vbuf)

    gates = gc_ref[...]
    lane = lax.broadcasted_iota(jnp.int32, gates.shape, 1)
    ri = lax.broadcasted_iota(jnp.int32, (CHUNK, CHUNK), 0)
    ci = lax.broadcasted_iota(jnp.int32, (CHUNK, CHUNK), 1)
    tri_incl = (ci <= ri)[None]
    tri_strict = (ci < ri)[None]
    same16 = ((ri // 16) == (ci // 16))[None]
    same32 = ((ri // 32) == (ci // 32))[None]
    eye = (ri == ci).astype(F32)[None]

    def chunk_terms(gg):
        h = hp * GDN_PAIR + gg
        cols = slice(gg * D, (gg + 1) * D)
        q, k, v = q_all[:, cols], k_all[:, cols], v_all[:, cols]
        q = q * (lax.rsqrt(jnp.sum(q * q, axis=-1, keepdims=True) + RMS_EPS) * (D ** -0.5))
        k = k * lax.rsqrt(jnp.sum(k * k, axis=-1, keepdims=True) + RMS_EPS)
        g_cum = jnp.sum(jnp.where(lane == FOX_HEADS + h, gates, 0.0), axis=-1, keepdims=True)
        beta = jnp.sum(jnp.where(lane == FOX_HEADS + GDN_HEADS + h, gates, 0.0), axis=-1,
                       keepdims=True)
        q3 = q.reshape(n, CHUNK, D)
        k3 = k.reshape(n, CHUNK, D)
        v3 = v.reshape(n, CHUNK, D)
        g3 = g_cum.reshape(n, CHUNK, 1)
        b3 = beta.reshape(n, CHUNK, 1)
        g_row = gr_ref[:, pl.ds(FOX_HEADS + h, 1), :]
        g_last = g3[:, CHUNK - 1:CHUNK, :]

        decay = jnp.exp(jnp.where(tri_incl, g3 - g_row, -jnp.inf))
        kk = _bmm(k3, k3, _B_NT)
        L = jnp.where(tri_strict, b3 * kk * decay, 0.0)

        P = jnp.where(same16, L, 0.0)
        X = eye - P
        P2 = _bmm(P, P, _B_NN)
        X = _bmm(X, eye + P2, _B_NN)
        P4 = _bmm(P2, P2, _B_NN)
        X = _bmm(X, eye + P4, _B_NN)
        P8 = _bmm(P4, P4, _B_NN)
        X = _bmm(X, eye + P8, _B_NN)
        O32 = jnp.where(same32 & jnp.logical_not(same16), L, 0.0)
        X = X - _bmm(_bmm(X, O32, _B_NN), X, _B_NN)
        O64 = jnp.where(same32, 0.0, L)
        X = X - _bmm(_bmm(X, O64, _B_NN), X, _B_NN)

        e3 = jnp.exp(g3)
        rhs = jnp.concatenate([v3 * b3, k3 * (b3 * e3)], axis=-1)
        sol = _bmm(X, rhs, _B_NN)
        attn = jnp.where(tri_incl, _bmm(q3, k3, _B_NT) * decay, 0.0)
        return dict(u=sol[:, :, :D], w=sol[:, :, D:].astype(BF16), q=(q3 * e3).astype(BF16),
                    a=attn.astype(BF16), k=(k3 * jnp.exp(g_last - g3)).astype(BF16),
                    gl=jnp.exp(g_last))

    def pair_terms(p):
        t0, t1 = chunk_terms(2 * p), chunk_terms(2 * p + 1)
        return dict(
            u=jnp.concatenate([t0["u"], t1["u"]], axis=-1),
            wq=jnp.concatenate([jnp.concatenate([t0["w"], t1["w"]], axis=-1),
                                jnp.concatenate([t0["q"], t1["q"]], axis=-1)], axis=1),
            a=jnp.concatenate([t0["a"], t1["a"]], axis=-1),
            k=jnp.concatenate([t0["k"], t1["k"]], axis=-1),
            gl=jnp.concatenate([jnp.broadcast_to(t0["gl"], (n, 1, D)),
                                jnp.broadcast_to(t1["gl"], (n, 1, D))], axis=-1))

    def block_diag(x):
        first = lax.broadcasted_iota(jnp.int32, x.shape, 1) < D
        zero = jnp.zeros_like(x)
        return jnp.concatenate([jnp.where(first, x, zero), jnp.where(first, zero, x)], axis=0)

    n_pairs = GDN_PAIR // 2
    terms = [pair_terms(p) for p in range(n_pairs)]
    S = [state_ref[:, p * 2 * D:(p + 1) * 2 * D] for p in range(n_pairs)]
    for c in range(n):
        for p in range(n_pairs):
            tm = terms[p]
            r1 = jnp.dot(tm["wq"][c], block_diag(S[p].astype(BF16)), preferred_element_type=F32)
            v_b = (tm["u"][c] - r1[0:CHUNK]).astype(BF16)
            o_c = r1[CHUNK:2 * CHUNK] + jnp.dot(tm["a"][c], block_diag(v_b),
                                                preferred_element_type=F32)
            kv = lax.dot_general(tm["k"][c], v_b, _TN, preferred_element_type=F32)
            S[p] = S[p] * tm["gl"][c] + jnp.concatenate([kv[0:D, 0:D], kv[D:2 * D, D:2 * D]],
                                                         axis=1)
            obuf[c * CHUNK:(c + 1) * CHUNK, p * 2 * D:(p + 1) * 2 * D] = o_c
    for p in range(n_pairs):
        state_ref[:, p * 2 * D:(p + 1) * 2 * D] = S[p]

    for gg in range(GDN_PAIR):
        cols = slice(gg * D, (gg + 1) * D)
        o = obuf[:, cols]
        ms = jnp.mean(o * o, axis=-1, keepdims=True)
        z = z_ref[:, cols].astype(F32)
        o = o * lax.rsqrt(ms + RMS_EPS) * gain_ref[...] * (z * jax.nn.sigmoid(z))
        o_ref[:, cols] = o.astype(o_ref.dtype)


def _gdn(proj, proj_z, conv_w, gates_c, gates_r, out_gain, batch, seq_len, rows=1024):
    T = proj.shape[0]
    rows = min(rows, seq_len)
    nt = seq_len // rows
    n = rows // CHUNK
    pairs = GDN_HEADS // GDN_PAIR
    W = GDN_PAIR * HEAD_DIM

    def col(base):
        return pl.BlockSpec((rows, W), lambda b, hp, t: (b * nt + t, base // GDN_PAIR + hp))

    def wcol(base):
        return pl.BlockSpec((GDN_CONV, W), lambda b, hp, t: (0, base // GDN_PAIR + hp))

    return pl.pallas_call(
        functools.partial(_gdn_kernel, rows=rows),
        out_shape=jax.ShapeDtypeStruct((T, GDN_WIDTH), BF16),
        grid=(batch, pairs, nt),
        in_specs=[col(_GQ), col(_GK), col(_GV), col(_GZ),
                  wcol(0), wcol(GDN_HEADS), wcol(2 * GDN_HEADS),
                  pl.BlockSpec((rows, GATE_LANES), lambda b, hp, t: (b * nt + t, 0)),
                  pl.BlockSpec((n, GATE_LANES, CHUNK), lambda b, hp, t: (b * nt + t, 0, 0)),
                  pl.BlockSpec((1, HEAD_DIM), lambda b, hp, t: (0, 0))],
        out_specs=pl.BlockSpec((rows, W), lambda b, hp, t: (b * nt + t, hp)),
        scratch_shapes=[pltpu.VMEM((_HALO + rows, W), F32),
                        pltpu.VMEM((_HALO + rows, W), F32),
                        pltpu.VMEM((_HALO + rows, W), F32),
                        pltpu.VMEM((HEAD_DIM, W), F32),
                        pltpu.VMEM((rows, W), F32)],
        compiler_params=_params(("parallel", "parallel", "arbitrary")),
        name="gdn",
    )(proj, proj, proj, proj_z, conv_w, conv_w, conv_w, gates_c, gates_r,
      out_gain.reshape(1, HEAD_DIM))


_GATE_F_ROWS, _GATE_AB_ROWS = 16, 64


def _gate_rows_kernel(f_ref, ab_ref, *o_refs, depth):
    kt = f_ref.shape[1] // depth
    row = lax.broadcasted_iota(jnp.int32, (GATE_LANES, HEAD_DIM), 0)
    for l in range(depth):
        f = pltpu.einshape("nkl->knl", f_ref[:, pl.ds(l, kt, stride=depth), :])
        ab = pltpu.einshape("nkl->knl", ab_ref[:, pl.ds(l, kt, stride=depth), :])
        for k in range(kt):
            fk = jnp.concatenate([f[k], jnp.zeros((GATE_LANES - _GATE_F_ROWS, HEAD_DIM), F32)], axis=0)
            abk = jnp.concatenate([ab[k], jnp.zeros((GATE_LANES - _GATE_AB_ROWS, HEAD_DIM), F32)],
                                  axis=0)
            w = jnp.where(row < FOX_HEADS, fk,
                          jnp.where(row < FOX_HEADS + 2 * GDN_HEADS, abk, 0.0))
            o_refs[l][:, k * HEAD_DIM:(k + 1) * HEAD_DIM] = w.astype(BF16)


def _small_gate_weights_t(w_view, depth):
    p, rows, lanes = w_view.shape
    d = rows // depth * lanes
    f_col, ab_col = _A_WIDTH, _B_COL + _B_WIDTH
    assert f_col % _GATE_F_ROWS == 0 and ab_col % _GATE_AB_ROWS == FOX_HEADS
    return pl.pallas_call(
        functools.partial(_gate_rows_kernel, depth=depth),
        out_shape=[jax.ShapeDtypeStruct((GATE_LANES, d), BF16)] * depth,
        grid=(1,),
        in_specs=[pl.BlockSpec((_GATE_F_ROWS, rows, lanes), lambda i: (f_col // _GATE_F_ROWS, 0, 0)),
                  pl.BlockSpec((_GATE_AB_ROWS, rows, lanes), lambda i: (ab_col // _GATE_AB_ROWS, 0, 0))],
        out_specs=[pl.BlockSpec((GATE_LANES, d), lambda i: (0, 0))] * depth,
        compiler_params=_params(("arbitrary",)),
        name="gate_rows",
    )(w_view, w_view)


def _gate_columns(fox_forget_bias, gdn_a_log, gdn_dt_bias):
    pad = GATE_LANES - FOX_HEADS - GDN_HEADS
    bias = jnp.concatenate([fox_forget_bias, gdn_dt_bias, jnp.zeros((pad,), F32)])
    alog = jnp.concatenate([jnp.zeros((FOX_HEADS,), F32), gdn_a_log, jnp.zeros((pad,), F32)])
    return bias.reshape(GATE_LANES, 1), alog.reshape(GATE_LANES, 1)


def _layer(x, u, batch, seq_len, layer, w_in, fox_forget_bias, fox_out_norm, sc_conv_w, gdn_conv_w,
           gdn_a_log, gdn_dt_bias, gdn_out_norm, w_out, mix_post_norm, ffn_pre_norm,
           w_gate, w_up, w_down, ffn_post_norm, next_pre_norm):
    T, D = x.shape
    wa_t, wb_t, wc_t, w_small_t = w_in
    bias_col, alog_col = _gate_columns(fox_forget_bias, gdn_a_log, gdn_dt_bias)

    q_scale = jnp.concatenate([jnp.full((FOX_WIDTH,), FOX_Q_SCALE, F32),
                               jnp.ones((_A_WIDTH - FOX_WIDTH,), F32)]).reshape(1, _A_WIDTH)
    proj_a, w_out_b = _matmul(u, wa_t, BF16, tm=1024, tn=768, name="in_proj_a", w_is_t=True,
                              col_scale=q_scale, carried=_CarriedCast(w_out, layer))
    proj_b, wg = _matmul(u, wb_t, BF16, tm=1024, tn=768, name="in_proj_b", w_is_t=True,
                         carried=_CarriedCast(w_gate, layer))
    proj_c, = _matmul(u, wc_t, BF16, tm=1024, tn=768, name="in_proj_c", w_is_t=True)
    gates_t = _gates(u, w_small_t, bias_col, alog_col, seq_len)
    gates_c = gates_t.T
    gates_r = gates_t.reshape(GATE_LANES, T // CHUNK, CHUNK).transpose(1, 0, 2)

    fox_out = _fox_attention(proj_a, gates_c, fox_out_norm, batch, seq_len)
    sc_out = _short_conv(proj_b, sc_conv_w, seq_len)
    gdn_out = _gdn(proj_b, proj_c, gdn_conv_w, gates_c, gates_r, gdn_out_norm, batch, seq_len)

    y, wu = _out_proj(fox_out, sc_out, gdn_out,
                      w_out_b[:FOX_WIDTH], w_out_b[FOX_WIDTH:FOX_WIDTH + SC_WIDTH],
                      w_out_b[FOX_WIDTH + SC_WIDTH:], _CarriedCast(w_up, layer))
    h, v = _norm_residual(y, x, mix_post_norm, ffn_pre_norm)

    act, wd = _ffn_up(v, wg, wu, _CarriedCast(w_down, layer), tm=2048, tn=256)
    y2, = _matmul_w_resident(act, wd, BF16, tm=512, tn=1024, name="ffn_down")
    return _norm_residual(y2, h, ffn_post_norm, next_pre_norm)


def kernel(x, mix_pre_norm, w_in, fox_forget_bias, fox_out_norm, sc_conv_w, gdn_conv_w, gdn_a_log,
           gdn_dt_bias, gdn_out_norm, w_out, mix_post_norm, ffn_pre_norm, w_gate, w_up, w_down,
           ffn_post_norm):
    B, S, D = x.shape
    depth = w_in.shape[0]
    h = x.reshape(B * S, D)
    u = _norm_cast(h, mix_pre_norm[0])
    w_view = _column_major_view(w_in)
    windows = [_window_t(w_view, depth, col, width)
               for col, width in ((_A_COL, _A_WIDTH), (_B_COL, _B_WIDTH), (_C_COL, _C_WIDTH))]
    windows.append(_small_gate_weights_t(w_view, depth))
    for l in range(depth):
        nxt = mix_pre_norm[l + 1] if l + 1 < depth else None
        w_in_l = tuple(win[l] for win in windows)
        h, u = _layer(h, u, B, S, l, w_in_l, fox_forget_bias[l], fox_out_norm[l], sc_conv_w[l],
                      gdn_conv_w[l], gdn_a_log[l], gdn_dt_bias[l], gdn_out_norm[l], w_out,
                      mix_post_norm[l], ffn_pre_norm[l], w_gate, w_up, w_down,
                      ffn_post_norm[l], nxt)
    return h.reshape(B, S, D)
```

```python
import functools
import math
from typing import NamedTuple

import jax
import jax.numpy as jnp
from jax import lax
from jax.experimental import pallas as pl
from jax.experimental.pallas import tpu as pltpu

F32 = jnp.float32
BF16 = jnp.bfloat16

HEAD_DIM = 128
FOX_HEADS = 12
FOX_WIDTH = FOX_HEADS * HEAD_DIM
SC_WIDTH = 8 * HEAD_DIM
SC_KERNEL = 3
GDN_HEADS = 12
GDN_WIDTH = GDN_HEADS * HEAD_DIM
GDN_CONV = 4
CHUNK = 64
RMS_EPS = 1e-6
GATE_LANES = 128

VMEM_LIMIT_BYTES = 56 * 1024 * 1024

_FQ, _FK, _FV = 0, 12, 24
_SB, _SC, _SH = 0, 8, 16
_GQ, _GK, _GV = 24, 36, 48
_GZ = 0
_A_COL, _A_WIDTH = 0, 3 * FOX_WIDTH
_B_COL, _B_WIDTH = 3 * FOX_WIDTH + FOX_HEADS, 3 * SC_WIDTH + 3 * GDN_WIDTH
_C_COL, _C_WIDTH = _B_COL + _B_WIDTH + 2 * GDN_HEADS, GDN_WIDTH

_NT = (((1,), (1,)), ((), ()))
_TN = (((0,), (0,)), ((), ()))


def _params(sem):
    return pltpu.CompilerParams(dimension_semantics=sem, vmem_limit_bytes=VMEM_LIMIT_BYTES)


def _norm_cast_kernel(x_ref, g_ref, o_ref):
    x = x_ref[...]
    ms = jnp.mean(x * x, axis=-1, keepdims=True)
    o_ref[...] = (x * lax.rsqrt(ms + RMS_EPS) * g_ref[...]).astype(o_ref.dtype)


def _norm_cast(x, gain, tm=256):
    T, D = x.shape
    tm = min(tm, T)
    return pl.pallas_call(
        _norm_cast_kernel,
        out_shape=jax.ShapeDtypeStruct((T, D), BF16),
        grid=(T // tm,),
        in_specs=[pl.BlockSpec((tm, D), lambda i: (i, 0)),
                  pl.BlockSpec((1, D), lambda i: (0, 0))],
        out_specs=pl.BlockSpec((tm, D), lambda i: (i, 0)),
        compiler_params=_params(("parallel",)),
        name="norm_cast",
    )(x, gain.reshape(1, D))


def _norm_residual_kernel(y_ref, x_ref, g_ref, gn_ref, h_ref, *maybe_u_ref):
    y = y_ref[...].astype(F32)
    ms = jnp.mean(y * y, axis=-1, keepdims=True)
    h = x_ref[...] + y * lax.rsqrt(ms + RMS_EPS) * g_ref[...]
    h_ref[...] = h
    if maybe_u_ref:
        ms2 = jnp.mean(h * h, axis=-1, keepdims=True)
        maybe_u_ref[0][...] = (h * lax.rsqrt(ms2 + RMS_EPS) * gn_ref[...]).astype(BF16)


def _norm_residual(y, x, gain, next_gain, tm=256):
    T, D = x.shape
    tm = min(tm, T)
    emit_next = next_gain is not None
    gn = (next_gain if emit_next else gain).reshape(1, D)
    row = pl.BlockSpec((tm, D), lambda i: (i, 0))
    vec = pl.BlockSpec((1, D), lambda i: (0, 0))
    out_shape = [jax.ShapeDtypeStruct((T, D), F32)]
    out_specs = [row]
    if emit_next:
        out_shape.append(jax.ShapeDtypeStruct((T, D), BF16))
        out_specs.append(row)
    res = pl.pallas_call(
        _norm_residual_kernel,
        out_shape=out_shape,
        grid=(T // tm,),
        in_specs=[row, row, vec, vec],
        out_specs=out_specs,
        compiler_params=_params(("parallel",)),
        name="norm_residual",
    )(y, x, gain.reshape(1, D), gn)
    return (res[0], res[1]) if emit_next else (res[0], None)


def _window_t_kernel(a_ref, b_ref, *o_refs, off, depth):
    tn = o_refs[0].shape[0]
    kt = a_ref.shape[1] // depth
    for l in range(depth):
        x = a_ref[:, pl.ds(l, kt, stride=depth), :]
        if off:
            x = jnp.concatenate([x, b_ref[:, pl.ds(l, kt, stride=depth), :]], axis=0)[off:off + tn]
        xt = pltpu.einshape("nkl->knl", x)
        for k in range(kt):
            o_refs[l][:, k * HEAD_DIM:(k + 1) * HEAD_DIM] = xt[k].astype(BF16)


def _column_major_view(w_in):
    depth, d, p = w_in.shape
    kt = d // HEAD_DIM
    v = jnp.transpose(w_in, (2, 0, 1)).reshape(p, depth, kt, HEAD_DIM)
    return jnp.transpose(v, (0, 2, 1, 3)).reshape(p, kt * depth, HEAD_DIM)


def _window_t(w_view, depth, src_col, width, tn=256, tail=64):
    p, rows, lanes = w_view.shape
    d = rows // depth * lanes
    base, off = src_col // tn, src_col % tn
    assert width % tn == 0 and tn % tail == 0 and off <= tail
    return pl.pallas_call(
        functools.partial(_window_t_kernel, off=off, depth=depth),
        out_shape=[jax.ShapeDtypeStruct((width, d), BF16)] * depth,
        grid=(width // tn,),
        in_specs=[pl.BlockSpec((tn, rows, lanes), lambda j: (base + j, 0, 0)),
                  pl.BlockSpec((tail, rows, lanes), lambda j: ((base + j + 1) * (tn // tail), 0, 0))],
        out_specs=[pl.BlockSpec((tn, d), lambda j: (j, 0))] * depth,
        compiler_params=_params(("parallel",)),
        name="window_t",
    )(w_view, w_view)


class _CarriedCast(NamedTuple):
    w_stack: jax.Array
    layer: int


_BF16_SUBLANES = 16


def _carried_tile_rows(rows, steps):
    for t in range(_BF16_SUBLANES, rows, _BF16_SUBLANES):
        if rows % t == 0 and rows // t <= steps:
            return t
    return rows


def _call(kernel, carried, grid, in_specs, out_specs, out_shape, args, name):
    n_in, n_out = len(in_specs), len(out_specs)
    body = kernel
    if carried is not None:
        _, rows, cols = carried.w_stack.shape
        tile_rows = _carried_tile_rows(rows, math.prod(grid))
        n_tiles = rows // tile_rows

        def tile(*g):
            step = 0
            for size, idx in zip(grid, g):
                step = step * size + idx
            return jnp.minimum(step, n_tiles - 1)

        in_specs = in_specs + [pl.BlockSpec((1, tile_rows, cols),
                                            lambda *g: (carried.layer, tile(*g), 0))]
        out_specs = out_specs + [pl.BlockSpec((tile_rows, cols), lambda *g: (tile(*g), 0))]
        out_shape = out_shape + [jax.ShapeDtypeStruct((rows, cols), BF16)]
        args = args + [carried.w_stack]

        def body(*refs):
            src, dst = refs[n_in], refs[n_in + 1 + n_out]
            dst[...] = src[0].astype(dst.dtype)
            kernel(*refs[:n_in], *refs[n_in + 1:n_in + 1 + n_out], *refs[n_in + 2 + n_out:])

    return pl.pallas_call(
        body, out_shape=out_shape, grid=grid, in_specs=in_specs, out_specs=out_specs,
        compiler_params=_params(("arbitrary",) * len(grid)), name=name)(*args)


def _matmul_kernel(x_ref, w_ref, *rest, scaled, w_is_t):
    if w_is_t:
        acc = lax.dot_general(x_ref[...], w_ref[...], _NT, preferred_element_type=F32)
    else:
        acc = jnp.dot(x_ref[...], w_ref[...], preferred_element_type=F32)
    if scaled:
        s_ref, o_ref = rest
        acc = acc * s_ref[...]
    else:
        o_ref, = rest
    o_ref[...] = acc.astype(o_ref.dtype)


def _matmul(x, w, out_dtype, tm, tn, name, col_scale=None, carried=None, w_is_t=False,
            w_resident=False):
    M, K = x.shape
    N = w.shape[0] if w_is_t else w.shape[1]
    tm, tn = min(tm, M), min(tn, N)
    scaled = col_scale is not None
    if w_resident:
        grid, held = (N // tn, M // tm), dict(pipeline_mode=pl.Buffered(1))

        def at(f):
            return lambda j, i: f(i, j)
    else:
        grid, held = (M // tm, N // tn), {}

        def at(f):
            return f

    in_specs = [pl.BlockSpec((tm, K), at(lambda i, j: (i, 0))),
                pl.BlockSpec((tn, K), at(lambda i, j: (j, 0)), **held) if w_is_t
                else pl.BlockSpec((K, tn), at(lambda i, j: (0, j)), **held)]
    args = [x, w]
    if scaled:
        in_specs.append(pl.BlockSpec((1, tn), at(lambda i, j: (0, j))))
        args.append(col_scale)
    return _call(functools.partial(_matmul_kernel, scaled=scaled, w_is_t=w_is_t), carried,
                 grid, in_specs, [pl.BlockSpec((tm, tn), at(lambda i, j: (i, j)))],
                 [jax.ShapeDtypeStruct((M, N), out_dtype)], args, name)


def _out_proj_kernel(a1_ref, a2_ref, a3_ref, w1_ref, w2_ref, w3_ref, o_ref):
    acc = jnp.dot(a1_ref[...], w1_ref[...], preferred_element_type=F32)
    acc += jnp.dot(a2_ref[...], w2_ref[...], preferred_element_type=F32)
    acc += jnp.dot(a3_ref[...], w3_ref[...], preferred_element_type=F32)
    o_ref[...] = acc.astype(o_ref.dtype)


def _out_proj(a1, a2, a3, w1, w2, w3, carried, tm=1024, tn=1024):
    T = a1.shape[0]
    N = w1.shape[1]
    tm, tn = min(tm, T), min(tn, N)

    def lhs(a):
        return pl.BlockSpec((tm, a.shape[1]), lambda i, j: (i, 0))

    def rhs(w):
        return pl.BlockSpec((w.shape[0], tn), lambda i, j: (0, j))

    return _call(_out_proj_kernel, carried, (T // tm, N // tn),
                 [lhs(a1), lhs(a2), lhs(a3), rhs(w1), rhs(w2), rhs(w3)],
                 [pl.BlockSpec((tm, tn), lambda i, j: (i, j))],
                 [jax.ShapeDtypeStruct((T, N), BF16)], [a1, a2, a3, w1, w2, w3], "out_proj")


def _ffn_up_kernel(x_ref, wg_ref, wu_ref, o_ref):
    x = x_ref[...]
    g = jnp.dot(x, wg_ref[...], preferred_element_type=F32)
    u = jnp.dot(x, wu_ref[...], preferred_element_type=F32)
    o_ref[...] = (g * jax.nn.sigmoid(g) * u).astype(o_ref.dtype)


def _ffn_up(x, wg, wu, carried, tm=1024, tn=512):
    T, K = x.shape
    N = wg.shape[1]
    tm = min(tm, T)
    return _call(_ffn_up_kernel, carried, (T // tm, pl.cdiv(N, tn)),
                 [pl.BlockSpec((tm, K), lambda i, j: (i, 0)),
                  pl.BlockSpec((K, tn), lambda i, j: (0, j)),
                  pl.BlockSpec((K, tn), lambda i, j: (0, j))],
                 [pl.BlockSpec((tm, tn), lambda i, j: (i, j))],
                 [jax.ShapeDtypeStruct((T, N), BF16)], [x, wg, wu], "ffn_up")


def _gates_kernel(w_ref, u_ref, bias_ref, alog_ref, gt_ref, carry_ref, *, steps_per_seq, ts):
    t = pl.program_id(0)

    @pl.when(t % steps_per_seq == 0)
    def _():
        carry_ref[...] = jnp.zeros_like(carry_ref)

    z = lax.dot_general(w_ref[...], u_ref[...], _NT, preferred_element_type=F32) + bias_ref[...]
    row = lax.broadcasted_iota(jnp.int32, z.shape, 0)
    tail = jnp.log1p(jnp.exp(-jnp.abs(z)))
    log_sig = jnp.minimum(z, 0.0) - tail
    softplus = jnp.maximum(z, 0.0) + tail
    sig = 1.0 / (1.0 + jnp.exp(-z))
    decay = -jnp.exp(alog_ref[...]) * softplus
    val = jnp.where(row < FOX_HEADS, log_sig, jnp.where(row < FOX_HEADS + GDN_HEADS, decay, sig))

    src = lax.broadcasted_iota(jnp.int32, (ts, ts), 0)
    dst = lax.broadcasted_iota(jnp.int32, (ts, ts), 1)
    upper = src <= dst
    same_chunk = (src // CHUNK) == (dst // CHUNK)
    hi = val.astype(BF16)
    rest = val - hi.astype(F32)
    mid = rest.astype(BF16)
    lo = (rest - mid.astype(F32)).astype(BF16)
    pieces = jnp.concatenate([hi, mid, lo], axis=1)

    def prefix_sum(mask):
        ones = mask.astype(BF16)
        return jnp.dot(pieces, jnp.concatenate([ones, ones, ones], axis=0),
                       preferred_element_type=F32)

    cum_all = prefix_sum(upper)
    cum_chunk = prefix_sum(upper & same_chunk)
    cum_all = cum_all + carry_ref[...]
    carry_ref[...] = cum_all[:, ts - 1:ts]
    gt_ref[...] = jnp.where(row < FOX_HEADS, cum_all,
                            jnp.where(row < FOX_HEADS + GDN_HEADS, cum_chunk, val))


def _gates(u, w_small_t, bias_col, alog_col, seq_len, ts=512):
    T, D = u.shape
    ts = min(ts, seq_len)
    return pl.pallas_call(
        functools.partial(_gates_kernel, steps_per_seq=seq_len // ts, ts=ts),
        out_shape=jax.ShapeDtypeStruct((GATE_LANES, T), F32),
        grid=(T // ts,),
        in_specs=[pl.BlockSpec((GATE_LANES, D), lambda t: (0, 0)),
                  pl.BlockSpec((ts, D), lambda t: (t, 0)),
                  pl.BlockSpec((GATE_LANES, 1), lambda t: (0, 0)),
                  pl.BlockSpec((GATE_LANES, 1), lambda t: (0, 0))],
        out_specs=pl.BlockSpec((GATE_LANES, ts), lambda t: (0, t)),
        scratch_shapes=[pltpu.VMEM((GATE_LANES, 1), F32)],
        compiler_params=_params(("arbitrary",)),
        name="gates",
    )(w_small_t, u, bias_col, alog_col)


FOX_Q_SCALE = (HEAD_DIM ** -0.5) * math.log2(math.e)
FOX_PAIR = 2
_FOX_BUILD_ROWS = 512


def _fox_kernel(q_ref, qn_ref, k_ref, v_ref, gc_ref, g_ref, o_ref, kaug, vaug, m_ref, acc_ref,
                s0_ref, sa_ref, sb_ref, *, tq, tk, seq_len, pairs):
    D = HEAD_DIM
    pair = pl.program_id(0) % pairs
    qi = pl.program_id(1)

    @pl.when(qi == 0)
    def _build():
        ri = lax.broadcasted_iota(jnp.int32, (3 * D, FOX_PAIR * D), 0)
        ci = lax.broadcasted_iota(jnp.int32, (3 * D, FOX_PAIR * D), 1)
        sel = ri < 0
        for gg in range(FOX_PAIR):
            for j in range(3):
                sel |= (ri == j * D + pair * FOX_PAIR + gg) & (ci == gg * D + j)
        sel = sel.astype(BF16)
        ones = jnp.ones((_FOX_BUILD_ROWS, D), BF16)

        def chunk(i, carry):
            r0 = pl.multiple_of(i * _FOX_BUILD_ROWS, _FOX_BUILD_ROWS)
            rows = pl.ds(r0, _FOX_BUILD_ROWS)
            g = gc_ref[rows, :] * (-math.log2(math.e))
            hi = g.astype(BF16)
            r1 = g - hi.astype(F32)
            mid = r1.astype(BF16)
            lo = (r1 - mid.astype(F32)).astype(BF16)
            pieces = jnp.concatenate([hi, mid, lo], axis=1)
            aug = jnp.dot(pieces, sel, preferred_element_type=F32).astype(BF16)
            for gg in range(FOX_PAIR):
                kaug[gg, rows, 0:D] = k_ref[rows, gg * D:(gg + 1) * D]
                kaug[gg, rows, D:2 * D] = aug[:, gg * D:(gg + 1) * D]
                vaug[gg, rows, 0:D] = v_ref[rows, gg * D:(gg + 1) * D]
                vaug[gg, rows, D:2 * D] = ones
            return carry

        lax.fori_loop(0, seq_len // _FOX_BUILD_ROWS, chunk, 0, unroll=8)

    lane = lax.broadcasted_iota(jnp.int32, (tq, D), 1)
    ones3 = jnp.where(lane < 3, 1.0, 0.0).astype(BF16)
    m_ref[...] = jnp.full_like(m_ref, -jnp.inf)
    acc_ref[...] = jnp.zeros_like(acc_ref)

    def scores(kj, slot_ref, queries=q_ref, width=tk):
        k0 = pl.multiple_of(kj * tk, tk)
        for gg in range(FOX_PAIR):
            q_aug = jnp.concatenate([queries[:, gg * D:(gg + 1) * D], ones3], axis=1)
            slot_ref[gg, :, 0:width] = lax.dot_general(q_aug, kaug[gg, pl.ds(k0, width), :], _NT,
                                                       preferred_element_type=F32)

    def accumulate(kj, slot_ref, masked=False, width=tk):
        k0 = pl.multiple_of(kj * tk, tk)
        for gg in range(FOX_PAIR):
            s = slot_ref[gg, :, 0:width]
            if masked:
                r = qi * tq + lax.broadcasted_iota(jnp.int32, s.shape, 0)
                c = kj * tk + lax.broadcasted_iota(jnp.int32, s.shape, 1)
                s = jnp.where(c <= r, s, -jnp.inf)
            m_prev = m_ref[gg]
            m_new = jnp.maximum(m_prev, jnp.max(s, axis=-1, keepdims=True))
            alpha = jnp.exp2(m_prev - m_new)
            p = jnp.concatenate(
                [jnp.exp2(s[:, j * D:(j + 1) * D] - m_new) for j in range(width // D)],
                axis=1).astype(BF16)
            pv = jnp.dot(p, vaug[gg, pl.ds(k0, width), :], preferred_element_type=F32)
            acc_ref[gg] = jnp.concatenate([alpha, alpha], axis=1) * acc_ref[gg] + pv
            m_ref[gg] = m_new

    n_full = (qi * tq) // tk
    first_prefetching_step = tk // tq

    @pl.when(qi <= first_prefetching_step)
    def _():
        scores(0, s0_ref)

    ratio = tk // tq

    def diagonal_variants(cond, emit):
        for r in range(ratio):
            pl.when(cond & (qi % ratio == r))(functools.partial(emit, (r + 1) * tq))

    diagonal_variants(n_full == 0, lambda width: accumulate(0, s0_ref, masked=True, width=width))

    @pl.when(n_full >= 1)
    def _():
        scores(1, sa_ref)
        accumulate(0, s0_ref)

    def body(i, carry):
        scores(2 * i + 2, sb_ref)
        accumulate(2 * i + 1, sa_ref)
        scores(2 * i + 3, sa_ref)
        accumulate(2 * i + 2, sb_ref)
        return carry

    lax.fori_loop(0, (n_full - 1) // 2, body, 0)

    def odd_tail(width):
        scores(0, s0_ref, qn_ref)
        accumulate(n_full, sa_ref, masked=True, width=width)

    def even_tail(width):
        scores(n_full, sb_ref, width=width)
        accumulate(n_full - 1, sa_ref)
        scores(0, s0_ref, qn_ref)
        accumulate(n_full, sb_ref, masked=True, width=width)

    diagonal_variants((n_full >= 1) & (n_full % 2 == 1), odd_tail)
    diagonal_variants((n_full >= 1) & (n_full % 2 == 0), even_tail)

    for gg in range(FOX_PAIR):
        acc = acc_ref[gg]
        out = acc[:, 0:D] / acc[:, D:2 * D]
        ms = jnp.mean(out * out, axis=-1, keepdims=True)
        o_ref[:, gg * D:(gg + 1) * D] = (out * lax.rsqrt(ms + RMS_EPS) * g_ref[...]).astype(o_ref.dtype)


def _fox_attention(proj, gates_c, out_gain, batch, seq_len, tq=512, tk=1024):
    T = proj.shape[0]
    tq = min(tq, seq_len)
    tk = min(tk, seq_len)
    assert tk % tq == 0 and seq_len % tk == 0
    nq = seq_len // tq
    pairs = FOX_HEADS // FOX_PAIR
    W = FOX_PAIR * HEAD_DIM

    once = pl.Buffered(1)

    def rows_spec(base):
        return pl.BlockSpec((seq_len, W), lambda bp, qi: (bp // pairs, base // FOX_PAIR + bp % pairs),
                            pipeline_mode=once)

    return pl.pallas_call(
        functools.partial(_fox_kernel, tq=tq, tk=tk, seq_len=seq_len, pairs=pairs),
        out_shape=jax.ShapeDtypeStruct((T, FOX_WIDTH), BF16),
        grid=(batch * pairs, nq),
        in_specs=[pl.BlockSpec((tq, W), lambda bp, qi: ((bp // pairs) * nq + qi,
                                                         _FQ // FOX_PAIR + bp % pairs)),
                  pl.BlockSpec((tq, W), lambda bp, qi: ((bp // pairs) * nq + jnp.minimum(qi + 1, nq - 1),
                                                         _FQ // FOX_PAIR + bp % pairs)),
                  rows_spec(_FK), rows_spec(_FV),
                  pl.BlockSpec((seq_len, GATE_LANES), lambda bp, qi: (bp // pairs, 0),
                               pipeline_mode=once),
                  pl.BlockSpec((1, HEAD_DIM), lambda bp, qi: (0, 0))],
        out_specs=pl.BlockSpec((tq, W), lambda bp, qi: ((bp // pairs) * nq + qi, bp % pairs)),
        scratch_shapes=[pltpu.VMEM((FOX_PAIR, seq_len, 2 * HEAD_DIM), BF16),
                        pltpu.VMEM((FOX_PAIR, seq_len, 2 * HEAD_DIM), BF16),
                        pltpu.VMEM((FOX_PAIR, tq, HEAD_DIM), F32),
                        pltpu.VMEM((FOX_PAIR, tq, 2 * HEAD_DIM), F32),
                        pltpu.VMEM((FOX_PAIR, tq, tk), F32),
                        pltpu.VMEM((FOX_PAIR, tq, tk), F32),
                        pltpu.VMEM((FOX_PAIR, tq, tk), F32)],
        compiler_params=_params(("parallel", "arbitrary")),
        name="fox_attention",
    )(proj, proj, proj, proj, gates_c, out_gain.reshape(1, HEAD_DIM))


_HALO = 8


def _short_conv_kernel(b_ref, c_ref, h_ref, w_ref, o_ref, buf_ref, *, steps_per_seq, ts):
    t = pl.program_id(1)

    @pl.when(t % steps_per_seq == 0)
    def _():
        buf_ref[0:_HALO, :] = jnp.zeros((_HALO, buf_ref.shape[1]), F32)

    buf_ref[_HALO:_HALO + ts, :] = c_ref[...].astype(F32) * h_ref[...].astype(F32)
    w = w_ref[...]
    xa = buf_ref[...]
    y = w[2:3, :] * xa[_HALO:_HALO + ts, :]
    y += w[1:2, :] * pltpu.roll(xa, 1, axis=0)[_HALO:_HALO + ts, :]
    y += w[0:1, :] * pltpu.roll(xa, 2, axis=0)[_HALO:_HALO + ts, :]
    o_ref[...] = (b_ref[...].astype(F32) * y).astype(o_ref.dtype)
    buf_ref[0:_HALO, :] = buf_ref[ts:ts + _HALO, :]


def _short_conv(proj, conv_w, seq_len, ts=1024, tc=1024):
    T = proj.shape[0]
    ts = min(ts, seq_len)
    per = tc // HEAD_DIM

    def col(base):
        return pl.BlockSpec((ts, tc), lambda c, t: (t, base // per + c))

    return pl.pallas_call(
        functools.partial(_short_conv_kernel, steps_per_seq=seq_len // ts, ts=ts),
        out_shape=jax.ShapeDtypeStruct((T, SC_WIDTH), BF16),
        grid=(SC_WIDTH // tc, T // ts),
        in_specs=[col(_SB), col(_SC), col(_SH),
                  pl.BlockSpec((SC_KERNEL, tc), lambda c, t: (0, c))],
        out_specs=pl.BlockSpec((ts, tc), lambda c, t: (t, c)),
        scratch_shapes=[pltpu.VMEM((_HALO + ts, tc), F32)],
        compiler_params=_params(("parallel", "arbitrary")),
        name="short_conv",
    )(proj, proj, proj, conv_w)


def _bmm(a, b, dims):
    return lax.dot_general(a.astype(BF16), b.astype(BF16), dims, preferred_element_type=F32)


_B_NN = (((2,), (1,)), ((0,), (0,)))
_B_NT = (((2,), (2,)), ((0,), (0,)))


GDN_PAIR = 4


def _gdn_kernel(q_ref, k_ref, v_ref, z_ref, wq_ref, wk_ref, wv_ref, gc_ref, gr_ref, gain_ref,
                o_ref, qbuf, kbuf, vbuf, state_ref, obuf, *, rows):
    hp = pl.program_id(1)
    t = pl.program_id(2)
    n = rows // CHUNK
    D = HEAD_DIM

    @pl.when(t == 0)
    def _():
        zeros = jnp.zeros((_HALO, GDN_PAIR * D), F32)
        qbuf[0:_HALO, :] = zeros
        kbuf[0:_HALO, :] = zeros
        vbuf[0:_HALO, :] = zeros
        state_ref[...] = jnp.zeros_like(state_ref)

    def conv_silu(x_ref, w_ref, buf):
        buf[_HALO:_HALO + rows, :] = x_ref[...].astype(F32)
        w = w_ref[...]
        xa = buf[...]

        def delayed(j):
            return pltpu.roll(xa, j, axis=0)[_HALO:_HALO + rows, :]

        y = w[3:4, :] * xa[_HALO:_HALO + rows, :]
        y += w[2:3, :] * delayed(1)
        y += w[1:2, :] * delayed(2)
        y += w[0:1, :] * delayed(3)
        buf[0:_HALO, :] = buf[rows:rows + _HALO, :]
        return y * jax.nn.sigmoid(y)

    q_all = conv_silu(q_ref, wq_ref, qbuf)
    k_all = conv_silu(k_ref, wk_ref, kbuf)
    v_all = conv_silu(v_ref, wv_ref, vbuf)

    gates = gc_ref[...]
    lane = lax.broadcasted_iota(jnp.int32, gates.shape, 1)
    ri = lax.broadcasted_iota(jnp.int32, (CHUNK, CHUNK), 0)
    ci = lax.broadcasted_iota(jnp.int32, (CHUNK, CHUNK), 1)
    tri_incl = (ci <= ri)[None]
    tri_strict = (ci < ri)[None]
    same16 = ((ri // 16) == (ci // 16))[None]
    same32 = ((ri // 32) == (ci // 32))[None]
    eye = (ri == ci).astype(F32)[None]

    def chunk_terms(gg):
        h = hp * GDN_PAIR + gg
        cols = slice(gg * D, (gg + 1) * D)
        q, k, v = q_all[:, cols], k_all[:, cols], v_all[:, cols]
        q = q * (lax.rsqrt(jnp.sum(q * q, axis=-1, keepdims=True) + RMS_EPS) * (D ** -0.5))
        k = k * lax.rsqrt(jnp.sum(k * k, axis=-1, keepdims=True) + RMS_EPS)
        g_cum = jnp.sum(jnp.where(lane == FOX_HEADS + h, gates, 0.0), axis=-1, keepdims=True)
        beta = jnp.sum(jnp.where(lane == FOX_HEADS + GDN_HEADS + h, gates, 0.0), axis=-1,
                       keepdims=True)
        q3 = q.reshape(n, CHUNK, D)
        k3 = k.reshape(n, CHUNK, D)
        v3 = v.reshape(n, CHUNK, D)
        g3 = g_cum.reshape(n, CHUNK, 1)
        b3 = beta.reshape(n, CHUNK, 1)
        g_row = gr_ref[:, pl.ds(FOX_HEADS + h, 1), :]
        g_last = g3[:, CHUNK - 1:CHUNK, :]

        decay = jnp.exp(jnp.where(tri_incl, g3 - g_row, -jnp.inf))
        kk = _bmm(k3, k3, _B_NT)
        L = jnp.where(tri_strict, b3 * kk * decay, 0.0)

        P = jnp.where(same16, L, 0.0)
        X = eye - P
        P2 = _bmm(P, P, _B_NN)
        X = _bmm(X, eye + P2, _B_NN)
        P4 = _bmm(P2, P2, _B_NN)
        X = _bmm(X, eye + P4, _B_NN)
        P8 = _bmm(P4, P4, _B_NN)
        X = _bmm(X, eye + P8, _B_NN)
        O32 = jnp.where(same32 & jnp.logical_not(same16), L, 0.0)
        X = X - _bmm(_bmm(X, O32, _B_NN), X, _B_NN)
        O64 = jnp.where(same32, 0.0, L)
        X = X - _bmm(_bmm(X, O64, _B_NN), X, _B_NN)

        e3 = jnp.exp(g3)
        rhs = jnp.concatenate([v3 * b3, k3 * (b3 * e3)], axis=-1)
        sol = _bmm(X, rhs, _B_NN)
        attn = jnp.where(tri_incl, _bmm(q3, k3, _B_NT) * decay, 0.0)
        return dict(u=sol[:, :, :D], w=sol[:, :, D:].astype(BF16), q=(q3 * e3).astype(BF16),
                    a=attn.astype(BF16), k=(k3 * jnp.exp(g_last - g3)).astype(BF16),
                    gl=jnp.exp(g_last))

    def pair_terms(p):
        t0, t1 = chunk_terms(2 * p), chunk_terms(2 * p + 1)
        return dict(
            u=jnp.concatenate([t0["u"], t1["u"]], axis=-1),
            wq=jnp.concatenate([jnp.concatenate([t0["w"], t1["w"]], axis=-1),
                                jnp.concatenate([t0["q"], t1["q"]], axis=-1)], axis=1),
            a=jnp.concatenate([t0["a"], t1["a"]], axis=-1),
            k=jnp.concatenate([t0["k"], t1["k"]], axis=-1),
            gl=jnp.concatenate([jnp.broadcast_to(t0["gl"], (n, 1, D)),
                                jnp.broadcast_to(t1["gl"], (n, 1, D))], axis=-1))

    def block_diag(x):
        first = lax.broadcasted_iota(jnp.int32, x.shape, 1) < D
        zero = jnp.zeros_like(x)
        return jnp.concatenate([jnp.where(first, x, zero), jnp.where(first, zero, x)], axis=0)

    n_pairs = GDN_PAIR // 2
    terms = [pair_terms(p) for p in range(n_pairs)]
    S = [state_ref[:, p * 2 * D:(p + 1) * 2 * D] for p in range(n_pairs)]
    for c in range(n):
        for p in range(n_pairs):
            tm = terms[p]
            r1 = jnp.dot(tm["wq"][c], block_diag(S[p].astype(BF16)), preferred_element_type=F32)
            v_b = (tm["u"][c] - r1[0:CHUNK]).astype(BF16)
            o_c = r1[CHUNK:2 * CHUNK] + jnp.dot(tm["a"][c], block_diag(v_b),
                                                preferred_element_type=F32)
            kv = lax.dot_general(tm["k"][c], v_b, _TN, preferred_element_type=F32)
            S[p] = S[p] * tm["gl"][c] + jnp.concatenate([kv[0:D, 0:D], kv[D:2 * D, D:2 * D]],
                                                         axis=1)
            obuf[c * CHUNK:(c + 1) * CHUNK, p * 2 * D:(p + 1) * 2 * D] = o_c
    for p in range(n_pairs):
        state_ref[:, p * 2 * D:(p + 1) * 2 * D] = S[p]

    for gg in range(GDN_PAIR):
        cols = slice(gg * D, (gg + 1) * D)
        o = obuf[:, cols]
        ms = jnp.mean(o * o, axis=-1, keepdims=True)
        z = z_ref[:, cols].astype(F32)
        o = o * lax.rsqrt(ms + RMS_EPS) * gain_ref[...] * (z * jax.nn.sigmoid(z))
        o_ref[:, cols] = o.astype(o_ref.dtype)


def _gdn(proj, proj_z, conv_w, gates_c, gates_r, out_gain, batch, seq_len, rows=1024):
    T = proj.shape[0]
    rows = min(rows, seq_len)
    nt = seq_len // rows
    n = rows // CHUNK
    pairs = GDN_HEADS // GDN_PAIR
    W = GDN_PAIR * HEAD_DIM

    def col(base):
        return pl.BlockSpec((rows, W), lambda b, hp, t: (b * nt + t, base // GDN_PAIR + hp))

    def wcol(base):
        return pl.BlockSpec((GDN_CONV, W), lambda b, hp, t: (0, base // GDN_PAIR + hp))

    return pl.pallas_call(
        functools.partial(_gdn_kernel, rows=rows),
        out_shape=jax.ShapeDtypeStruct((T, GDN_WIDTH), BF16),
        grid=(batch, pairs, nt),
        in_specs=[col(_GQ), col(_GK), col(_GV), col(_GZ),
                  wcol(0), wcol(GDN_HEADS), wcol(2 * GDN_HEADS),
                  pl.BlockSpec((rows, GATE_LANES), lambda b, hp, t: (b * nt + t, 0)),
                  pl.BlockSpec((n, GATE_LANES, CHUNK), lambda b, hp, t: (b * nt + t, 0, 0)),
                  pl.BlockSpec((1, HEAD_DIM), lambda b, hp, t: (0, 0))],
        out_specs=pl.BlockSpec((rows, W), lambda b, hp, t: (b * nt + t, hp)),
        scratch_shapes=[pltpu.VMEM((_HALO + rows, W), F32),
                        pltpu.VMEM((_HALO + rows, W), F32),
                        pltpu.VMEM((_HALO + rows, W), F32),
                        pltpu.VMEM((HEAD_DIM, W), F32),
                        pltpu.VMEM((rows, W), F32)],
        compiler_params=_params(("parallel", "parallel", "arbitrary")),
        name="gdn",
    )(proj, proj, proj, proj_z, conv_w, conv_w, conv_w, gates_c, gates_r,
      out_gain.reshape(1, HEAD_DIM))


_GATE_F_ROWS, _GATE_AB_ROWS = 16, 64


def _gate_rows_kernel(f_ref, ab_ref, *o_refs, depth):
    kt = f_ref.shape[1] // depth
    row = lax.broadcasted_iota(jnp.int32, (GATE_LANES, HEAD_DIM), 0)
    for l in range(depth):
        f = pltpu.einshape("nkl->knl", f_ref[:, pl.ds(l, kt, stride=depth), :])
        ab = pltpu.einshape("nkl->knl", ab_ref[:, pl.ds(l, kt, stride=depth), :])
        for k in range(kt):
            fk = jnp.concatenate([f[k], jnp.zeros((GATE_LANES - _GATE_F_ROWS, HEAD_DIM), F32)], axis=0)
            abk = jnp.concatenate([ab[k], jnp.zeros((GATE_LANES - _GATE_AB_ROWS, HEAD_DIM), F32)],
                                  axis=0)
            w = jnp.where(row < FOX_HEADS, fk,
                          jnp.where(row < FOX_HEADS + 2 * GDN_HEADS, abk, 0.0))
            o_refs[l][:, k * HEAD_DIM:(k + 1) * HEAD_DIM] = w.astype(BF16)


def _small_gate_weights_t(w_view, depth):
    p, rows, lanes = w_view.shape
    d = rows // depth * lanes
    f_col, ab_col = _A_WIDTH, _B_COL + _B_WIDTH
    assert f_col % _GATE_F_ROWS == 0 and ab_col % _GATE_AB_ROWS == FOX_HEADS
    return pl.pallas_call(
        functools.partial(_gate_rows_kernel, depth=depth),
        out_shape=[jax.ShapeDtypeStruct((GATE_LANES, d), BF16)] * depth,
        grid=(1,),
        in_specs=[pl.BlockSpec((_GATE_F_ROWS, rows, lanes), lambda i: (f_col // _GATE_F_ROWS, 0, 0)),
                  pl.BlockSpec((_GATE_AB_ROWS, rows, lanes), lambda i: (ab_col // _GATE_AB_ROWS, 0, 0))],
        out_specs=[pl.BlockSpec((GATE_LANES, d), lambda i: (0, 0))] * depth,
        compiler_params=_params(("arbitrary",)),
        name="gate_rows",
    )(w_view, w_view)


def _gate_columns(fox_forget_bias, gdn_a_log, gdn_dt_bias):
    pad = GATE_LANES - FOX_HEADS - GDN_HEADS
    bias = jnp.concatenate([fox_forget_bias, gdn_dt_bias, jnp.zeros((pad,), F32)])
    alog = jnp.concatenate([jnp.zeros((FOX_HEADS,), F32), gdn_a_log, jnp.zeros((pad,), F32)])
    return bias.reshape(GATE_LANES, 1), alog.reshape(GATE_LANES, 1)


def _layer(x, u, batch, seq_len, layer, w_in, fox_forget_bias, fox_out_norm, sc_conv_w, gdn_conv_w,
           gdn_a_log, gdn_dt_bias, gdn_out_norm, w_out, mix_post_norm, ffn_pre_norm,
           w_gate, w_up, w_down, ffn_post_norm, next_pre_norm):
    T, D = x.shape
    wa_t, wb_t, wc_t, w_small_t = w_in
    bias_col, alog_col = _gate_columns(fox_forget_bias, gdn_a_log, gdn_dt_bias)

    q_scale = jnp.concatenate([jnp.full((FOX_WIDTH,), FOX_Q_SCALE, F32),
                               jnp.ones((_A_WIDTH - FOX_WIDTH,), F32)]).reshape(1, _A_WIDTH)
    proj_a, w_out_b = _matmul(u, wa_t, BF16, tm=1024, tn=768, name="in_proj_a", w_is_t=True,
                              w_resident=True,
                              col_scale=q_scale, carried=_CarriedCast(w_out, layer))
    proj_b, wg = _matmul(u, wb_t, BF16, tm=1024, tn=768, name="in_proj_b", w_is_t=True,
                         w_resident=True,
                         carried=_CarriedCast(w_gate, layer))
    proj_c, = _matmul(u, wc_t, BF16, tm=1024, tn=768, name="in_proj_c", w_is_t=True)
    gates_t = _gates(u, w_small_t, bias_col, alog_col, seq_len)
    gates_c = gates_t.T
    gates_r = gates_t.reshape(GATE_LANES, T // CHUNK, CHUNK).transpose(1, 0, 2)

    fox_out = _fox_attention(proj_a, gates_c, fox_out_norm, batch, seq_len)
    sc_out = _short_conv(proj_b, sc_conv_w, seq_len)
    gdn_out = _gdn(proj_b, proj_c, gdn_conv_w, gates_c, gates_r, gdn_out_norm, batch, seq_len)

    y, wu = _out_proj(fox_out, sc_out, gdn_out,
                      w_out_b[:FOX_WIDTH], w_out_b[FOX_WIDTH:FOX_WIDTH + SC_WIDTH],
                      w_out_b[FOX_WIDTH + SC_WIDTH:], _CarriedCast(w_up, layer))
    h, v = _norm_residual(y, x, mix_post_norm, ffn_pre_norm)

    act, wd = _ffn_up(v, wg, wu, _CarriedCast(w_down, layer), tm=2048, tn=256)
    y2, = _matmul(act, wd, BF16, tm=512, tn=1024, name="ffn_down", w_resident=True)
    return _norm_residual(y2, h, ffn_post_norm, next_pre_norm)


def kernel(x, mix_pre_norm, w_in, fox_forget_bias, fox_out_norm, sc_conv_w, gdn_conv_w, gdn_a_log,
           gdn_dt_bias, gdn_out_norm, w_out, mix_post_norm, ffn_pre_norm, w_gate, w_up, w_down,
           ffn_post_norm):
    B, S, D = x.shape
    depth = w_in.shape[0]
    h = x.reshape(B * S, D)
    u = _norm_cast(h, mix_pre_norm[0])
    w_view = _column_major_view(w_in)
    windows = [_window_t(w_view, depth, col, width)
               for col, width in ((_A_COL, _A_WIDTH), (_B_COL, _B_WIDTH), (_C_COL, _C_WIDTH))]
    windows.append(_small_gate_weights_t(w_view, depth))
    for l in range(depth):
        nxt = mix_pre_norm[l + 1] if l + 1 < depth else None
        w_in_l = tuple(win[l] for win in windows)
        h, u = _layer(h, u, B, S, l, w_in_l, fox_forget_bias[l], fox_out_norm[l], sc_conv_w[l],
                      gdn_conv_w[l], gdn_a_log[l], gdn_dt_bias[l], gdn_out_norm[l], w_out,
                      mix_post_norm[l], ffn_pre_norm[l], w_gate, w_up, w_down,
                      ffn_post_norm[l], nxt)
    return h.reshape(B, S, D)
```
